```python
import jax
import jax.numpy as jnp
from jax import lax
import numpy as np

D_MODEL = 1024
BATCH = 8
SEQ = 2048
DEPTH = 2

GRID_W = 64
CTX_LEN = 256
N_EVEN = (DEPTH + 1) // 2
N_ODD = DEPTH // 2
N_MOD = 6
EPS = 1e-6

RET_HEADS = 4
RET_DK = 64
RET_DV = 128
RET_QK = RET_HEADS * RET_DK
RET_V = RET_HEADS * RET_DV
RET_CHUNK = 128

RWKV_HEADS = 8
RWKV_N = 64
RWKV_W = RWKV_HEADS * RWKV_N
DECAY_RANK = 64
ICLR_RANK = 64
GATE_RANK = 160
GN_EPS = 64e-5
SHIFT_W = 3 * RWKV_W + 2 * DECAY_RANK + 2 * ICLR_RANK + GATE_RANK
EVEN_IN = 2 * RET_QK + 2 * RET_V + SHIFT_W
EVEN_OUT = RET_V + RWKV_W

MLA_HEADS = 8
Q_RANK = 384
KV_RANK = 256
QK_NOPE = 128
QK_ROPE = 64
V_HEAD = 128
ODD_IN = Q_RANK + KV_RANK + QK_ROPE
MLA_SCALE = (QK_NOPE + QK_ROPE) ** -0.5
ROPE_BASE = 10000.0
ATTN_BLOCK = 128

N_GROUPS = 4
EXPERTS_PER_GROUP = 8
N_EXPERTS = N_GROUPS * EXPERTS_PER_GROUP
EXPERT_HIDDEN = 256
TOP_K = 2

kernel_name = 'hybrid_retention_rwkv7_mla_hmoe_dit'


def split_cols(t, sizes):
    return jnp.split(t, np.cumsum(sizes)[:-1].tolist(), axis=-1)


def rmsnorm(x, g):
    xf = x.astype(jnp.float32)
    y = xf * lax.rsqrt(jnp.mean(xf * xf, axis=-1, keepdims=True) + EPS)
    return (y * g).astype(x.dtype)


def modulate(x, shift, scale):
    return x * (1.0 + scale[:, None]) + shift[:, None]


def centred_shift(y, mu):
    prev = jnp.pad(y[:, :-1], ((0, 0), (1, 0), (0, 0)))
    nxt = jnp.pad(y[:, 1:], ((0, 0), (0, 1), (0, 0)))
    return y + mu[0] * (prev - y) + mu[1] * (nxt - y)


def retention_chunkwise(q, k, v, log_gamma, state0):
    b, h, l, dk = q.shape
    dv = v.shape[-1]
    n, cs = l // RET_CHUNK, RET_CHUNK
    qc = q.reshape(b, h, n, cs, dk)
    kc = k.reshape(b, h, n, cs, dk)
    vc = v.reshape(b, h, n, cs, dv)
    pos = jnp.arange(cs, dtype=jnp.float32)
    lg = log_gamma[:, None]
    rel = pos[:, None] - pos[None, :]
    dmask = jnp.where(rel >= 0, jnp.exp(log_gamma[:, None, None] * jnp.maximum(rel, 0.0)), 0.0)
    scores = jnp.einsum('bhnid,bhnjd->bhnij', qc, kc) * dmask[None, :, None]
    o_inner = jnp.einsum('bhnij,bhnjv->bhniv', scores, vc)
    k_dec = jnp.exp(lg * (cs - 1.0 - pos))
    delta = jnp.einsum('bhnjd,bhnjv->nbhdv', kc * k_dec[None, :, None, :, None], vc)
    chunk_decay = jnp.exp(log_gamma * cs)[None, :, None, None]

    def step(s, d):
        return s * chunk_decay + d, s

    s_final, s_prev = lax.scan(step, state0, delta)
    q_dec = jnp.exp(lg * (pos + 1.0))
    o_cross = jnp.einsum('bhnid,nbhdv->bhniv', qc * q_dec[None, :, None, :, None], s_prev)
    return (o_inner + o_cross).reshape(b, h, l, dv), s_final


def bidir_retention(qkv_c, qkv_l, log_gamma):
    qc, kc, vc = qkv_c
    ql, kl, vl = qkv_l
    s0 = jnp.zeros(qc.shape[:2] + (RET_DK, RET_DV), jnp.float32)
    flip = lambda t: jnp.flip(t, axis=2)
    o_cf, s_cf = retention_chunkwise(qc, kc, vc, log_gamma, s0)
    o_lf, _ = retention_chunkwise(ql, kl, vl, log_gamma, s_cf)
    o_cb, s_cb = retention_chunkwise(flip(qc), flip(kc), flip(vc), log_gamma, s0)
    o_lb, _ = retention_chunkwise(flip(ql), flip(kl), flip(vl), log_gamma, s_cb)
    diag = lambda q, k, v: jnp.sum(q * k, axis=-1, keepdims=True) * v
    return (o_cf + flip(o_cb) - diag(qc, kc, vc), o_lf + flip(o_lb) - diag(ql, kl, vl))


def rwkv7_scan(s0, r, w, kk, a, kt, v, reverse):
    def step(s, inp):
        r_t, w_t, kk_t, a_t, k_t, v_t = inp
        sa = jnp.einsum('bhvk,bhk->bhv', s, kk_t)
        s = (s * w_t[:, :, None, :] - sa[..., None] * (kk_t * a_t)[:, :, None, :]
             + v_t[..., None] * k_t[:, :, None, :])
        return s, jnp.einsum('bhvk,bhk->bhv', s, r_t)

    xs = tuple(jnp.swapaxes(t, 0, 1) for t in (r, w, kk, a, kt, v))
    s_final, ys = lax.scan(step, s0, xs, reverse=reverse)
    return jnp.swapaxes(ys, 0, 1), s_final


def bidir_rwkv7(seq_c, seq_l):
    rc, kkc, vc, dc = seq_c
    rl, kkl, vl, dl = seq_l
    s0 = jnp.zeros(rc.shape[:1] + (RWKV_HEADS, RWKV_N, RWKV_N), jnp.float32)
    ys_c, ys_l = [], []
    for direction in range(2):
        rev = direction == 1
        wc, ac, ktc = dc[direction]
        y_c, s_c = rwkv7_scan(s0, rc, wc, kkc, ac, ktc, vc, rev)
        wl, al, ktl = dl[direction]
        y_l, _ = rwkv7_scan(s_c, rl, wl, kkl, al, ktl, vl, rev)
        ys_c.append(y_c)
        ys_l.append(y_l)
    return ys_c[0] + ys_c[1], ys_l[0] + ys_l[1]


def even_mixer(xc, xl, w_in, shift_mu, w0, w2, a0, a2, g2, k_k, k_a, r_k, lnx_w, lnx_b, w_out):
    log_gamma = jnp.log1p(-jnp.exp2(-5.0 - jnp.arange(RET_HEADS, dtype=jnp.float32)))

    def features(x):
        b, l, _ = x.shape
        p = (x @ w_in).astype(jnp.float32)
        q, k, v, g, rw = split_cols(p, [RET_QK, RET_QK, RET_V, RET_V, SHIFT_W])
        to_heads = lambda t: t.reshape(b, l, RET_HEADS, -1).transpose(0, 2, 1, 3)
        ret = (to_heads(q) * RET_DK ** -0.5, to_heads(k), to_heads(v), g)
        rw = centred_shift(rw, shift_mu)
        r, kr, vr, wd_f, wd_b, ad_f, ad_b, gd = split_cols(
            rw, [RWKV_W] * 3 + [DECAY_RANK] * 2 + [ICLR_RANK] * 2 + [GATE_RANK])
        hd = lambda t: t.reshape(b, l, RWKV_HEADS, RWKV_N)
        r, kr, vr = hd(r), hd(kr), hd(vr)
        kk = kr * k_k.reshape(RWKV_HEADS, RWKV_N)
        kk = kk / jnp.maximum(jnp.linalg.norm(kk, axis=-1, keepdims=True), 1e-12)
        dirs = []
        for d, (wd, ad) in enumerate(((wd_f, ad_f), (wd_b, ad_b))):
            wlog = -jax.nn.softplus(-(w0[d] + jnp.tanh(wd) @ w2[d])) - 0.5
            decay = hd(jnp.exp(-jnp.exp(wlog)))
            a = hd(jax.nn.sigmoid(a0[d] + ad @ a2[d]))
            kt = kr * (1.0 + (a - 1.0) * k_a.reshape(RWKV_HEADS, RWKV_N))
            dirs.append((decay, a, kt))
        gate = jax.nn.sigmoid(gd) @ g2
        return ret, (r, kk, vr, dirs, gate)

    def ret_out(o, g):
        o = o * lax.rsqrt(jnp.mean(o * o, axis=-1, keepdims=True) + EPS)
        bb, hh, ll, dv = o.shape
        return o.transpose(0, 2, 1, 3).reshape(bb, ll, hh * dv) * jax.nn.silu(g)

    def rwkv_out(y, feats):
        r, kk, v, dirs, gate = feats
        bb, ll = r.shape[:2]
        mu = jnp.mean(y, axis=-1, keepdims=True)
        var = jnp.mean(jnp.square(y - mu), axis=-1, keepdims=True)
        yn = ((y - mu) * lax.rsqrt(var + GN_EPS) * lnx_w.reshape(RWKV_HEADS, RWKV_N)
              + lnx_b.reshape(RWKV_HEADS, RWKV_N))
        bonus = jnp.sum(r * (dirs[0][2] + dirs[1][2]) * r_k, axis=-1, keepdims=True) * v
        return (yn + bonus).reshape(bb, ll, RWKV_W) * gate

    (qc, kc, vc, gc), rwc = features(xc)
    (ql, kl, vl, gl), rwl = features(xl)
    oc, ol = bidir_retention((qc, kc, vc), (ql, kl, vl), log_gamma)
    yc, yl = bidir_rwkv7(rwc[:4], rwl[:4])
    out_c = jnp.concatenate([ret_out(oc, gc), rwkv_out(yc, rwc)], axis=-1).astype(xc.dtype) @ w_out
    out_l = jnp.concatenate([ret_out(ol, gl), rwkv_out(yl, rwl)], axis=-1).astype(xl.dtype) @ w_out
    return out_c, out_l


def axial_rope_angles(length):
    rows = length // GRID_W
    row = jnp.repeat(jnp.arange(rows, dtype=jnp.float32), GRID_W)
    col = jnp.tile(jnp.arange(GRID_W, dtype=jnp.float32), rows)
    n_freq = QK_ROPE // 4
    inv_freq = ROPE_BASE ** (-jnp.arange(n_freq, dtype=jnp.float32) / n_freq)
    return row[:, None] * inv_freq, col[:, None] * inv_freq


def rotate_half_axis(t, ang):
    cos = jnp.cos(ang)[None, :, None, :]
    sin = jnp.sin(ang)[None, :, None, :]
    t1, t2 = jnp.split(t, 2, axis=-1)
    return jnp.concatenate([t1 * cos - t2 * sin, t2 * cos + t1 * sin], axis=-1)


def apply_axial_rope(t, ang_row, ang_col):
    half = t.shape[-1] // 2
    out = jnp.concatenate([rotate_half_axis(t[..., :half], ang_row),
                           rotate_half_axis(t[..., half:], ang_col)], axis=-1)
    return out.astype(t.dtype)


def softmax_attention(q, k, v):
    s = jnp.einsum('bqhd,bkhd->bhqk', q, k).astype(jnp.float32) * MLA_SCALE
    p = jax.nn.softmax(s, axis=-1).astype(v.dtype)
    return jnp.einsum('bhqk,bkhd->bqhd', p, v)


def blocked_attention(q, k, v):
    b, l, h, d = q.shape
    qb = jnp.swapaxes(q.reshape(b, l // ATTN_BLOCK, ATTN_BLOCK, h, d), 0, 1)
    ob = lax.map(lambda blk: softmax_attention(blk, k, v), qb)
    return jnp.swapaxes(ob, 0, 1).reshape(b, l, h, v.shape[-1])


def mla_mixer(xc, xl, w_in, q_norm_g, w_uq, kv_norm_g, w_ukv, w_o, need_ctx):
    b, l, _ = xl.shape
    ang_r, ang_c = axial_rope_angles(l)

    def project_q(x, rotate):
        bb, ll, _ = x.shape
        c_q = rmsnorm(x @ w_in[:, :Q_RANK], q_norm_g)
        q = (c_q @ w_uq).reshape(bb, ll, MLA_HEADS, QK_NOPE + QK_ROPE)
        q_nope, q_pe = q[..., :QK_NOPE], q[..., QK_NOPE:]
        if rotate:
            q_pe = apply_axial_rope(q_pe, ang_r, ang_c)
        return jnp.concatenate([q_nope, q_pe], axis=-1)

    def project_kv(x, rotate):
        bb, ll, _ = x.shape
        kv_a = x @ w_in[:, Q_RANK:]
        c_kv = rmsnorm(kv_a[..., :KV_RANK], kv_norm_g)
        k_pe = kv_a[..., KV_RANK:][:, :, None, :]
        if rotate:
            k_pe = apply_axial_rope(k_pe, ang_r, ang_c)
        kv = (c_kv @ w_ukv).reshape(bb, ll, MLA_HEADS, QK_NOPE + V_HEAD)
        k = jnp.concatenate([kv[..., :QK_NOPE],
                             jnp.broadcast_to(k_pe, (bb, ll, MLA_HEADS, QK_ROPE))], axis=-1)
        return k, kv[..., QK_NOPE:]

    kc, vc = project_kv(xc, False)
    kl, vl = project_kv(xl, True)
    ql = project_q(xl, True)
    k_all = jnp.concatenate([kc, kl], axis=1)
    v_all = jnp.concatenate([vc, vl], axis=1)
    out_l = blocked_attention(ql, k_all, v_all).reshape(b, l, MLA_HEADS * V_HEAD) @ w_o
    out_c = None
    if need_ctx:
        qc = project_q(xc, False)
        out_c = softmax_attention(qc, kc, vc).reshape(b, xc.shape[1], MLA_HEADS * V_HEAD) @ w_o
    return out_c, out_l


def hier_moe(x, w_grp, b_grp, w_exp, b_exp, w_gate, w_up, w_down):
    xf = x.astype(jnp.float32)
    p_grp = jax.nn.softmax(xf @ w_grp.astype(jnp.float32) + b_grp.astype(jnp.float32), axis=-1)
    pg, g_idx = lax.top_k(p_grp, 1)
    e_logits = (xf @ w_exp.astype(jnp.float32) + b_exp.astype(jnp.float32)).reshape(
        -1, N_GROUPS, EXPERTS_PER_GROUP)
    e_sel = jnp.einsum('tg,tge->te', jax.nn.one_hot(g_idx[:, 0], N_GROUPS, dtype=jnp.float32), e_logits)
    pe, e_idx = lax.top_k(jax.nn.softmax(e_sel, axis=-1), TOP_K)
    wts = pg * pe / jnp.sum(pe, axis=-1, keepdims=True)
    flat = g_idx * EXPERTS_PER_GROUP + e_idx
    gates = jnp.einsum('tk,tke->te', wts, jax.nn.one_hot(flat, N_EXPERTS, dtype=jnp.float32)).astype(x.dtype)
    y = jnp.zeros_like(x)
    for e in range(N_EXPERTS):
        hidden = jax.nn.silu(x @ w_gate[e]) * (x @ w_up[e])
        y = y + gates[:, e:e + 1] * (hidden @ w_down[e])
    return y


def setup_inputs(seed: int = 0) -> dict:
    key = jax.random.key(seed)
    keys = iter(jax.random.split(key, 40))

    def normal(shape, scale):
        return jax.random.normal(next(keys), shape, jnp.float32) * scale

    def uniform(shape, lo, hi):
        return jax.random.uniform(next(keys), shape, jnp.float32, lo, hi)

    def gain(shape):
        return 1.0 + normal(shape, 0.05)

    d = D_MODEL
    return {
        'x': normal((BATCH, SEQ, d), 1.0),
        'c': normal((BATCH, d), 1.0),
        'ctx': normal((BATCH, CTX_LEN, d), 1.0),
        'c_ctx': normal((d,), 1.0),
        'ada_w': normal((DEPTH, d, N_MOD * d), 0.5 * d ** -0.5),
        'ada_b': normal((DEPTH, N_MOD * d), 0.02),
        'norm1_g': gain((DEPTH, d)),
        'norm2_g': gain((DEPTH, d)),
        'final_g': gain((d,)),
        'ev_w_in': normal((N_EVEN, d, EVEN_IN), d ** -0.5),
        'ev_shift_mu': uniform((N_EVEN, 2, SHIFT_W), 0.0, 0.5),
        'rwkv_w0': uniform((N_EVEN, 2, RWKV_W), -6.0, 0.0),
        'rwkv_w2': normal((N_EVEN, 2, DECAY_RANK, RWKV_W), 0.5 * DECAY_RANK ** -0.5),
        'rwkv_a0': normal((N_EVEN, 2, RWKV_W), 0.5),
        'rwkv_a2': normal((N_EVEN, 2, ICLR_RANK, RWKV_W), 0.5 * ICLR_RANK ** -0.5),
        'rwkv_g2': normal((N_EVEN, GATE_RANK, RWKV_W), GATE_RANK ** -0.5),
        'rwkv_k_k': 0.85 + normal((N_EVEN, RWKV_W), 0.1),
        'rwkv_k_a': uniform((N_EVEN, RWKV_W), 0.5, 1.0),
        'rwkv_r_k': normal((N_EVEN, RWKV_HEADS, RWKV_N), 0.1),
        'rwkv_lnx_w': gain((N_EVEN, RWKV_W)),
        'rwkv_lnx_b': normal((N_EVEN, RWKV_W), 0.02),
        'ev_w_out': normal((N_EVEN, EVEN_OUT, d), EVEN_OUT ** -0.5),
        'mla_w_in': normal((N_ODD, d, ODD_IN), d ** -0.5),
        'mla_q_norm_g': gain((N_ODD, Q_RANK)),
        'mla_w_uq': normal((N_ODD, Q_RANK, MLA_HEADS * (QK_NOPE + QK_ROPE)), Q_RANK ** -0.5),
        'mla_kv_norm_g': gain((N_ODD, KV_RANK)),
        'mla_w_ukv': normal((N_ODD, KV_RANK, MLA_HEADS * (QK_NOPE + V_HEAD)), KV_RANK ** -0.5),
        'mla_w_o': normal((N_ODD, MLA_HEADS * V_HEAD, d), (MLA_HEADS * V_HEAD) ** -0.5),
        'moe_w_grp': normal((DEPTH, d, N_GROUPS), d ** -0.5),
        'moe_b_grp': normal((DEPTH, N_GROUPS), 0.01),
        'moe_w_exp': normal((DEPTH, d, N_EXPERTS), d ** -0.5),
        'moe_b_exp': normal((DEPTH, N_EXPERTS), 0.01),
        'moe_w_gate': normal((DEPTH, N_EXPERTS, d, EXPERT_HIDDEN), d ** -0.5),
        'moe_w_up': normal((DEPTH, N_EXPERTS, d, EXPERT_HIDDEN), d ** -0.5),
        'moe_w_down': normal((DEPTH, N_EXPERTS, EXPERT_HIDDEN, d), EXPERT_HIDDEN ** -0.5),
    }


def reference(x, c, ctx, c_ctx, ada_w, ada_b, norm1_g, norm2_g, final_g,
              ev_w_in, ev_shift_mu, rwkv_w0, rwkv_w2, rwkv_a0, rwkv_a2, rwkv_g2,
              rwkv_k_k, rwkv_k_a, rwkv_r_k, rwkv_lnx_w, rwkv_lnx_b, ev_w_out,
              mla_w_in, mla_q_norm_g, mla_w_uq, mla_kv_norm_g, mla_w_ukv, mla_w_o,
              moe_w_grp, moe_b_grp, moe_w_exp, moe_b_exp, moe_w_gate, moe_w_up, moe_w_down):
    b, l, d = x.shape
    lc = ctx.shape[1]
    h, hc = x, ctx
    s_lat = jax.nn.silu(c)
    s_ctx = jax.nn.silu(c_ctx)[None]
    for layer in range(DEPTH):
        last = layer == DEPTH - 1
        i = layer // 2
        m = jnp.split(s_lat @ ada_w[layer] + ada_b[layer], N_MOD, axis=-1)
        mc = jnp.split(s_ctx @ ada_w[layer] + ada_b[layer], N_MOD, axis=-1)
        xl = modulate(rmsnorm(h, norm1_g[layer]), m[0], m[1])
        xc = modulate(rmsnorm(hc, norm1_g[layer]), mc[0], mc[1])
        if layer % 2 == 0:
            yc, yl = even_mixer(xc, xl, ev_w_in[i], ev_shift_mu[i], rwkv_w0[i], rwkv_w2[i],
                                rwkv_a0[i], rwkv_a2[i], rwkv_g2[i], rwkv_k_k[i], rwkv_k_a[i],
                                rwkv_r_k[i], rwkv_lnx_w[i], rwkv_lnx_b[i], ev_w_out[i])
        else:
            yc, yl = mla_mixer(xc, xl, mla_w_in[i], mla_q_norm_g[i], mla_w_uq[i],
                               mla_kv_norm_g[i], mla_w_ukv[i], mla_w_o[i], not last)
        h = h + m[2][:, None] * yl
        xl = modulate(rmsnorm(h, norm2_g[layer]), m[3], m[4])

        def channel_mix(t):
            return hier_moe(t, moe_w_grp[layer], moe_b_grp[layer], moe_w_exp[layer], moe_b_exp[layer],
                            moe_w_gate[layer], moe_w_up[layer], moe_w_down[layer])

        if last:
            h = h + m[5][:, None] * channel_mix(xl.reshape(b * l, d)).reshape(b, l, d)
        else:
            hc = hc + mc[2][:, None] * yc
            xc = modulate(rmsnorm(hc, norm2_g[layer]), mc[3], mc[4])
            y = channel_mix(jnp.concatenate([xc.reshape(b * lc, d), xl.reshape(b * l, d)], axis=0))
            hc = hc + mc[5][:, None] * y[: b * lc].reshape(b, lc, d)
            h = h + m[5][:, None] * y[b * lc:].reshape(b, l, d)
    return rmsnorm(h, final_g)
```

```python
import functools

import jax
import jax.numpy as jnp
import numpy as np
from jax import lax
from jax.experimental import pallas as pl
from jax.experimental.pallas import tpu as pltpu

F32 = jnp.float32
BF16 = jnp.bfloat16
HIGHEST = lax.Precision.HIGHEST

TM = 256
TMOE = 256
SCAN_TC = 32
EPS = 1e-6
GN_EPS = 64e-5
GRID_W = 64
ROPE_BASE = 10000.0

RET_HEADS, RET_DK, RET_DV = 4, 64, 128
RWKV_HEADS, RWKV_N = 8, 64
RWKV_W = RWKV_HEADS * RWKV_N
MLA_HEADS, Q_RANK, KV_RANK, QK_NOPE, QK_ROPE, V_HEAD = 8, 384, 256, 128, 64, 128
MLA_SCALE = (QK_NOPE + QK_ROPE) ** -0.5
N_GROUPS, EXPERTS_PER_GROUP = 4, 8
N_EXPERTS = N_GROUPS * EXPERTS_PER_GROUP
N_MOD = 6
MOD_ROWS = 16


def _cparams(sem, vmem_mb=None):
    kw = dict(dimension_semantics=sem)
    if vmem_mb is not None:
        kw["vmem_limit_bytes"] = vmem_mb * 1024 * 1024
    return pltpu.CompilerParams(**kw)


def _norm_mod(x, g, mod, shift_row, scale_row):
    var = jnp.mean(x * x, axis=-1, keepdims=True)
    y = x * lax.rsqrt(var + EPS) * g
    return y * (1.0 + mod[scale_row:scale_row + 1, :]) + mod[shift_row:shift_row + 1, :]


def _ada_kernel(c_ref, w_ref, b_ref, o_ref):
    s = c_ref[...]
    s = s * jax.nn.sigmoid(s)
    o_ref[...] = jnp.dot(s, w_ref[...], precision=HIGHEST, preferred_element_type=F32) + b_ref[...]


def _ada_table(c_all, ada_w, ada_b):
    depth, d, nd = ada_w.shape
    out = pl.pallas_call(
        _ada_kernel,
        grid=(depth, nd // d),
        in_specs=[pl.BlockSpec((MOD_ROWS, d), lambda l, j: (0, 0)),
                  pl.BlockSpec((None, d, d), lambda l, j: (l, 0, j)),
                  pl.BlockSpec((None, 1, d), lambda l, j: (l, 0, j))],
        out_specs=pl.BlockSpec((None, MOD_ROWS, d), lambda l, j: (l, 0, j)),
        out_shape=jax.ShapeDtypeStruct((depth, MOD_ROWS, nd), F32),
        compiler_params=_cparams(("arbitrary", "arbitrary")),
        name="ada_table",
    )(c_all, ada_w, ada_b.reshape(depth, 1, nd))
    return out.reshape(depth, MOD_ROWS, N_MOD, d)


def _proj_kernel(x_ref, g_ref, mod_ref, w_ref, *o_refs, splits):
    xm = _norm_mod(x_ref[...], g_ref[...], mod_ref[...], 0, 1).astype(BF16)
    off = 0
    for o_ref, n in zip(o_refs, splits):
        for j in range(0, n, 512):
            c = min(512, n - j)
            o_ref[:, j:j + c] = jnp.dot(xm, w_ref[:, off + j:off + j + c],
                                        preferred_element_type=F32).astype(o_ref.dtype)
        off += n


def _norm_proj(h, g, mod, modrow, w_bf16, splits, dtypes):
    n, d = h.shape
    nout = w_bf16.shape[1]
    return pl.pallas_call(
        functools.partial(_proj_kernel, splits=splits),
        grid=(n // TM,),
        in_specs=[pl.BlockSpec((TM, d), lambda t: (t, 0)),
                  pl.BlockSpec((1, d), lambda t: (0, 0)),
                  pl.BlockSpec((None, N_MOD, d), lambda t: (modrow(t), 0, 0)),
                  pl.BlockSpec((d, nout), lambda t: (0, 0))],
        out_specs=[pl.BlockSpec((TM, s), lambda t: (t, 0)) for s in splits],
        out_shape=[jax.ShapeDtypeStruct((n, s), dt) for s, dt in zip(splits, dtypes)],
        compiler_params=_cparams(("arbitrary",)),
        name="norm_proj",
    )(h, g.reshape(1, d), mod, w_bf16)


def _ret_kernel(lg_ref, q_ref, k_ref, v_ref, o_ref, *, lc, l, tq):
    qi = pl.program_id(2)
    nct = lc // tq
    nk = (lc + l) // tq
    lg = lg_ref[0:1, 0:1]
    q = q_ref[...]
    rel0 = (lax.broadcasted_iota(jnp.int32, (tq, tq), 0)
            - lax.broadcasted_iota(jnp.int32, (tq, tq), 1))
    q_lat = qi >= nct

    def body(kj, acc):
        start = pl.multiple_of(kj * tq, tq)
        k = k_ref[pl.ds(start, tq), :]
        v = v_ref[pl.ds(start, tq), :]
        s = lax.dot_general(q, k, (((1,), (1,)), ((), ())), preferred_element_type=F32)
        dist = (rel0 + (qi - kj) * tq).astype(F32)
        mask = jnp.exp(lg * jnp.abs(dist))
        k_ctx = kj < nct
        extra = jnp.exp(lg * (float(l + lc) - dist))
        mask = mask + jnp.where(jnp.logical_and(q_lat, k_ctx), extra, 0.0)
        mask = jnp.where(jnp.logical_and(jnp.logical_not(q_lat), jnp.logical_not(k_ctx)), 0.0, mask)
        p = (s * mask * (RET_DK ** -0.5)).astype(BF16)
        return acc + jnp.dot(p, v, preferred_element_type=F32)

    o_ref[...] = lax.fori_loop(0, nk, body, jnp.zeros((tq, RET_DV), F32))


def _retention(q, k, v, lg, lc, l):
    b, hh, s, dk = q.shape
    dv = v.shape[-1]
    tq = TM
    return pl.pallas_call(
        functools.partial(_ret_kernel, lc=lc, l=l, tq=tq),
        grid=(b, hh, s // tq),
        in_specs=[pl.BlockSpec((None, 1, 128), lambda bi, hi, qi: (hi, 0, 0)),
                  pl.BlockSpec((None, None, tq, dk), lambda bi, hi, qi: (bi, hi, qi, 0)),
                  pl.BlockSpec((None, None, s, dk), lambda bi, hi, qi: (bi, hi, 0, 0)),
                  pl.BlockSpec((None, None, s, dv), lambda bi, hi, qi: (bi, hi, 0, 0))],
        out_specs=pl.BlockSpec((None, None, tq, dv), lambda bi, hi, qi: (bi, hi, qi, 0)),
        out_shape=jax.ShapeDtypeStruct((b, hh, s, dv), F32),
        compiler_params=_cparams(("arbitrary", "arbitrary", "arbitrary")),
        name="retention",
    )(lg, q, k, v)


def _feat_kernel(rw_ref, prev_ref, next_ref, mu_ref, w0_ref, w2_ref, a0_ref, a2_ref, g2_ref,
                 kkw_ref, ka_ref, rk_ref, ones_ref,
                 r_o, kk_o, v_o, w_o, kka_o, kt_o, gate_o, bonus_o, *, tpb, nct):
    wt = pl.program_id(0) % tpb
    first = jnp.logical_or(wt == 0, wt == nct)
    last = jnp.logical_or(wt == nct - 1, wt == tpb - 1)
    y = rw_ref[...]
    tm = y.shape[0]
    prow = jnp.where(first, 0.0, prev_ref[7:8, :])
    nrow = jnp.where(last, 0.0, next_ref[0:1, :])
    rid = lax.broadcasted_iota(jnp.int32, (tm, 1), 0)
    prev = jnp.where(rid == 0, prow, pltpu.roll(y, 1, 0))
    nxt = jnp.where(rid == tm - 1, nrow, pltpu.roll(y, tm - 1, 0))
    ys = y + mu_ref[0:1, :] * (prev - y) + mu_ref[1:2, :] * (nxt - y)

    w_ = RWKV_W
    r = ys[:, 0:w_]
    kr = ys[:, w_:2 * w_]
    vr = ys[:, 2 * w_:3 * w_]
    wd = ys[:, 3 * w_:3 * w_ + 128]
    ad = ys[:, 3 * w_ + 128:3 * w_ + 256]
    gd = ys[:, 3 * w_ + 256:3 * w_ + 512]
    ones = ones_ref[...]

    kk = kr * kkw_ref[...]
    ss = jnp.dot(kk * kk, ones, precision=HIGHEST, preferred_element_type=F32)
    kk = kk / jnp.maximum(jnp.sqrt(ss), 1e-12)
    zw = w0_ref[...] + jnp.dot(jnp.tanh(wd), w2_ref[...], precision=HIGHEST, preferred_element_type=F32)
    wlog = -(jnp.maximum(-zw, 0.0) + jnp.log1p(jnp.exp(-jnp.abs(zw)))) - 0.5
    decay = jnp.exp(-jnp.exp(wlog))
    a = jax.nn.sigmoid(a0_ref[...] + jnp.dot(ad, a2_ref[...], precision=HIGHEST, preferred_element_type=F32))
    gate_o[...] = jnp.dot(jax.nn.sigmoid(gd), g2_ref[...], precision=HIGHEST, preferred_element_type=F32)
    ka = ka_ref[...]
    r_o[...] = r
    kk_o[...] = kk
    v_o[...] = vr
    ktsum = None
    for d in range(2):
        a_d = a[:, d * w_:(d + 1) * w_]
        kt = kr * (1.0 + (a_d - 1.0) * ka)
        w_o[d] = decay[:, d * w_:(d + 1) * w_]
        kka_o[d] = kk * a_d
        kt_o[d] = kt
        ktsum = kt if ktsum is None else ktsum + kt
    bonus_o[...] = jnp.dot(r * ktsum * rk_ref[...], ones, precision=HIGHEST, preferred_element_type=F32) * vr


def _rwkv_features(rw, params, tpb, nct):
    n, wid = rw.shape
    ntile = n // TM
    rb = TM // 8
    nrb = n // 8
    w_ = RWKV_W
    tok = pl.BlockSpec((TM, w_), lambda t: (t, 0))
    tok2 = pl.BlockSpec((2, TM, w_), lambda t: (0, t, 0))

    def full(a):
        return pl.BlockSpec(a.shape, lambda t: (0,) * a.ndim)

    one = jax.ShapeDtypeStruct((n, w_), F32)
    two = jax.ShapeDtypeStruct((2, n, w_), F32)
    return pl.pallas_call(
        functools.partial(_feat_kernel, tpb=tpb, nct=nct),
        grid=(ntile,),
        in_specs=[pl.BlockSpec((TM, wid), lambda t: (t, 0)),
                  pl.BlockSpec((8, wid), lambda t: (jnp.maximum(t * rb - 1, 0), 0)),
                  pl.BlockSpec((8, wid), lambda t: (jnp.minimum((t + 1) * rb, nrb - 1), 0))]
                 + [full(a) for a in params],
        out_specs=[tok, tok, tok, tok2, tok2, tok2, tok, tok],
        out_shape=[one, one, one, two, two, two, one, one],
        compiler_params=_cparams(("arbitrary",)),
        name="rwkv_features",
    )(rw, rw, rw, *params)


def _scan_kernel(kk_ref, w_ref, kka_ref, kt_ref, r_ref, v_ref, y_ref, s_ref):
    @pl.when(pl.program_id(0) == 0)
    def _():
        s_ref[...] = jnp.zeros_like(s_ref)

    nkey = s_ref.shape[0]
    tc = kk_ref.shape[0]

    def row(ref, i, k):
        return ref[i, pl.ds(k, 1), :][None]

    def step(i, carry):
        v = v_ref[i]
        sa0 = s_ref[0] * row(kk_ref, i, 0)
        sa1 = s_ref[1] * row(kk_ref, i, 1)
        for k in range(2, nkey, 2):
            sa0 = sa0 + s_ref[k] * row(kk_ref, i, k)
            sa1 = sa1 + s_ref[k + 1] * row(kk_ref, i, k + 1)
        sa = sa0 + sa1
        ys = [None, None]
        for k in range(nkey):
            s_new = s_ref[k] * row(w_ref, i, k) + (v * row(kt_ref, i, k) - sa * row(kka_ref, i, k))
            s_ref[k] = s_new
            t = s_new * row(r_ref, i, k)
            ys[k % 2] = t if ys[k % 2] is None else ys[k % 2] + t
        y_ref[i] = ys[0] + ys[1]
        return carry

    lax.fori_loop(0, tc, step, 0)


def _rwkv_scan(kk, w, kka, kt, r, v):
    s, nkey, lanes = kk.shape
    tc = SCAN_TC
    kspec = pl.BlockSpec((tc, nkey, lanes), lambda t: (t, 0, 0))
    vspec = pl.BlockSpec((tc, nkey // 8, 8, lanes), lambda t: (t, 0, 0, 0))
    y = pl.pallas_call(
        _scan_kernel,
        grid=(s // tc,),
        in_specs=[kspec, kspec, kspec, kspec, kspec, vspec],
        out_specs=vspec,
        out_shape=jax.ShapeDtypeStruct((s, nkey // 8, 8, lanes), F32),
        scratch_shapes=[pltpu.VMEM((nkey, nkey // 8, 8, lanes), F32)],
        compiler_params=_cparams(("arbitrary",)),
        name="rwkv_scan",
    )(kk, w, kka, kt, r, v.reshape(s, nkey // 8, 8, lanes))
    return y.reshape(s, nkey, lanes)


def _even_out_kernel(o_ref, g_ref, y_ref, bonus_ref, gate_ref, lnw_ref, lnb_ref, ones_ref,
                     w_ref, h_ref, mod_ref, out_ref):
    parts = []
    for hh in range(RET_HEADS):
        o = o_ref[hh]
        o = o * lax.rsqrt(jnp.mean(o * o, axis=-1, keepdims=True) + EPS)
        gg = g_ref[:, hh * RET_DV:(hh + 1) * RET_DV]
        parts.append(o * (gg * jax.nn.sigmoid(gg)))
    ones = ones_ref[...]
    y = y_ref[0] + y_ref[1]
    mu = jnp.dot(y, ones, precision=HIGHEST, preferred_element_type=F32) * (1.0 / RWKV_N)
    yc = y - mu
    var = jnp.dot(yc * yc, ones, precision=HIGHEST, preferred_element_type=F32) * (1.0 / RWKV_N)
    yn = yc * lax.rsqrt(var + GN_EPS) * lnw_ref[...] + lnb_ref[...]
    parts.append((yn + bonus_ref[...]) * gate_ref[...])
    cat = jnp.concatenate(parts, axis=-1).astype(BF16)
    mix = jnp.dot(cat, w_ref[...], preferred_element_type=F32)
    out_ref[...] = h_ref[...] + mod_ref[2:3, :] * mix


def _even_out(o_ret, g, y2, bonus, gate, lnw, lnb, ones, w_out, h, mod, modrow, tpb):
    n, d = h.shape
    w_ = RWKV_W
    tok = pl.BlockSpec((TM, w_), lambda t: (t, 0))

    def full(a):
        return pl.BlockSpec(a.shape, lambda t: (0,) * a.ndim)

    return pl.pallas_call(
        _even_out_kernel,
        grid=(n // TM,),
        in_specs=[pl.BlockSpec((None, RET_HEADS, TM, RET_DV), lambda t: (t // tpb, 0, t % tpb, 0)),
                  tok,
                  pl.BlockSpec((2, TM, w_), lambda t: (0, t, 0)),
                  tok, tok, full(lnw), full(lnb), full(ones), full(w_out),
                  pl.BlockSpec((TM, d), lambda t: (t, 0)),
                  pl.BlockSpec((None, N_MOD, d), lambda t: (modrow(t), 0, 0))],
        out_specs=pl.BlockSpec((TM, d), lambda t: (t, 0)),
        out_shape=jax.ShapeDtypeStruct((n, d), F32),
        compiler_params=_cparams(("arbitrary",)),
        name="even_out",
    )(o_ret, g, y2, bonus, gate, lnw, lnb, ones, w_out, h, mod)


def _mla_proj_kernel(a_ref, qg_ref, kvg_ref, wuq_ref, wukv_ref, cos_ref, sin_ref,
                     qn_o, qp_o, kn_o, v_o, kp_o):
    a = a_ref[...]
    cos = cos_ref[...]
    sin = sin_ref[...]

    def rms(t, g):
        return (t * lax.rsqrt(jnp.mean(t * t, axis=-1, keepdims=True) + EPS) * g).astype(BF16)

    cq = rms(a[:, :Q_RANK], qg_ref[...])
    ckv = rms(a[:, Q_RANK:Q_RANK + KV_RANK], kvg_ref[...])
    pe0 = Q_RANK + KV_RANK
    kp_o[...] = (a[:, pe0:pe0 + QK_ROPE] * cos + a[:, pe0 + QK_ROPE:pe0 + 2 * QK_ROPE] * sin).astype(BF16)
    hw = QK_NOPE + 2 * QK_ROPE
    for hh in range(MLA_HEADS):
        qh = jnp.dot(cq, wuq_ref[:, hh * hw:(hh + 1) * hw], preferred_element_type=F32)
        qn_o[hh] = qh[:, :QK_NOPE].astype(BF16)
        qp_o[hh] = (qh[:, QK_NOPE:QK_NOPE + QK_ROPE] * cos + qh[:, QK_NOPE + QK_ROPE:] * sin).astype(BF16)
        kvh = jnp.dot(ckv, wukv_ref[:, hh * hw:(hh + 1) * hw], preferred_element_type=F32)
        kn_o[hh] = kvh[:, :QK_NOPE].astype(BF16)
        v_o[hh] = kvh[:, QK_NOPE:].astype(BF16)


def _mla_proj(a, qg, kvg, wuq, wukv, cos, sin, tpb):
    n, wid = a.shape
    hd = MLA_HEADS

    def full(x):
        return pl.BlockSpec(x.shape, lambda t: (0,) * x.ndim)

    h128 = pl.BlockSpec((hd, TM, 128), lambda t: (0, t, 0))
    rope = pl.BlockSpec((TM, QK_ROPE), lambda t: (t % tpb, 0))
    return pl.pallas_call(
        _mla_proj_kernel,
        grid=(n // TM,),
        in_specs=[pl.BlockSpec((TM, wid), lambda t: (t, 0)), full(qg), full(kvg), full(wuq), full(wukv),
                  rope, rope],
        out_specs=[h128, pl.BlockSpec((hd, TM, QK_ROPE), lambda t: (0, t, 0)), h128, h128,
                   pl.BlockSpec((TM, QK_ROPE), lambda t: (t, 0))],
        out_shape=[jax.ShapeDtypeStruct((hd, n, 128), BF16),
                   jax.ShapeDtypeStruct((hd, n, QK_ROPE), BF16),
                   jax.ShapeDtypeStruct((hd, n, 128), BF16),
                   jax.ShapeDtypeStruct((hd, n, 128), BF16),
                   jax.ShapeDtypeStruct((n, QK_ROPE), BF16)],
        compiler_params=_cparams(("arbitrary",)),
        name="mla_proj",
    )(a, qg, kvg, wuq, wukv, cos, sin)


def _attn_kernel(qn_ref, qp_ref, kn_ref, kp_ref, v_ref, o_ref):
    dn = (((1,), (1,)), ((), ()))
    s = (lax.dot_general(qn_ref[...], kn_ref[...], dn, preferred_element_type=F32)
         + lax.dot_general(qp_ref[...], kp_ref[...], dn, preferred_element_type=F32)) * MLA_SCALE
    m = jnp.max(s, axis=-1, keepdims=True)
    p = jnp.exp(s - m)
    l = jnp.sum(p, axis=-1, keepdims=True)
    o = jnp.dot(p.astype(BF16), v_ref[...], preferred_element_type=F32)
    o_ref[...] = (o / l).astype(o_ref.dtype)


def _attention(qn, qp, kn, kp, v, b, lc, l):
    hd = MLA_HEADS
    s = lc + l
    tpb = s // TM
    nct = lc // TM
    nq = l // TM
    return pl.pallas_call(
        _attn_kernel,
        grid=(b, hd, nq),
        in_specs=[pl.BlockSpec((None, TM, 128), lambda bi, hi, qi: (hi, bi * tpb + nct + qi, 0)),
                  pl.BlockSpec((None, TM, QK_ROPE), lambda bi, hi, qi: (hi, bi * tpb + nct + qi, 0)),
                  pl.BlockSpec((None, s, 128), lambda bi, hi, qi: (hi, bi, 0)),
                  pl.BlockSpec((s, QK_ROPE), lambda bi, hi, qi: (bi, 0)),
                  pl.BlockSpec((None, s, 128), lambda bi, hi, qi: (hi, bi, 0))],
        out_specs=pl.BlockSpec((TM, V_HEAD), lambda bi, hi, qi: (bi * nq + qi, hi)),
        out_shape=jax.ShapeDtypeStruct((b * l, hd * V_HEAD), BF16),
        compiler_params=_cparams(("arbitrary", "arbitrary", "arbitrary")),
        name="mla_attention",
    )(qn, qp, kn, kp, v)


def _oproj_kernel(o_ref, w_ref, h_ref, mod_ref, out_ref):
    mix = jnp.dot(o_ref[...], w_ref[...], preferred_element_type=F32)
    out_ref[...] = h_ref[...] + mod_ref[2:3, :] * mix


def _oproj(o, w_o, h, mod, hrow, modrow):
    n, d = o.shape[0], h.shape[1]
    return pl.pallas_call(
        _oproj_kernel,
        grid=(n // TM,),
        in_specs=[pl.BlockSpec((TM, o.shape[1]), lambda t: (t, 0)),
                  pl.BlockSpec(w_o.shape, lambda t: (0, 0)),
                  pl.BlockSpec((TM, d), lambda t: (hrow(t), 0)),
                  pl.BlockSpec((None, N_MOD, d), lambda t: (modrow(t), 0, 0))],
        out_specs=pl.BlockSpec((TM, d), lambda t: (t, 0)),
        out_shape=jax.ShapeDtypeStruct((n, d), F32),
        compiler_params=_cparams(("arbitrary",)),
        name="mla_oproj",
    )(o, w_o, h, mod)


def _route_kernel(h_ref, g_ref, mod_ref, wr_ref, br_ref, xl_ref, route_ref):
    xl = _norm_mod(h_ref[...], g_ref[...], mod_ref[...], 3, 4)
    xl_ref[...] = xl.astype(BF16)
    logits = jnp.dot(xl, wr_ref[...], precision=HIGHEST, preferred_element_type=F32) + br_ref[...]
    lane_i = lax.broadcasted_iota(jnp.int32, logits.shape, 1)
    lane = lane_i.astype(F32)
    lane_grp = (lane_i >> 3).astype(F32)
    neg = -jnp.inf
    big = 1e6
    gl = jnp.where(jnp.logical_and(lane_i >= N_EXPERTS, lane_i < N_EXPERTS + N_GROUPS), logits, neg)
    gmax = jnp.max(gl, axis=-1, keepdims=True)
    gsum = jnp.sum(jnp.exp(gl - gmax), axis=-1, keepdims=True)
    pg = 1.0 / gsum
    gidx = jnp.min(jnp.where(gl == gmax, lane - N_EXPERTS, big), axis=-1, keepdims=True)
    in_grp = jnp.logical_and(lane_i < N_EXPERTS, lane_grp == gidx)
    el = jnp.where(in_grp, logits, neg)
    emax = jnp.max(el, axis=-1, keepdims=True)
    esum = jnp.sum(jnp.exp(el - emax), axis=-1, keepdims=True)
    i1 = jnp.min(jnp.where(el == emax, lane, big), axis=-1, keepdims=True)
    el2 = jnp.where(lane == i1, neg, el)
    emax2 = jnp.max(el2, axis=-1, keepdims=True)
    i2 = jnp.min(jnp.where(el2 == emax2, lane, big), axis=-1, keepdims=True)
    pe1 = 1.0 / esum
    pe2 = jnp.exp(emax2 - emax) / esum
    den = pe1 + pe2
    w1 = pg * pe1 / den
    w2 = pg * pe2 / den
    route_ref[...] = jnp.where(lane_i == 0, i1,
                               jnp.where(lane_i == 1, i2,
                                         jnp.where(lane_i == 2, w1, jnp.where(lane_i == 3, w2, 0.0))))


def _route(h, g, mod, modrow, wr, br):
    n, d = h.shape
    return pl.pallas_call(
        _route_kernel,
        grid=(n // TM,),
        in_specs=[pl.BlockSpec((TM, d), lambda t: (t, 0)),
                  pl.BlockSpec((1, d), lambda t: (0, 0)),
                  pl.BlockSpec((None, N_MOD, d), lambda t: (modrow(t), 0, 0)),
                  pl.BlockSpec(wr.shape, lambda t: (0, 0)),
                  pl.BlockSpec(br.shape, lambda t: (0, 0))],
        out_specs=[pl.BlockSpec((TM, d), lambda t: (t, 0)),
                   pl.BlockSpec((TM, 128), lambda t: (t, 0))],
        out_shape=[jax.ShapeDtypeStruct((n, d), BF16), jax.ShapeDtypeStruct((n, 128), F32)],
        compiler_params=_cparams(("arbitrary",)),
        name="moe_route",
    )(h, g.reshape(1, d), mod, wr, br)


def _expert_kernel(te_ref, nt_ref, x_ref, wg_ref, wu_ref, wd_ref, o_ref, wg_s, wu_s, wd_s):
    t = pl.program_id(0)

    @pl.when(t < nt_ref[0])
    def _():
        changed = jnp.logical_or(t == 0, te_ref[t] != te_ref[jnp.maximum(t - 1, 0)])

        @pl.when(changed)
        def _():
            wg_s[...] = wg_ref[...].astype(BF16)
            wu_s[...] = wu_ref[...].astype(BF16)
            wd_s[...] = wd_ref[...].astype(BF16)

        x = x_ref[...]
        h1 = jnp.dot(x, wg_s[...], preferred_element_type=F32)
        h2 = jnp.dot(x, wu_s[...], preferred_element_type=F32)
        hid = ((h1 * jax.nn.sigmoid(h1)) * h2).astype(BF16)
        o_ref[...] = jnp.dot(hid, wd_s[...], preferred_element_type=F32)

    @pl.when(t >= nt_ref[0])
    def _():
        o_ref[...] = jnp.zeros_like(o_ref)


def _experts(tile_e, ntiles, xs, w_gate, w_up, w_down, layer):
    rows, d = xs.shape
    hid = w_gate.shape[-1]
    grid_spec = pltpu.PrefetchScalarGridSpec(
        num_scalar_prefetch=2,
        grid=(rows // TMOE,),
        in_specs=[pl.BlockSpec((TMOE, d), lambda t, te, nt: (t, 0)),
                  pl.BlockSpec((None, None, d, hid), lambda t, te, nt: (layer, te[t], 0, 0)),
                  pl.BlockSpec((None, None, d, hid), lambda t, te, nt: (layer, te[t], 0, 0)),
                  pl.BlockSpec((None, None, hid, d), lambda t, te, nt: (layer, te[t], 0, 0))],
        out_specs=pl.BlockSpec((TMOE, d), lambda t, te, nt: (t, 0)),
        scratch_shapes=[pltpu.VMEM((d, hid), BF16), pltpu.VMEM((d, hid), BF16), pltpu.VMEM((hid, d), BF16)],
    )
    return pl.pallas_call(
        _expert_kernel,
        grid_spec=grid_spec,
        out_shape=jax.ShapeDtypeStruct((rows, d), F32),
        compiler_params=_cparams(("arbitrary",)),
        name="moe_experts",
    )(tile_e, ntiles, xs, w_gate, w_up, w_down)


def _combine_kernel(y1_ref, y2_ref, route_ref, h_ref, mod_ref, fg_ref, out_ref, *, final):
    route = route_ref[...]
    y = route[:, 2:3] * y1_ref[...] + route[:, 3:4] * y2_ref[...]
    hn = h_ref[...] + mod_ref[5:6, :] * y
    if final:
        hn = hn * lax.rsqrt(jnp.mean(hn * hn, axis=-1, keepdims=True) + EPS) * fg_ref[...]
    out_ref[...] = hn


def _combine(y1, y2, route, h, mod, modrow, fg, final):
    n, d = h.shape
    tok = pl.BlockSpec((TM, d), lambda t: (t, 0))
    return pl.pallas_call(
        functools.partial(_combine_kernel, final=final),
        grid=(n // TM,),
        in_specs=[tok, tok, pl.BlockSpec((TM, 128), lambda t: (t, 0)), tok,
                  pl.BlockSpec((None, N_MOD, d), lambda t: (modrow(t), 0, 0)),
                  pl.BlockSpec((1, d), lambda t: (0, 0))],
        out_specs=tok,
        out_shape=jax.ShapeDtypeStruct((n, d), F32),
        compiler_params=_cparams(("arbitrary",)),
        name="moe_combine",
    )(y1, y2, route, h, mod, fg.reshape(1, d))


def _moe(h, g, mod, modrow, wr, br, w_gate, w_up, w_down, layer, fg, final):
    n, d = h.shape
    xl, route = _route(h, g, mod, modrow, wr, br)
    e = route[:, :2].astype(jnp.int32).reshape(-1)
    onehot = (e[:, None] == jnp.arange(N_EXPERTS, dtype=jnp.int32)[None, :]).astype(jnp.int32)
    csum = jnp.cumsum(onehot, axis=0)
    counts = csum[-1]
    rank = jnp.sum(csum * onehot, axis=1) - 1
    padded = ((counts + TMOE - 1) // TMOE) * TMOE
    pend = jnp.cumsum(padded)
    pos = (pend - padded)[e] + rank
    rows = 2 * n + N_EXPERTS * TMOE
    src = jnp.zeros((rows,), jnp.int32).at[pos].set(jnp.arange(2 * n, dtype=jnp.int32) // 2)
    ntile = rows // TMOE
    nvalid = (pend[-1] // TMOE).astype(jnp.int32)
    tstart = jnp.arange(ntile, dtype=jnp.int32) * TMOE
    tile_e = jnp.sum((tstart[:, None] >= pend[None, :]).astype(jnp.int32), axis=1)
    last_e = jnp.sum((((nvalid - 1) * TMOE) >= pend).astype(jnp.int32))
    tile_e = jnp.where(tstart < pend[-1], tile_e, last_e).astype(jnp.int32)
    xs = jnp.take(xl, src, axis=0)
    ys = _experts(tile_e, nvalid.reshape(1), xs, w_gate, w_up, w_down, layer)
    pos2 = pos.reshape(n, 2)
    y1 = jnp.take(ys, pos2[:, 0], axis=0)
    y2 = jnp.take(ys, pos2[:, 1], axis=0)
    return _combine(y1, y2, route, h, mod, modrow, fg, final)


def _block_ones(width, seg):
    idx = np.arange(width) // seg
    return jnp.asarray((idx[:, None] == idx[None, :]).astype(np.float32))


def _rope_tables(lc, l):
    rows = l // GRID_W
    row = np.repeat(np.arange(rows, dtype=np.float32), GRID_W)
    col = np.tile(np.arange(GRID_W, dtype=np.float32), rows)
    n_freq = QK_ROPE // 4
    inv_freq = jnp.asarray(ROPE_BASE, F32) ** (-jnp.arange(n_freq, dtype=F32) / n_freq)
    ang_r = jnp.asarray(row)[:, None] * inv_freq
    ang_c = jnp.asarray(col)[:, None] * inv_freq
    cos = jnp.concatenate([jnp.cos(ang_r), jnp.cos(ang_r), jnp.cos(ang_c), jnp.cos(ang_c)], axis=-1)
    sin = jnp.concatenate([jnp.sin(ang_r), jnp.sin(ang_r), jnp.sin(ang_c), jnp.sin(ang_c)], axis=-1)
    cos = jnp.concatenate([jnp.ones((lc, QK_ROPE), F32), cos], axis=0)
    sin = jnp.concatenate([jnp.zeros((lc, QK_ROPE), F32), sin], axis=0)
    return cos, sin


def _rot_cols(pe):
    q = QK_ROPE // 4
    return jnp.concatenate([-pe[..., q:2 * q], pe[..., 0:q], -pe[..., 3 * q:4 * q], pe[..., 2 * q:3 * q]], axis=-1)


def _pad_cols(w, n):
    return jnp.pad(w, ((0, 0), (0, n - w.shape[1])))


def kernel(x, c, ctx, c_ctx, ada_w, ada_b, norm1_g, norm2_g, final_g, ev_w_in, ev_shift_mu, rwkv_w0, rwkv_w2, rwkv_a0, rwkv_a2, rwkv_g2, rwkv_k_k, rwkv_k_a, rwkv_r_k, rwkv_lnx_w, rwkv_lnx_b, ev_w_out, mla_w_in, mla_q_norm_g, mla_w_uq, mla_kv_norm_g, mla_w_ukv, mla_w_o, moe_w_grp, moe_b_grp, moe_w_exp, moe_b_exp, moe_w_gate, moe_w_up, moe_w_down):
    b, l, d = x.shape
    lc = ctx.shape[1]
    s = lc + l
    n = b * s
    tpb = s // TM
    nct = lc // TM
    nlt = l // TM
    assert lc % TM == 0 and l % TM == 0 and b + 1 <= MOD_ROWS and ada_w.shape[0] == 2

    def modrow_all(t):
        return jnp.where(t % tpb < nct, b, t // tpb)

    def modrow_lat(t):
        return t // nlt

    def lat_tile(t):
        return (t // nlt) * tpb + nct + t % nlt

    c_all = jnp.concatenate([c, c_ctx[None], jnp.zeros((MOD_ROWS - b - 1, d), F32)], axis=0)
    mod = _ada_table(c_all, ada_w, ada_b)
    h = jnp.concatenate([ctx, x], axis=1).reshape(n, d)

    def router_weights(layer):
        wr = jnp.concatenate([moe_w_exp[layer], moe_w_grp[layer]], axis=1)
        br = jnp.concatenate([moe_b_exp[layer], moe_b_grp[layer]])[None]
        return _pad_cols(wr, 128), _pad_cols(br, 128)

    w_ = RWKV_W
    ret_w = 2 * RET_HEADS * RET_DK + RET_HEADS * RET_DV
    w_in = _pad_cols(ev_w_in[0], 3584).astype(BF16)
    qkv, gret, rw = _norm_proj(h, norm1_g[0], mod[0], modrow_all, w_in,
                               (ret_w, RET_HEADS * RET_DV, 2048), (BF16, F32, F32))

    def heads(t, dh):
        return t.reshape(b, s, RET_HEADS, dh).transpose(0, 2, 1, 3)

    nqk = RET_HEADS * RET_DK
    lg = jnp.log1p(-jnp.exp2(-5.0 - jnp.arange(RET_HEADS, dtype=F32)))
    o_ret = _retention(heads(qkv[:, :nqk], RET_DK), heads(qkv[:, nqk:2 * nqk], RET_DK),
                       heads(qkv[:, 2 * nqk:], RET_DV),
                       jnp.broadcast_to(lg[:, None, None], (RET_HEADS, 1, 128)), lc, l)

    ones8 = _block_ones(w_, RWKV_N)
    zero = jnp.zeros((64, w_), F32)
    w2bd = jnp.concatenate([jnp.concatenate([rwkv_w2[0, 0], zero], axis=1),
                            jnp.concatenate([zero, rwkv_w2[0, 1]], axis=1)], axis=0)
    a2bd = jnp.concatenate([jnp.concatenate([rwkv_a2[0, 0], zero], axis=1),
                            jnp.concatenate([zero, rwkv_a2[0, 1]], axis=1)], axis=0)
    g2p = jnp.pad(rwkv_g2[0], ((0, 256 - rwkv_g2.shape[1]), (0, 0)))
    mu = _pad_cols(ev_shift_mu[0], 2048)
    feat_params = (mu, rwkv_w0[0].reshape(1, 2 * w_), w2bd, rwkv_a0[0].reshape(1, 2 * w_), a2bd, g2p,
                   rwkv_k_k[0][None], rwkv_k_a[0][None], rwkv_r_k[0].reshape(1, w_), ones8)
    r_t, kk_t, v_t, w_t, kka_t, kt_t, gate, bonus = _rwkv_features(rw, feat_params, tpb, nct)

    def seq_order(t, rev):
        t = t.reshape(b, s, RWKV_HEADS, RWKV_N)
        if rev:
            t = jnp.concatenate([jnp.flip(t[:, :lc], axis=1), jnp.flip(t[:, lc:], axis=1)], axis=1)
        return t

    def to_scan(fwd, bwd):
        t = jnp.stack([seq_order(fwd, False), seq_order(bwd, True)], axis=0)
        return t.transpose(2, 4, 0, 1, 3).reshape(s, RWKV_N, 2 * b * RWKV_HEADS)

    y_scan = _rwkv_scan(to_scan(kk_t, kk_t), to_scan(w_t[0], w_t[1]), to_scan(kka_t[0], kka_t[1]),
                        to_scan(kt_t[0], kt_t[1]), to_scan(r_t, r_t), to_scan(v_t, v_t))
    y_scan = y_scan.reshape(s, RWKV_N, 2, b, RWKV_HEADS).transpose(2, 3, 0, 4, 1)
    y_f = y_scan[0]
    y_b = jnp.concatenate([jnp.flip(y_scan[1][:, :lc], axis=1), jnp.flip(y_scan[1][:, lc:], axis=1)], axis=1)
    y2 = jnp.stack([y_f, y_b], axis=0).reshape(2, n, w_)

    h = _even_out(o_ret, gret, y2, bonus, gate, rwkv_lnx_w[0][None], rwkv_lnx_b[0][None], ones8,
                  ev_w_out[0].astype(BF16), h, mod[0], modrow_all, tpb)
    wr, br = router_weights(0)
    h = _moe(h, norm2_g[0], mod[0], modrow_all, wr, br, moe_w_gate, moe_w_up, moe_w_down, 0, final_g, False)

    w_in1 = jnp.concatenate([mla_w_in[0], _rot_cols(mla_w_in[0][:, Q_RANK + KV_RANK:])], axis=1).astype(BF16)
    (a1,) = _norm_proj(h, norm1_g[1], mod[1], modrow_all, w_in1, (w_in1.shape[1],), (F32,))
    wq = mla_w_uq[0].reshape(Q_RANK, MLA_HEADS, QK_NOPE + QK_ROPE)
    wq = jnp.concatenate([wq, _rot_cols(wq[..., QK_NOPE:])], axis=-1).reshape(Q_RANK, -1).astype(BF16)
    cos, sin = _rope_tables(lc, l)
    qn, qp, kn, vv, kp = _mla_proj(a1, mla_q_norm_g[0][None], mla_kv_norm_g[0][None], wq,
                                   mla_w_ukv[0].astype(BF16), cos, sin, tpb)
    o = _attention(qn, qp, kn, kp, vv, b, lc, l)
    h = _oproj(o, mla_w_o[0].astype(BF16), h, mod[1], lat_tile, modrow_lat)
    wr, br = router_weights(1)
    h = _moe(h, norm2_g[1], mod[1], modrow_lat, wr, br, moe_w_gate, moe_w_up, moe_w_down, 1, final_g, True)
    return h.reshape(b, l, d)
```

```python
import functools

import jax
import jax.numpy as jnp
import numpy as np
from jax import lax
from jax.experimental import pallas as pl
from jax.experimental.pallas import tpu as pltpu

F32 = jnp.float32
BF16 = jnp.bfloat16
HIGHEST = lax.Precision.HIGHEST

TM = 256
TMOE = 512
SCAN_TC = 32
EPS = 1e-6
GN_EPS = 64e-5
GRID_W = 64
ROPE_BASE = 10000.0

RET_HEADS, RET_DK, RET_DV = 4, 64, 128
RWKV_HEADS, RWKV_N = 8, 64
RWKV_W = RWKV_HEADS * RWKV_N
MLA_HEADS, Q_RANK, KV_RANK, QK_NOPE, QK_ROPE, V_HEAD = 8, 384, 256, 128, 64, 128
MLA_SCALE = (QK_NOPE + QK_ROPE) ** -0.5
N_GROUPS, EXPERTS_PER_GROUP = 4, 8
N_EXPERTS = N_GROUPS * EXPERTS_PER_GROUP
N_MOD = 6
MOD_ROWS = 16


def _cparams(sem, vmem_mb=None):
    kw = dict(dimension_semantics=sem)
    if vmem_mb is not None:
        kw["vmem_limit_bytes"] = vmem_mb * 1024 * 1024
    return pltpu.CompilerParams(**kw)


def _norm_mod(x, g, mod, shift_row, scale_row):
    var = jnp.mean(x * x, axis=-1, keepdims=True)
    y = x * lax.rsqrt(var + EPS) * g
    return y * (1.0 + mod[scale_row:scale_row + 1, :]) + mod[shift_row:shift_row + 1, :]


def _ada_kernel(c_ref, w_ref, b_ref, o_ref):
    s = c_ref[...]
    s = s * jax.nn.sigmoid(s)
    o_ref[...] = jnp.dot(s, w_ref[...], precision=HIGHEST, preferred_element_type=F32) + b_ref[...]


def _ada_table(c_all, ada_w, ada_b):
    depth, d, nd = ada_w.shape
    out = pl.pallas_call(
        _ada_kernel,
        grid=(depth, nd // d),
        in_specs=[pl.BlockSpec((MOD_ROWS, d), lambda l, j: (0, 0)),
                  pl.BlockSpec((None, d, d), lambda l, j: (l, 0, j)),
                  pl.BlockSpec((None, 1, d), lambda l, j: (l, 0, j))],
        out_specs=pl.BlockSpec((None, MOD_ROWS, d), lambda l, j: (l, 0, j)),
        out_shape=jax.ShapeDtypeStruct((depth, MOD_ROWS, nd), F32),
        compiler_params=_cparams(("arbitrary", "arbitrary")),
        name="ada_table",
    )(c_all, ada_w, ada_b.reshape(depth, 1, nd))
    return out.reshape(depth, MOD_ROWS, N_MOD, d)


def _proj_kernel(x_ref, g_ref, mod_ref, w_ref, *o_refs, splits):
    xm = _norm_mod(x_ref[...], g_ref[...], mod_ref[...], 0, 1).astype(BF16)
    off = 0
    for o_ref, n in zip(o_refs, splits):
        for j in range(0, n, 512):
            c = min(512, n - j)
            o_ref[:, j:j + c] = jnp.dot(xm, w_ref[:, off + j:off + j + c],
                                        preferred_element_type=F32).astype(o_ref.dtype)
        off += n


def _norm_proj(h, g, mod, modrow, w_bf16, splits, dtypes):
    n, d = h.shape
    nout = w_bf16.shape[1]
    return pl.pallas_call(
        functools.partial(_proj_kernel, splits=splits),
        grid=(n // TM,),
        in_specs=[pl.BlockSpec((TM, d), lambda t: (t, 0)),
                  pl.BlockSpec((1, d), lambda t: (0, 0)),
                  pl.BlockSpec((None, N_MOD, d), lambda t: (modrow(t), 0, 0)),
                  pl.BlockSpec((d, nout), lambda t: (0, 0))],
        out_specs=[pl.BlockSpec((TM, s), lambda t: (t, 0)) for s in splits],
        out_shape=[jax.ShapeDtypeStruct((n, s), dt) for s, dt in zip(splits, dtypes)],
        compiler_params=_cparams(("arbitrary",)),
        name="norm_proj",
    )(h, g.reshape(1, d), mod, w_bf16)


def _ret_kernel(lg_ref, q_ref, k_ref, v_ref, o_ref, *, lc, l, tq):
    qi = pl.program_id(2)
    nct = lc // tq
    nk = (lc + l) // tq
    lg = lg_ref[0:1, 0:1]
    q = q_ref[...]
    rel0 = (lax.broadcasted_iota(jnp.int32, (tq, tq), 0)
            - lax.broadcasted_iota(jnp.int32, (tq, tq), 1))
    q_lat = qi >= nct

    def body(kj, acc):
        start = pl.multiple_of(kj * tq, tq)
        k = k_ref[pl.ds(start, tq), :]
        v = v_ref[pl.ds(start, tq), :]
        s = lax.dot_general(q, k, (((1,), (1,)), ((), ())), preferred_element_type=F32)
        dist = (rel0 + (qi - kj) * tq).astype(F32)
        mask = jnp.exp(lg * jnp.abs(dist))
        k_ctx = kj < nct
        extra = jnp.exp(lg * (float(l + lc) - dist))
        mask = mask + jnp.where(jnp.logical_and(q_lat, k_ctx), extra, 0.0)
        mask = jnp.where(jnp.logical_and(jnp.logical_not(q_lat), jnp.logical_not(k_ctx)), 0.0, mask)
        p = (s * mask * (RET_DK ** -0.5)).astype(BF16)
        return acc + jnp.dot(p, v, preferred_element_type=F32)

    o_ref[...] = lax.fori_loop(0, nk, body, jnp.zeros((tq, RET_DV), F32))


def _retention(q, k, v, lg, lc, l):
    b, hh, s, dk = q.shape
    dv = v.shape[-1]
    tq = TM
    return pl.pallas_call(
        functools.partial(_ret_kernel, lc=lc, l=l, tq=tq),
        grid=(b, hh, s // tq),
        in_specs=[pl.BlockSpec((None, 1, 128), lambda bi, hi, qi: (hi, 0, 0)),
                  pl.BlockSpec((None, None, tq, dk), lambda bi, hi, qi: (bi, hi, qi, 0)),
                  pl.BlockSpec((None, None, s, dk), lambda bi, hi, qi: (bi, hi, 0, 0)),
                  pl.BlockSpec((None, None, s, dv), lambda bi, hi, qi: (bi, hi, 0, 0))],
        out_specs=pl.BlockSpec((None, None, tq, dv), lambda bi, hi, qi: (bi, hi, qi, 0)),
        out_shape=jax.ShapeDtypeStruct((b, hh, s, dv), F32),
        compiler_params=_cparams(("arbitrary", "arbitrary", "arbitrary")),
        name="retention",
    )(lg, q, k, v)


def _feat_kernel(rw_ref, prev_ref, next_ref, mu_ref, w0_ref, w2_ref, a0_ref, a2_ref, g2_ref,
                 kkw_ref, ka_ref, rk_ref, ones_ref,
                 r_o, kk_o, v_o, w_o, kka_o, kt_o, gate_o, bonus_o, *, tpb, nct):
    wt = pl.program_id(0) % tpb
    first = jnp.logical_or(wt == 0, wt == nct)
    last = jnp.logical_or(wt == nct - 1, wt == tpb - 1)
    y = rw_ref[...]
    tm = y.shape[0]
    prow = jnp.where(first, 0.0, prev_ref[7:8, :])
    nrow = jnp.where(last, 0.0, next_ref[0:1, :])
    rid = lax.broadcasted_iota(jnp.int32, (tm, 1), 0)
    prev = jnp.where(rid == 0, prow, pltpu.roll(y, 1, 0))
    nxt = jnp.where(rid == tm - 1, nrow, pltpu.roll(y, tm - 1, 0))
    ys = y + mu_ref[0:1, :] * (prev - y) + mu_ref[1:2, :] * (nxt - y)

    w_ = RWKV_W
    r = ys[:, 0:w_]
    kr = ys[:, w_:2 * w_]
    vr = ys[:, 2 * w_:3 * w_]
    wd = ys[:, 3 * w_:3 * w_ + 128]
    ad = ys[:, 3 * w_ + 128:3 * w_ + 256]
    gd = ys[:, 3 * w_ + 256:3 * w_ + 512]
    ones = ones_ref[...]

    kk = kr * kkw_ref[...]
    ss = jnp.dot(kk * kk, ones, precision=HIGHEST, preferred_element_type=F32)
    kk = kk / jnp.maximum(jnp.sqrt(ss), 1e-12)
    zw = w0_ref[...] + jnp.dot(jnp.tanh(wd), w2_ref[...], precision=HIGHEST, preferred_element_type=F32)
    wlog = -(jnp.maximum(-zw, 0.0) + jnp.log1p(jnp.exp(-jnp.abs(zw)))) - 0.5
    decay = jnp.exp(-jnp.exp(wlog))
    a = jax.nn.sigmoid(a0_ref[...] + jnp.dot(ad, a2_ref[...], precision=HIGHEST, preferred_element_type=F32))
    gate_o[...] = jnp.dot(jax.nn.sigmoid(gd), g2_ref[...], precision=HIGHEST, preferred_element_type=F32)
    ka = ka_ref[...]
    r_o[...] = r
    kk_o[...] = kk
    v_o[...] = vr
    ktsum = None
    for d in range(2):
        a_d = a[:, d * w_:(d + 1) * w_]
        kt = kr * (1.0 + (a_d - 1.0) * ka)
        w_o[d] = decay[:, d * w_:(d + 1) * w_]
        kka_o[d] = kk * a_d
        kt_o[d] = kt
        ktsum = kt if ktsum is None else ktsum + kt
    bonus_o[...] = jnp.dot(r * ktsum * rk_ref[...], ones, precision=HIGHEST, preferred_element_type=F32) * vr


def _rwkv_features(rw, params, tpb, nct):
    n, wid = rw.shape
    ntile = n // TM
    rb = TM // 8
    nrb = n // 8
    w_ = RWKV_W
    tok = pl.BlockSpec((TM, w_), lambda t: (t, 0))
    tok2 = pl.BlockSpec((2, TM, w_), lambda t: (0, t, 0))

    def full(a):
        return pl.BlockSpec(a.shape, lambda t: (0,) * a.ndim)

    one = jax.ShapeDtypeStruct((n, w_), F32)
    two = jax.ShapeDtypeStruct((2, n, w_), F32)
    return pl.pallas_call(
        functools.partial(_feat_kernel, tpb=tpb, nct=nct),
        grid=(ntile,),
        in_specs=[pl.BlockSpec((TM, wid), lambda t: (t, 0)),
                  pl.BlockSpec((8, wid), lambda t: (jnp.maximum(t * rb - 1, 0), 0)),
                  pl.BlockSpec((8, wid), lambda t: (jnp.minimum((t + 1) * rb, nrb - 1), 0))]
                 + [full(a) for a in params],
        out_specs=[tok, tok, tok, tok2, tok2, tok2, tok, tok],
        out_shape=[one, one, one, two, two, two, one, one],
        compiler_params=_cparams(("arbitrary",)),
        name="rwkv_features",
    )(rw, rw, rw, *params)


def _scan_kernel(kkf, kkb, wf, wb, kaf, kab, ktf, ktb, rf, rb, vf, vb, yf_ref, yb_ref, s_ref, t_ref, y_buf):
    @pl.when(pl.program_id(0) == 0)
    def _():
        s_ref[...] = jnp.zeros_like(s_ref)

    nkey = s_ref.shape[0]
    nb, tc, nh, _ = kkf.shape
    half = nb * nh
    lanes = 2 * half
    kblk = 8
    pairs = ((kkf, kkb), (wf, wb), (kaf, kab), (ktf, ktb), (rf, rb), (vf, vb))

    def relayout(i, carry):
        for q, (f_ref, b_ref) in enumerate(pairs):
            xf = f_ref[:, i].reshape(half, nkey)
            xb = b_ref[:, tc - 1 - i].reshape(half, nkey)
            t_ref[i, q] = jnp.concatenate([xf, xb], axis=0).T
        return carry

    lax.fori_loop(0, tc, relayout, 0, unroll=2)

    def step(i, carry):
        def row(q, k):
            return t_ref[i, q, pl.ds(k, 1), :][None]

        def sa_body(kb, sa):
            base = pl.multiple_of(kb * kblk, kblk)
            for j in range(kblk):
                sa = sa + s_ref[base + j] * row(0, base + j)
            return sa

        zero = jnp.zeros((nkey // 8, 8, lanes), F32)
        sa = lax.fori_loop(0, nkey // kblk, sa_body, zero)
        v = t_ref[i, 5].reshape(nkey // 8, 8, lanes)

        def upd_body(kb, y):
            base = pl.multiple_of(kb * kblk, kblk)
            for j in range(kblk):
                k = base + j
                s_new = s_ref[k] * row(1, k) + (v * row(3, k) - sa * row(2, k))
                s_ref[k] = s_new
                y = y + s_new * row(4, k)
            return y

        y_buf[i] = lax.fori_loop(0, nkey // kblk, upd_body, zero)
        return carry

    lax.fori_loop(0, tc, step, 0)

    def emit(i, carry):
        yt = y_buf[i].reshape(nkey, lanes).T
        yf_ref[:, i] = yt[:half].reshape(nb, nh, nkey)
        yb_ref[:, tc - 1 - i] = yt[half:].reshape(nb, nh, nkey)
        return carry

    lax.fori_loop(0, tc, emit, 0, unroll=2)


def _rwkv_scan(kk, w, kka, kt, r, v, lc):
    nb, s, nh, nkey = kk.shape
    tc = SCAN_TC
    nctb = lc // tc
    ntb = s // tc

    def mirror(g):
        return jnp.where(g < nctb, nctb - 1 - g, nctb + ntb - 1 - g)

    blk = (nb, tc, nh, nkey)
    sf = pl.BlockSpec(blk, lambda g: (0, g, 0, 0))
    sb = pl.BlockSpec(blk, lambda g: (0, mirror(g), 0, 0))
    df = pl.BlockSpec((None,) + blk, lambda g: (0, 0, g, 0, 0))
    db = pl.BlockSpec((None,) + blk, lambda g: (1, 0, mirror(g), 0, 0))
    out = jax.ShapeDtypeStruct((nb, s, nh, nkey), F32)
    lanes = 2 * nb * nh
    return pl.pallas_call(
        _scan_kernel,
        grid=(ntb,),
        in_specs=[sf, sb, df, db, df, db, df, db, sf, sb, sf, sb],
        out_specs=[sf, sb],
        out_shape=[out, out],
        scratch_shapes=[pltpu.VMEM((nkey, nkey // 8, 8, lanes), F32),
                        pltpu.VMEM((tc, 6, nkey, lanes), F32),
                        pltpu.VMEM((tc, nkey // 8, 8, lanes), F32)],
        compiler_params=_cparams(("arbitrary",)),
        name="rwkv_scan",
    )(kk, kk, w, w, kka, kka, kt, kt, r, r, v, v)


def _even_out_kernel(o_ref, g_ref, yf_ref, yb_ref, bonus_ref, gate_ref, lnw_ref, lnb_ref, ones_ref,
                     w_ref, h_ref, mod_ref, out_ref):
    parts = []
    for hh in range(RET_HEADS):
        o = o_ref[hh]
        o = o * lax.rsqrt(jnp.mean(o * o, axis=-1, keepdims=True) + EPS)
        gg = g_ref[:, hh * RET_DV:(hh + 1) * RET_DV]
        parts.append(o * (gg * jax.nn.sigmoid(gg)))
    ones = ones_ref[...]
    y = yf_ref[...] + yb_ref[...]
    mu = jnp.dot(y, ones, precision=HIGHEST, preferred_element_type=F32) * (1.0 / RWKV_N)
    yc = y - mu
    var = jnp.dot(yc * yc, ones, precision=HIGHEST, preferred_element_type=F32) * (1.0 / RWKV_N)
    yn = yc * lax.rsqrt(var + GN_EPS) * lnw_ref[...] + lnb_ref[...]
    parts.append((yn + bonus_ref[...]) * gate_ref[...])
    cat = jnp.concatenate(parts, axis=-1).astype(BF16)
    mix = jnp.dot(cat, w_ref[...], preferred_element_type=F32)
    out_ref[...] = h_ref[...] + mod_ref[2:3, :] * mix


def _even_out(o_ret, g, y2, bonus, gate, lnw, lnb, ones, w_out, h, mod, modrow, tpb):
    n, d = h.shape
    w_ = RWKV_W
    tok = pl.BlockSpec((TM, w_), lambda t: (t, 0))

    def full(a):
        return pl.BlockSpec(a.shape, lambda t: (0,) * a.ndim)

    return pl.pallas_call(
        _even_out_kernel,
        grid=(n // TM,),
        in_specs=[pl.BlockSpec((None, RET_HEADS, TM, RET_DV), lambda t: (t // tpb, 0, t % tpb, 0)),
                  tok, tok, tok, tok, tok, full(lnw), full(lnb), full(ones), full(w_out),
                  pl.BlockSpec((TM, d), lambda t: (t, 0)),
                  pl.BlockSpec((None, N_MOD, d), lambda t: (modrow(t), 0, 0))],
        out_specs=pl.BlockSpec((TM, d), lambda t: (t, 0)),
        out_shape=jax.ShapeDtypeStruct((n, d), F32),
        compiler_params=_cparams(("arbitrary",)),
        name="even_out",
    )(o_ret, g, y2[0], y2[1], bonus, gate, lnw, lnb, ones, w_out, h, mod)


def _mla_proj_kernel(a_ref, qg_ref, kvg_ref, wuq_ref, wukv_ref, cos_ref, sin_ref,
                     qn_o, qp_o, kn_o, v_o, kp_o):
    a = a_ref[...]
    cos = cos_ref[...]
    sin = sin_ref[...]

    def rms(t, g):
        return (t * lax.rsqrt(jnp.mean(t * t, axis=-1, keepdims=True) + EPS) * g).astype(BF16)

    cq = rms(a[:, :Q_RANK], qg_ref[...])
    ckv = rms(a[:, Q_RANK:Q_RANK + KV_RANK], kvg_ref[...])
    pe0 = Q_RANK + KV_RANK
    kp_o[...] = (a[:, pe0:pe0 + QK_ROPE] * cos + a[:, pe0 + QK_ROPE:pe0 + 2 * QK_ROPE] * sin).astype(BF16)
    hw = QK_NOPE + 2 * QK_ROPE
    for hh in range(MLA_HEADS):
        qh = jnp.dot(cq, wuq_ref[:, hh * hw:(hh + 1) * hw], preferred_element_type=F32)
        qn_o[hh] = qh[:, :QK_NOPE].astype(BF16)
        qp_o[hh] = (qh[:, QK_NOPE:QK_NOPE + QK_ROPE] * cos + qh[:, QK_NOPE + QK_ROPE:] * sin).astype(BF16)
        kvh = jnp.dot(ckv, wukv_ref[:, hh * hw:(hh + 1) * hw], preferred_element_type=F32)
        kn_o[hh] = kvh[:, :QK_NOPE].astype(BF16)
        v_o[hh] = kvh[:, QK_NOPE:].astype(BF16)


def _mla_proj(a, qg, kvg, wuq, wukv, cos, sin, tpb):
    n, wid = a.shape
    hd = MLA_HEADS

    def full(x):
        return pl.BlockSpec(x.shape, lambda t: (0,) * x.ndim)

    h128 = pl.BlockSpec((hd, TM, 128), lambda t: (0, t, 0))
    rope = pl.BlockSpec((TM, QK_ROPE), lambda t: (t % tpb, 0))
    return pl.pallas_call(
        _mla_proj_kernel,
        grid=(n // TM,),
        in_specs=[pl.BlockSpec((TM, wid), lambda t: (t, 0)), full(qg), full(kvg), full(wuq), full(wukv),
                  rope, rope],
        out_specs=[h128, pl.BlockSpec((hd, TM, QK_ROPE), lambda t: (0, t, 0)), h128, h128,
                   pl.BlockSpec((TM, QK_ROPE), lambda t: (t, 0))],
        out_shape=[jax.ShapeDtypeStruct((hd, n, 128), BF16),
                   jax.ShapeDtypeStruct((hd, n, QK_ROPE), BF16),
                   jax.ShapeDtypeStruct((hd, n, 128), BF16),
                   jax.ShapeDtypeStruct((hd, n, 128), BF16),
                   jax.ShapeDtypeStruct((n, QK_ROPE), BF16)],
        compiler_params=_cparams(("arbitrary",)),
        name="mla_proj",
    )(a, qg, kvg, wuq, wukv, cos, sin)


def _attn_kernel(qn_ref, qp_ref, kn_ref, kp_ref, v_ref, o_ref):
    dn = (((1,), (1,)), ((), ()))
    s = (lax.dot_general(qn_ref[...], kn_ref[...], dn, preferred_element_type=F32)
         + lax.dot_general(qp_ref[...], kp_ref[...], dn, preferred_element_type=F32)) * MLA_SCALE
    m = jnp.max(s, axis=-1, keepdims=True)
    p = jnp.exp(s - m)
    l = jnp.sum(p, axis=-1, keepdims=True)
    o = jnp.dot(p.astype(BF16), v_ref[...], preferred_element_type=F32)
    o_ref[...] = (o / l).astype(o_ref.dtype)


def _attention(qn, qp, kn, kp, v, b, lc, l):
    hd = MLA_HEADS
    s = lc + l
    tpb = s // TM
    nct = lc // TM
    nq = l // TM
    return pl.pallas_call(
        _attn_kernel,
        grid=(b, hd, nq),
        in_specs=[pl.BlockSpec((None, TM, 128), lambda bi, hi, qi: (hi, bi * tpb + nct + qi, 0)),
                  pl.BlockSpec((None, TM, QK_ROPE), lambda bi, hi, qi: (hi, bi * tpb + nct + qi, 0)),
                  pl.BlockSpec((None, s, 128), lambda bi, hi, qi: (hi, bi, 0)),
                  pl.BlockSpec((s, QK_ROPE), lambda bi, hi, qi: (bi, 0)),
                  pl.BlockSpec((None, s, 128), lambda bi, hi, qi: (hi, bi, 0))],
        out_specs=pl.BlockSpec((TM, V_HEAD), lambda bi, hi, qi: (bi * nq + qi, hi)),
        out_shape=jax.ShapeDtypeStruct((b * l, hd * V_HEAD), BF16),
        compiler_params=_cparams(("arbitrary", "arbitrary", "arbitrary")),
        name="mla_attention",
    )(qn, qp, kn, kp, v)


def _oproj_kernel(o_ref, w_ref, h_ref, mod_ref, out_ref):
    mix = jnp.dot(o_ref[...], w_ref[...], preferred_element_type=F32)
    out_ref[...] = h_ref[...] + mod_ref[2:3, :] * mix


def _oproj(o, w_o, h, mod, hrow, modrow):
    n, d = o.shape[0], h.shape[1]
    return pl.pallas_call(
        _oproj_kernel,
        grid=(n // TM,),
        in_specs=[pl.BlockSpec((TM, o.shape[1]), lambda t: (t, 0)),
                  pl.BlockSpec(w_o.shape, lambda t: (0, 0)),
                  pl.BlockSpec((TM, d), lambda t: (hrow(t), 0)),
                  pl.BlockSpec((None, N_MOD, d), lambda t: (modrow(t), 0, 0))],
        out_specs=pl.BlockSpec((TM, d), lambda t: (t, 0)),
        out_shape=jax.ShapeDtypeStruct((n, d), F32),
        compiler_params=_cparams(("arbitrary",)),
        name="mla_oproj",
    )(o, w_o, h, mod)


def _route_kernel(h_ref, g_ref, mod_ref, wr_ref, br_ref, xl_ref, route_ref):
    xl = _norm_mod(h_ref[...], g_ref[...], mod_ref[...], 3, 4)
    xl_ref[...] = xl
    logits = jnp.dot(xl, wr_ref[...], precision=HIGHEST, preferred_element_type=F32) + br_ref[...]
    lane_i = lax.broadcasted_iota(jnp.int32, logits.shape, 1)
    lane = lane_i.astype(F32)
    lane_grp = (lane_i >> 3).astype(F32)
    neg = -jnp.inf
    big = 1e6
    gl = jnp.where(jnp.logical_and(lane_i >= N_EXPERTS, lane_i < N_EXPERTS + N_GROUPS), logits, neg)
    gmax = jnp.max(gl, axis=-1, keepdims=True)
    gsum = jnp.sum(jnp.exp(gl - gmax), axis=-1, keepdims=True)
    pg = 1.0 / gsum
    gidx = jnp.min(jnp.where(gl == gmax, lane - N_EXPERTS, big), axis=-1, keepdims=True)
    in_grp = jnp.logical_and(lane_i < N_EXPERTS, lane_grp == gidx)
    el = jnp.where(in_grp, logits, neg)
    emax = jnp.max(el, axis=-1, keepdims=True)
    esum = jnp.sum(jnp.exp(el - emax), axis=-1, keepdims=True)
    i1 = jnp.min(jnp.where(el == emax, lane, big), axis=-1, keepdims=True)
    el2 = jnp.where(lane == i1, neg, el)
    emax2 = jnp.max(el2, axis=-1, keepdims=True)
    i2 = jnp.min(jnp.where(el2 == emax2, lane, big), axis=-1, keepdims=True)
    pe1 = 1.0 / esum
    pe2 = jnp.exp(emax2 - emax) / esum
    den = pe1 + pe2
    w1 = pg * pe1 / den
    w2 = pg * pe2 / den
    route_ref[...] = jnp.where(lane_i == 0, i1,
                               jnp.where(lane_i == 1, i2,
                                         jnp.where(lane_i == 2, w1, jnp.where(lane_i == 3, w2, 0.0))))


def _route(h, g, mod, modrow, wr, br):
    n, d = h.shape
    return pl.pallas_call(
        _route_kernel,
        grid=(n // TM,),
        in_specs=[pl.BlockSpec((TM, d), lambda t: (t, 0)),
                  pl.BlockSpec((1, d), lambda t: (0, 0)),
                  pl.BlockSpec((None, N_MOD, d), lambda t: (modrow(t), 0, 0)),
                  pl.BlockSpec(wr.shape, lambda t: (0, 0)),
                  pl.BlockSpec(br.shape, lambda t: (0, 0))],
        out_specs=[pl.BlockSpec((TM, d), lambda t: (t, 0)),
                   pl.BlockSpec((TM, 128), lambda t: (t, 0))],
        out_shape=[jax.ShapeDtypeStruct((n, d), F32), jax.ShapeDtypeStruct((n, 128), F32)],
        compiler_params=_cparams(("arbitrary",)),
        name="moe_route",
    )(h, g.reshape(1, d), mod, wr, br)


def _expert_kernel(te_ref, nt_ref, x_ref, wg_ref, wu_ref, wd_ref, o_ref, wg_s, wu_s, wd_s):
    t = pl.program_id(0)

    @pl.when(t < nt_ref[0])
    def _():
        changed = jnp.logical_or(t == 0, te_ref[t] != te_ref[jnp.maximum(t - 1, 0)])

        @pl.when(changed)
        def _():
            wg_s[...] = wg_ref[...].astype(BF16)
            wu_s[...] = wu_ref[...].astype(BF16)
            wd_s[...] = wd_ref[...].astype(BF16)

        x = x_ref[...].astype(BF16)
        h1 = jnp.dot(x, wg_s[...], preferred_element_type=F32)
        h2 = jnp.dot(x, wu_s[...], preferred_element_type=F32)
        hid = ((h1 * jax.nn.sigmoid(h1)) * h2).astype(BF16)
        o_ref[...] = jnp.dot(hid, wd_s[...], preferred_element_type=F32)

    @pl.when(t >= nt_ref[0])
    def _():
        o_ref[...] = jnp.zeros_like(o_ref)


def _experts(tile_e, ntiles, xs, w_gate, w_up, w_down, layer):
    rows, d = xs.shape
    hid = w_gate.shape[-1]
    grid_spec = pltpu.PrefetchScalarGridSpec(
        num_scalar_prefetch=2,
        grid=(rows // TMOE,),
        in_specs=[pl.BlockSpec((TMOE, d), lambda t, te, nt: (t, 0)),
                  pl.BlockSpec((None, None, d, hid), lambda t, te, nt: (layer, te[t], 0, 0)),
                  pl.BlockSpec((None, None, d, hid), lambda t, te, nt: (layer, te[t], 0, 0)),
                  pl.BlockSpec((None, None, hid, d), lambda t, te, nt: (layer, te[t], 0, 0))],
        out_specs=pl.BlockSpec((TMOE, d), lambda t, te, nt: (t, 0)),
        scratch_shapes=[pltpu.VMEM((d, hid), BF16), pltpu.VMEM((d, hid), BF16), pltpu.VMEM((hid, d), BF16)],
    )
    return pl.pallas_call(
        _expert_kernel,
        grid_spec=grid_spec,
        out_shape=jax.ShapeDtypeStruct((rows, d), F32),
        compiler_params=_cparams(("arbitrary",)),
        name="moe_experts",
    )(tile_e, ntiles, xs, w_gate, w_up, w_down)


def _combine_kernel(y1_ref, y2_ref, route_ref, h_ref, mod_ref, fg_ref, out_ref, *, final):
    route = route_ref[...]
    y = route[:, 2:3] * y1_ref[...] + route[:, 3:4] * y2_ref[...]
    hn = h_ref[...] + mod_ref[5:6, :] * y
    if final:
        hn = hn * lax.rsqrt(jnp.mean(hn * hn, axis=-1, keepdims=True) + EPS) * fg_ref[...]
    out_ref[...] = hn


def _combine(y1, y2, route, h, mod, modrow, fg, final):
    n, d = h.shape
    tok = pl.BlockSpec((TM, d), lambda t: (t, 0))
    return pl.pallas_call(
        functools.partial(_combine_kernel, final=final),
        grid=(n // TM,),
        in_specs=[tok, tok, pl.BlockSpec((TM, 128), lambda t: (t, 0)), tok,
                  pl.BlockSpec((None, N_MOD, d), lambda t: (modrow(t), 0, 0)),
                  pl.BlockSpec((1, d), lambda t: (0, 0))],
        out_specs=tok,
        out_shape=jax.ShapeDtypeStruct((n, d), F32),
        compiler_params=_cparams(("arbitrary",)),
        name="moe_combine",
    )(y1, y2, route, h, mod, fg.reshape(1, d))


def _moe(h, g, mod, modrow, wr, br, w_gate, w_up, w_down, layer, fg, final):
    n, d = h.shape
    xl, route = _route(h, g, mod, modrow, wr, br)
    e = route[:, :2].astype(jnp.int32).reshape(-1)
    onehot = (e[:, None] == jnp.arange(N_EXPERTS, dtype=jnp.int32)[None, :]).astype(jnp.int32)
    csum = jnp.cumsum(onehot, axis=0)
    counts = csum[-1]
    rank = jnp.sum(csum * onehot, axis=1) - 1
    padded = ((counts + TMOE - 1) // TMOE) * TMOE
    pend = jnp.cumsum(padded)
    pos = (pend - padded)[e] + rank
    rows = 2 * n + N_EXPERTS * TMOE
    src = jnp.zeros((rows,), jnp.int32).at[pos].set(jnp.arange(2 * n, dtype=jnp.int32) // 2,
                                                     mode="promise_in_bounds", unique_indices=True)
    ntile = rows // TMOE
    nvalid = (pend[-1] // TMOE).astype(jnp.int32)
    tstart = jnp.arange(ntile, dtype=jnp.int32) * TMOE
    tile_e = jnp.sum((tstart[:, None] >= pend[None, :]).astype(jnp.int32), axis=1)
    last_e = jnp.sum((((nvalid - 1) * TMOE) >= pend).astype(jnp.int32))
    tile_e = jnp.where(tstart < pend[-1], tile_e, last_e).astype(jnp.int32)
    xs = xl.at[src].get(mode="promise_in_bounds")
    ys = _experts(tile_e, nvalid.reshape(1), xs, w_gate, w_up, w_down, layer)
    pos2 = pos.reshape(n, 2)
    y1 = ys.at[pos2[:, 0]].get(mode="promise_in_bounds", unique_indices=True)
    y2 = ys.at[pos2[:, 1]].get(mode="promise_in_bounds", unique_indices=True)
    return _combine(y1, y2, route, h, mod, modrow, fg, final)


def _block_ones(width, seg):
    idx = np.arange(width) // seg
    return jnp.asarray((idx[:, None] == idx[None, :]).astype(np.float32))


def _rope_tables(lc, l):
    rows = l // GRID_W
    row = np.repeat(np.arange(rows, dtype=np.float32), GRID_W)
    col = np.tile(np.arange(GRID_W, dtype=np.float32), rows)
    n_freq = QK_ROPE // 4
    inv_freq = jnp.asarray(ROPE_BASE, F32) ** (-jnp.arange(n_freq, dtype=F32) / n_freq)
    ang_r = jnp.asarray(row)[:, None] * inv_freq
    ang_c = jnp.asarray(col)[:, None] * inv_freq
    cos = jnp.concatenate([jnp.cos(ang_r), jnp.cos(ang_r), jnp.cos(ang_c), jnp.cos(ang_c)], axis=-1)
    sin = jnp.concatenate([jnp.sin(ang_r), jnp.sin(ang_r), jnp.sin(ang_c), jnp.sin(ang_c)], axis=-1)
    cos = jnp.concatenate([jnp.ones((lc, QK_ROPE), F32), cos], axis=0)
    sin = jnp.concatenate([jnp.zeros((lc, QK_ROPE), F32), sin], axis=0)
    return cos, sin


def _rot_cols(pe):
    q = QK_ROPE // 4
    return jnp.concatenate([-pe[..., q:2 * q], pe[..., 0:q], -pe[..., 3 * q:4 * q], pe[..., 2 * q:3 * q]], axis=-1)


def _pad_cols(w, n):
    return jnp.pad(w, ((0, 0), (0, n - w.shape[1])))


def kernel(x, c, ctx, c_ctx, ada_w, ada_b, norm1_g, norm2_g, final_g, ev_w_in, ev_shift_mu, rwkv_w0, rwkv_w2, rwkv_a0, rwkv_a2, rwkv_g2, rwkv_k_k, rwkv_k_a, rwkv_r_k, rwkv_lnx_w, rwkv_lnx_b, ev_w_out, mla_w_in, mla_q_norm_g, mla_w_uq, mla_kv_norm_g, mla_w_ukv, mla_w_o, moe_w_grp, moe_b_grp, moe_w_exp, moe_b_exp, moe_w_gate, moe_w_up, moe_w_down):
    b, l, d = x.shape
    lc = ctx.shape[1]
    s = lc + l
    n = b * s
    tpb = s // TM
    nct = lc // TM
    nlt = l // TM
    assert lc % TM == 0 and l % TM == 0 and b + 1 <= MOD_ROWS and ada_w.shape[0] == 2

    def modrow_all(t):
        return jnp.where(t % tpb < nct, b, t // tpb)

    def modrow_lat(t):
        return t // nlt

    def lat_tile(t):
        return (t // nlt) * tpb + nct + t % nlt

    c_all = jnp.concatenate([c, c_ctx[None], jnp.zeros((MOD_ROWS - b - 1, d), F32)], axis=0)
    mod = _ada_table(c_all, ada_w, ada_b)
    h = jnp.concatenate([ctx, x], axis=1).reshape(n, d)

    def router_weights(layer):
        wr = jnp.concatenate([moe_w_exp[layer], moe_w_grp[layer]], axis=1)
        br = jnp.concatenate([moe_b_exp[layer], moe_b_grp[layer]])[None]
        return _pad_cols(wr, 128), _pad_cols(br, 128)

    w_ = RWKV_W
    ret_w = 2 * RET_HEADS * RET_DK + RET_HEADS * RET_DV
    w_in = _pad_cols(ev_w_in[0], 3584).astype(BF16)
    qkv, gret, rw = _norm_proj(h, norm1_g[0], mod[0], modrow_all, w_in,
                               (ret_w, RET_HEADS * RET_DV, 2048), (BF16, F32, F32))

    def heads(t, dh):
        return t.reshape(b, s, RET_HEADS, dh).transpose(0, 2, 1, 3)

    nqk = RET_HEADS * RET_DK
    lg = jnp.log1p(-jnp.exp2(-5.0 - jnp.arange(RET_HEADS, dtype=F32)))
    o_ret = _retention(heads(qkv[:, :nqk], RET_DK), heads(qkv[:, nqk:2 * nqk], RET_DK),
                       heads(qkv[:, 2 * nqk:], RET_DV),
                       jnp.broadcast_to(lg[:, None, None], (RET_HEADS, 1, 128)), lc, l)

    ones8 = _block_ones(w_, RWKV_N)
    zero = jnp.zeros((64, w_), F32)
    w2bd = jnp.concatenate([jnp.concatenate([rwkv_w2[0, 0], zero], axis=1),
                            jnp.concatenate([zero, rwkv_w2[0, 1]], axis=1)], axis=0)
    a2bd = jnp.concatenate([jnp.concatenate([rwkv_a2[0, 0], zero], axis=1),
                            jnp.concatenate([zero, rwkv_a2[0, 1]], axis=1)], axis=0)
    g2p = jnp.pad(rwkv_g2[0], ((0, 256 - rwkv_g2.shape[1]), (0, 0)))
    mu = _pad_cols(ev_shift_mu[0], 2048)
    feat_params = (mu, rwkv_w0[0].reshape(1, 2 * w_), w2bd, rwkv_a0[0].reshape(1, 2 * w_), a2bd, g2p,
                   rwkv_k_k[0][None], rwkv_k_a[0][None], rwkv_r_k[0].reshape(1, w_), ones8)
    r_t, kk_t, v_t, w_t, kka_t, kt_t, gate, bonus = _rwkv_features(rw, feat_params, tpb, nct)

    def bshn(t):
        return t.reshape(t.shape[:-2] + (b, s, RWKV_HEADS, RWKV_N))

    y_f, y_b = _rwkv_scan(bshn(kk_t), bshn(w_t), bshn(kka_t), bshn(kt_t), bshn(r_t), bshn(v_t), lc)
    y2 = (y_f.reshape(n, w_), y_b.reshape(n, w_))

    h = _even_out(o_ret, gret, y2, bonus, gate, rwkv_lnx_w[0][None], rwkv_lnx_b[0][None], ones8,
                  ev_w_out[0].astype(BF16), h, mod[0], modrow_all, tpb)
    wr, br = router_weights(0)
    h = _moe(h, norm2_g[0], mod[0], modrow_all, wr, br, moe_w_gate, moe_w_up, moe_w_down, 0, final_g, False)

    w_in1 = jnp.concatenate([mla_w_in[0], _rot_cols(mla_w_in[0][:, Q_RANK + KV_RANK:])], axis=1).astype(BF16)
    (a1,) = _norm_proj(h, norm1_g[1], mod[1], modrow_all, w_in1, (w_in1.shape[1],), (F32,))
    wq = mla_w_uq[0].reshape(Q_RANK, MLA_HEADS, QK_NOPE + QK_ROPE)
    wq = jnp.concatenate([wq, _rot_cols(wq[..., QK_NOPE:])], axis=-1).reshape(Q_RANK, -1).astype(BF16)
    cos, sin = _rope_tables(lc, l)
    qn, qp, kn, vv, kp = _mla_proj(a1, mla_q_norm_g[0][None], mla_kv_norm_g[0][None], wq,
                                   mla_w_ukv[0].astype(BF16), cos, sin, tpb)
    o = _attention(qn, qp, kn, kp, vv, b, lc, l)
    h = _oproj(o, mla_w_o[0].astype(BF16), h, mod[1], lat_tile, modrow_lat)
    wr, br = router_weights(1)
    h = _moe(h, norm2_g[1], mod[1], modrow_lat, wr, br, moe_w_gate, moe_w_up, moe_w_down, 1, final_g, True)
    return h.reshape(b, l, d)
```

```python
import functools

import jax
import jax.numpy as jnp
import numpy as np
from jax import lax
from jax.experimental import pallas as pl
from jax.experimental.pallas import tpu as pltpu

F32 = jnp.float32
BF16 = jnp.bfloat16
HIGHEST = lax.Precision.HIGHEST

TM = 256
TQ_ATTN = 256
TMOE = 512
SCAN_TC = 32
EPS = 1e-6
GN_EPS = 64e-5
GRID_W = 64
ROPE_BASE = 10000.0

RET_HEADS, RET_DK, RET_DV = 4, 64, 128
RWKV_HEADS, RWKV_N = 8, 64
RWKV_W = RWKV_HEADS * RWKV_N
MLA_HEADS, Q_RANK, KV_RANK, QK_NOPE, QK_ROPE, V_HEAD = 8, 384, 256, 128, 64, 128
MLA_SCALE = (QK_NOPE + QK_ROPE) ** -0.5
N_GROUPS, EXPERTS_PER_GROUP = 4, 8
N_EXPERTS = N_GROUPS * EXPERTS_PER_GROUP
N_MOD = 6
MOD_ROWS = 16


def _cparams(sem, vmem_mb=None):
    kw = dict(dimension_semantics=sem)
    if vmem_mb is not None:
        kw["vmem_limit_bytes"] = vmem_mb * 1024 * 1024
    return pltpu.CompilerParams(**kw)


def _norm_mod(x, g, mod, shift_row, scale_row):
    var = jnp.mean(x * x, axis=-1, keepdims=True)
    y = x * lax.rsqrt(var + EPS) * g
    return y * (1.0 + mod[scale_row:scale_row + 1, :]) + mod[shift_row:shift_row + 1, :]


def _ada_kernel(c_ref, w_ref, b_ref, o_ref):
    s = c_ref[...]
    s = s * jax.nn.sigmoid(s)
    o_ref[...] = jnp.dot(s, w_ref[...], precision=HIGHEST, preferred_element_type=F32) + b_ref[...]


def _ada_table(c_all, ada_w, ada_b):
    depth, d, nd = ada_w.shape
    out = pl.pallas_call(
        _ada_kernel,
        grid=(depth, nd // d),
        in_specs=[pl.BlockSpec((MOD_ROWS, d), lambda l, j: (0, 0)),
                  pl.BlockSpec((None, d, d), lambda l, j: (l, 0, j)),
                  pl.BlockSpec((None, 1, d), lambda l, j: (l, 0, j))],
        out_specs=pl.BlockSpec((None, MOD_ROWS, d), lambda l, j: (l, 0, j)),
        out_shape=jax.ShapeDtypeStruct((depth, MOD_ROWS, nd), F32),
        compiler_params=_cparams(("arbitrary", "arbitrary")),
        name="ada_table",
    )(c_all, ada_w, ada_b.reshape(depth, 1, nd))
    return out.reshape(depth, MOD_ROWS, N_MOD, d)


def _proj_kernel(x_ref, g_ref, mod_ref, w_ref, *o_refs, splits):
    xm = _norm_mod(x_ref[...], g_ref[...], mod_ref[...], 0, 1).astype(BF16)
    off = 0
    for o_ref, n in zip(o_refs, splits):
        for j in range(0, n, 512):
            c = min(512, n - j)
            o_ref[:, j:j + c] = jnp.dot(xm, w_ref[:, off + j:off + j + c],
                                        preferred_element_type=F32).astype(o_ref.dtype)
        off += n


def _norm_proj(h, g, mod, modrow, w_bf16, splits, dtypes):
    n, d = h.shape
    nout = w_bf16.shape[1]
    return pl.pallas_call(
        functools.partial(_proj_kernel, splits=splits),
        grid=(n // TM,),
        in_specs=[pl.BlockSpec((TM, d), lambda t: (t, 0)),
                  pl.BlockSpec((1, d), lambda t: (0, 0)),
                  pl.BlockSpec((None, N_MOD, d), lambda t: (modrow(t), 0, 0)),
                  pl.BlockSpec((d, nout), lambda t: (0, 0))],
        out_specs=[pl.BlockSpec((TM, s), lambda t: (t, 0)) for s in splits],
        out_shape=[jax.ShapeDtypeStruct((n, s), dt) for s, dt in zip(splits, dtypes)],
        compiler_params=_cparams(("arbitrary",)),
        name="norm_proj",
    )(h, g.reshape(1, d), mod, w_bf16)


def _ret_kernel(lg_ref, q_ref, k_ref, v_ref, o_ref, g_ref, *, lc, l, tq):
    qi = pl.program_id(2)
    nct = lc // tq
    nk = (lc + l) // tq
    dn = (((1,), (1,)), ((), ()))
    scale = RET_DK ** -0.5

    @pl.when(qi == 0)
    def _():
        rel = (lax.broadcasted_iota(jnp.int32, (tq, tq), 0)
               - lax.broadcasted_iota(jnp.int32, (tq, tq), 1)).astype(F32)
        for hh in range(2):
            lg = lg_ref[hh][0:1, 0:1]
            g_ref[hh, 0] = jnp.exp(lg * rel)
            g_ref[hh, 1] = jnp.exp(-(lg * rel))
            g_ref[hh, 2] = jnp.exp(lg * jnp.abs(rel))

    q_lat = qi >= nct
    q2 = q_ref[...]
    for hh in range(2):
        lg = lg_ref[hh][0:1, 0:1]
        s = lax.dot_general(q2[:, hh * RET_DK:(hh + 1) * RET_DK], k_ref[:, hh * RET_DK:(hh + 1) * RET_DK], dn,
                            preferred_element_type=F32)
        pieces = []
        for kj in range(nk):
            d = qi - kj
            idx = jnp.where(d > 0, 0, jnp.where(d < 0, 1, 2))
            dabs = jnp.full((1, 1), jnp.abs(d) * tq, jnp.int32).astype(F32)
            sig = jnp.exp(lg * dabs) * scale
            if kj < nct:
                dback = jnp.full((1, 1), l + lc - d * tq, jnp.int32).astype(F32)
                sig2 = jnp.where(q_lat, jnp.exp(lg * dback) * scale, 0.0)
                m = g_ref[hh, idx] * sig + g_ref[hh, 1] * sig2
            else:
                m = g_ref[hh, idx] * jnp.where(q_lat, sig, 0.0)
            pieces.append((s[:, kj * tq:(kj + 1) * tq] * m).astype(BF16))
        p = jnp.concatenate(pieces, axis=1)
        o_ref[:, hh * RET_DV:(hh + 1) * RET_DV] = jnp.dot(p, v_ref[:, hh * RET_DV:(hh + 1) * RET_DV],
                                                          preferred_element_type=F32)


def _retention(qkv, lg, b, lc, l):
    s = lc + l
    tq = TM
    tpb = s // tq
    qk_w = 2 * RET_DK
    v_w = 2 * RET_DV
    k_blk0 = RET_HEADS * RET_DK // qk_w
    v_blk0 = 2 * RET_HEADS * RET_DK // v_w
    return pl.pallas_call(
        functools.partial(_ret_kernel, lc=lc, l=l, tq=tq),
        grid=(b, RET_HEADS // 2, tpb),
        in_specs=[pl.BlockSpec((2, 1, 128), lambda bi, hp, qi: (hp, 0, 0)),
                  pl.BlockSpec((tq, qk_w), lambda bi, hp, qi: (bi * tpb + qi, hp)),
                  pl.BlockSpec((s, qk_w), lambda bi, hp, qi: (bi, k_blk0 + hp)),
                  pl.BlockSpec((s, v_w), lambda bi, hp, qi: (bi, v_blk0 + hp))],
        out_specs=pl.BlockSpec((tq, v_w), lambda bi, hp, qi: (bi * tpb + qi, hp)),
        out_shape=jax.ShapeDtypeStruct((b * s, RET_HEADS * RET_DV), F32),
        scratch_shapes=[pltpu.VMEM((2, 3, tq, tq), F32)],
        compiler_params=_cparams(("arbitrary", "arbitrary", "arbitrary")),
        name="retention",
    )(lg, qkv, qkv, qkv)


def _feat_kernel(rw_ref, prev_ref, next_ref, mu_ref, w0_ref, w2_ref, a0_ref, a2_ref, g2_ref,
                 kkw_ref, ka_ref, rk_ref, ones_ref,
                 r_o, kk_o, v_o, w_o, kka_o, kt_o, gate_o, bonus_o, *, tpb, nct):
    wt = pl.program_id(0) % tpb
    first = jnp.logical_or(wt == 0, wt == nct)
    last = jnp.logical_or(wt == nct - 1, wt == tpb - 1)
    y = rw_ref[...]
    tm = y.shape[0]
    prow = jnp.where(first, 0.0, prev_ref[7:8, :])
    nrow = jnp.where(last, 0.0, next_ref[0:1, :])
    rid = lax.broadcasted_iota(jnp.int32, (tm, 1), 0)
    prev = jnp.where(rid == 0, prow, pltpu.roll(y, 1, 0))
    nxt = jnp.where(rid == tm - 1, nrow, pltpu.roll(y, tm - 1, 0))
    ys = y + mu_ref[0:1, :] * (prev - y) + mu_ref[1:2, :] * (nxt - y)

    w_ = RWKV_W
    r = ys[:, 0:w_]
    kr = ys[:, w_:2 * w_]
    vr = ys[:, 2 * w_:3 * w_]
    wd = ys[:, 3 * w_:3 * w_ + 128]
    ad = ys[:, 3 * w_ + 128:3 * w_ + 256]
    gd = ys[:, 3 * w_ + 256:3 * w_ + 512]
    ones = ones_ref[...]

    kk = kr * kkw_ref[...]
    ss = jnp.dot(kk * kk, ones, precision=HIGHEST, preferred_element_type=F32)
    kk = kk / jnp.maximum(jnp.sqrt(ss), 1e-12)
    zw = w0_ref[...] + jnp.dot(jnp.tanh(wd), w2_ref[...], precision=HIGHEST, preferred_element_type=F32)
    wlog = -(jnp.maximum(-zw, 0.0) + jnp.log1p(jnp.exp(-jnp.abs(zw)))) - 0.5
    decay = jnp.exp(-jnp.exp(wlog))
    a = jax.nn.sigmoid(a0_ref[...] + jnp.dot(ad, a2_ref[...], precision=HIGHEST, preferred_element_type=F32))
    gate_o[...] = jnp.dot(jax.nn.sigmoid(gd), g2_ref[...], precision=HIGHEST, preferred_element_type=F32)
    ka = ka_ref[...]
    r_o[...] = r
    kk_o[...] = kk
    v_o[...] = vr
    ktsum = None
    for d in range(2):
        a_d = a[:, d * w_:(d + 1) * w_]
        kt = kr * (1.0 + (a_d - 1.0) * ka)
        w_o[d] = decay[:, d * w_:(d + 1) * w_]
        kka_o[d] = kk * a_d
        kt_o[d] = kt
        ktsum = kt if ktsum is None else ktsum + kt
    bonus_o[...] = jnp.dot(r * ktsum * rk_ref[...], ones, precision=HIGHEST, preferred_element_type=F32) * vr


def _rwkv_features(rw, params, tpb, nct):
    n, wid = rw.shape
    ntile = n // TM
    rb = TM // 8
    nrb = n // 8
    w_ = RWKV_W
    tok = pl.BlockSpec((TM, w_), lambda t: (t, 0))
    tok2 = pl.BlockSpec((2, TM, w_), lambda t: (0, t, 0))

    def full(a):
        return pl.BlockSpec(a.shape, lambda t: (0,) * a.ndim)

    one = jax.ShapeDtypeStruct((n, w_), F32)
    two = jax.ShapeDtypeStruct((2, n, w_), F32)
    return pl.pallas_call(
        functools.partial(_feat_kernel, tpb=tpb, nct=nct),
        grid=(ntile,),
        in_specs=[pl.BlockSpec((TM, wid), lambda t: (t, 0)),
                  pl.BlockSpec((8, wid), lambda t: (jnp.maximum(t * rb - 1, 0), 0)),
                  pl.BlockSpec((8, wid), lambda t: (jnp.minimum((t + 1) * rb, nrb - 1), 0))]
                 + [full(a) for a in params],
        out_specs=[tok, tok, tok, tok2, tok2, tok2, tok, tok],
        out_shape=[one, one, one, two, two, two, one, one],
        compiler_params=_cparams(("arbitrary",)),
        name="rwkv_features",
    )(rw, rw, rw, *params)


def _scan_kernel(kkf, kkb, wf, wb, kaf, kab, ktf, ktb, rf, rb, vf, vb, yf_ref, yb_ref, s_ref, t_ref, y_buf):
    @pl.when(pl.program_id(0) == 0)
    def _():
        s_ref[...] = jnp.zeros_like(s_ref)
        y_buf[...] = jnp.zeros_like(y_buf)

    nkey = s_ref.shape[0]
    nb, tc, nh, _ = kkf.shape
    half = nb * nh
    lanes = 2 * half
    kblk = 8
    pairs = ((kkf, kkb), (wf, wb), (kaf, kab), (ktf, ktb), (rf, rb), (vf, vb))

    def relayout(i, dst):
        for q, (f_ref, b_ref) in enumerate(pairs):
            xf = f_ref[:, i].reshape(half, nkey)
            xb = b_ref[:, tc - 1 - i].reshape(half, nkey)
            dst[q] = jnp.concatenate([xf, xb], axis=0).T

    def emit(i):
        yt = y_buf[...].reshape(nkey, lanes).T
        yf_ref[:, i] = yt[:half].reshape(nb, nh, nkey)
        yb_ref[:, tc - 1 - i] = yt[half:].reshape(nb, nh, nkey)

    def step(i, cur, nxt):
        def row(q, k):
            return cur[q, pl.ds(k, 1), :][None]

        sa = s_ref[0] * row(0, 0)
        for k in range(1, nkey):
            sa = sa + s_ref[k] * row(0, k)
        relayout(jnp.minimum(i + 1, tc - 1), nxt)
        emit(jnp.maximum(i - 1, 0))
        v = cur[5].reshape(nkey // 8, 8, lanes)

        def upd_body(kb, y):
            base = pl.multiple_of(kb * kblk, kblk)
            for j in range(kblk):
                k = base + j
                s_new = s_ref[k] * row(1, k) + (v * row(3, k) - sa * row(2, k))
                s_ref[k] = s_new
                y = y + s_new * row(4, k)
            return y

        y_buf[...] = lax.fori_loop(0, nkey // kblk, upd_body, jnp.zeros((nkey // 8, 8, lanes), F32))

    relayout(0, t_ref.at[0])

    def two_steps(j, carry):
        step(2 * j, t_ref.at[0], t_ref.at[1])
        step(2 * j + 1, t_ref.at[1], t_ref.at[0])
        return carry

    lax.fori_loop(0, tc // 2, two_steps, 0)
    emit(tc - 1)


def _rwkv_scan(kk, w, kka, kt, r, v, lc):
    nb, s, nh, nkey = kk.shape
    tc = SCAN_TC
    nctb = lc // tc
    ntb = s // tc

    def mirror(g):
        return jnp.where(g < nctb, nctb - 1 - g, nctb + ntb - 1 - g)

    blk = (nb, tc, nh, nkey)
    sf = pl.BlockSpec(blk, lambda g: (0, g, 0, 0))
    sb = pl.BlockSpec(blk, lambda g: (0, mirror(g), 0, 0))
    df = pl.BlockSpec((None,) + blk, lambda g: (0, 0, g, 0, 0))
    db = pl.BlockSpec((None,) + blk, lambda g: (1, 0, mirror(g), 0, 0))
    out = jax.ShapeDtypeStruct((nb, s, nh, nkey), F32)
    lanes = 2 * nb * nh
    return pl.pallas_call(
        _scan_kernel,
        grid=(ntb,),
        in_specs=[sf, sb, df, db, df, db, df, db, sf, sb, sf, sb],
        out_specs=[sf, sb],
        out_shape=[out, out],
        scratch_shapes=[pltpu.VMEM((nkey, nkey // 8, 8, lanes), F32),
                        pltpu.VMEM((2, 6, nkey, lanes), F32),
                        pltpu.VMEM((nkey // 8, 8, lanes), F32)],
        compiler_params=_cparams(("arbitrary",)),
        name="rwkv_scan",
    )(kk, kk, w, w, kka, kka, kt, kt, r, r, v, v)


def _even_out_kernel(o_ref, g_ref, yf_ref, yb_ref, bonus_ref, gate_ref, lnw_ref, lnb_ref, ones_ref,
                     w_ref, h_ref, mod_ref, out_ref):
    parts = []
    for hh in range(RET_HEADS):
        o = o_ref[:, hh * RET_DV:(hh + 1) * RET_DV]
        o = o * lax.rsqrt(jnp.mean(o * o, axis=-1, keepdims=True) + EPS)
        gg = g_ref[:, hh * RET_DV:(hh + 1) * RET_DV]
        parts.append(o * (gg * jax.nn.sigmoid(gg)))
    ones = ones_ref[...]
    y = yf_ref[...] + yb_ref[...]
    mu = jnp.dot(y, ones, precision=HIGHEST, preferred_element_type=F32) * (1.0 / RWKV_N)
    yc = y - mu
    var = jnp.dot(yc * yc, ones, precision=HIGHEST, preferred_element_type=F32) * (1.0 / RWKV_N)
    yn = yc * lax.rsqrt(var + GN_EPS) * lnw_ref[...] + lnb_ref[...]
    parts.append((yn + bonus_ref[...]) * gate_ref[...])
    cat = jnp.concatenate(parts, axis=-1).astype(BF16)
    mix = jnp.dot(cat, w_ref[...], preferred_element_type=F32)
    out_ref[...] = h_ref[...] + mod_ref[2:3, :] * mix


def _even_out(o_ret, g, y2, bonus, gate, lnw, lnb, ones, w_out, h, mod, modrow, tpb):
    n, d = h.shape
    w_ = RWKV_W
    tok = pl.BlockSpec((TM, w_), lambda t: (t, 0))

    def full(a):
        return pl.BlockSpec(a.shape, lambda t: (0,) * a.ndim)

    return pl.pallas_call(
        _even_out_kernel,
        grid=(n // TM,),
        in_specs=[tok, tok, tok, tok, tok, tok, full(lnw), full(lnb), full(ones), full(w_out),
                  pl.BlockSpec((TM, d), lambda t: (t, 0)),
                  pl.BlockSpec((None, N_MOD, d), lambda t: (modrow(t), 0, 0))],
        out_specs=pl.BlockSpec((TM, d), lambda t: (t, 0)),
        out_shape=jax.ShapeDtypeStruct((n, d), F32),
        compiler_params=_cparams(("arbitrary",)),
        name="even_out",
    )(o_ret, g, y2[0], y2[1], bonus, gate, lnw, lnb, ones, w_out, h, mod)


def _mla_proj_kernel(a_ref, qg_ref, kvg_ref, wuq_ref, wukv_ref, cos_ref, sin_ref,
                     qn_o, qp_o, kn_o, v_o, kp_o):
    a = a_ref[...]
    cos = cos_ref[...]
    sin = sin_ref[...]

    def rms(t, g):
        return (t * lax.rsqrt(jnp.mean(t * t, axis=-1, keepdims=True) + EPS) * g).astype(BF16)

    cq = rms(a[:, :Q_RANK], qg_ref[...])
    ckv = rms(a[:, Q_RANK:Q_RANK + KV_RANK], kvg_ref[...])
    pe0 = Q_RANK + KV_RANK
    kp_o[...] = (a[:, pe0:pe0 + QK_ROPE] * cos + a[:, pe0 + QK_ROPE:pe0 + 2 * QK_ROPE] * sin).astype(BF16)
    hw = QK_NOPE + 2 * QK_ROPE
    for hh in range(MLA_HEADS):
        qh = jnp.dot(cq, wuq_ref[:, hh * hw:(hh + 1) * hw], preferred_element_type=F32)
        qn_o[hh] = (qh[:, :QK_NOPE] * MLA_SCALE).astype(BF16)
        qp_o[hh] = ((qh[:, QK_NOPE:QK_NOPE + QK_ROPE] * cos + qh[:, QK_NOPE + QK_ROPE:] * sin)
                    * MLA_SCALE).astype(BF16)
        kvh = jnp.dot(ckv, wukv_ref[:, hh * hw:(hh + 1) * hw], preferred_element_type=F32)
        kn_o[hh] = kvh[:, :QK_NOPE].astype(BF16)
        v_o[hh] = kvh[:, QK_NOPE:].astype(BF16)


def _mla_proj(a, qg, kvg, wuq, wukv, cos, sin, tpb, nct):
    n, wid = a.shape
    hd = MLA_HEADS
    nlt = tpb - nct
    nlat = (n // tpb) * nlt

    def full(x):
        return pl.BlockSpec(x.shape, lambda t: (0,) * x.ndim)

    def lat_blk(t):
        return (t // tpb) * nlt + jnp.maximum(t % tpb - nct, 0)

    h128 = pl.BlockSpec((hd, TM, 128), lambda t: (0, t, 0))
    rope = pl.BlockSpec((TM, QK_ROPE), lambda t: (t % tpb, 0))
    return pl.pallas_call(
        _mla_proj_kernel,
        grid=(n // TM,),
        in_specs=[pl.BlockSpec((TM, wid), lambda t: (t, 0)), full(qg), full(kvg), full(wuq), full(wukv),
                  rope, rope],
        out_specs=[pl.BlockSpec((hd, TM, 128), lambda t: (0, lat_blk(t), 0)),
                   pl.BlockSpec((hd, TM, QK_ROPE), lambda t: (0, lat_blk(t), 0)), h128, h128,
                   pl.BlockSpec((TM, QK_ROPE), lambda t: (t, 0))],
        out_shape=[jax.ShapeDtypeStruct((hd, nlat * TM, 128), BF16),
                   jax.ShapeDtypeStruct((hd, nlat * TM, QK_ROPE), BF16),
                   jax.ShapeDtypeStruct((hd, n, 128), BF16),
                   jax.ShapeDtypeStruct((hd, n, 128), BF16),
                   jax.ShapeDtypeStruct((n, QK_ROPE), BF16)],
        compiler_params=_cparams(("arbitrary",)),
        name="mla_proj",
    )(a, qg, kvg, wuq, wukv, cos, sin)


def _attn_kernel(qn_ref, qp_ref, kn_ref, kp_ref, v_ref, o_ref):
    dn = (((1,), (1,)), ((), ()))
    s = (lax.dot_general(qn_ref[...], kn_ref[...], dn, preferred_element_type=F32)
         + lax.dot_general(qp_ref[...], kp_ref[...], dn, preferred_element_type=F32))
    m = jnp.max(s, axis=-1, keepdims=True)
    p = jnp.exp(s - m)
    l = jnp.sum(p, axis=-1, keepdims=True)
    o = jnp.dot(p.astype(BF16), v_ref[...], preferred_element_type=F32)
    o_ref[...] = (o / l).astype(o_ref.dtype)


def _attention(qn, qp, kn, kp, v, b, lc, l):
    hd = MLA_HEADS
    s = lc + l
    tq = TQ_ATTN if l % TQ_ATTN == 0 else TM
    nq = l // tq
    return pl.pallas_call(
        _attn_kernel,
        grid=(b, hd, nq),
        in_specs=[pl.BlockSpec((None, tq, 128), lambda bi, hi, qi: (hi, bi * nq + qi, 0)),
                  pl.BlockSpec((None, tq, QK_ROPE), lambda bi, hi, qi: (hi, bi * nq + qi, 0)),
                  pl.BlockSpec((None, s, 128), lambda bi, hi, qi: (hi, bi, 0)),
                  pl.BlockSpec((s, QK_ROPE), lambda bi, hi, qi: (bi, 0)),
                  pl.BlockSpec((None, s, 128), lambda bi, hi, qi: (hi, bi, 0))],
        out_specs=pl.BlockSpec((tq, V_HEAD), lambda bi, hi, qi: (bi * nq + qi, hi)),
        out_shape=jax.ShapeDtypeStruct((b * l, hd * V_HEAD), BF16),
        compiler_params=_cparams(("arbitrary", "arbitrary", "arbitrary")),
        name="mla_attention",
    )(qn, qp, kn, kp, v)


def _oproj_kernel(o_ref, w_ref, h_ref, mod_ref, out_ref):
    mix = jnp.dot(o_ref[...], w_ref[...], preferred_element_type=F32)
    out_ref[...] = h_ref[...] + mod_ref[2:3, :] * mix


def _oproj(o, w_o, h, mod, hrow, modrow):
    n, d = o.shape[0], h.shape[1]
    return pl.pallas_call(
        _oproj_kernel,
        grid=(n // TM,),
        in_specs=[pl.BlockSpec((TM, o.shape[1]), lambda t: (t, 0)),
                  pl.BlockSpec(w_o.shape, lambda t: (0, 0)),
                  pl.BlockSpec((TM, d), lambda t: (hrow(t), 0)),
                  pl.BlockSpec((None, N_MOD, d), lambda t: (modrow(t), 0, 0))],
        out_specs=pl.BlockSpec((TM, d), lambda t: (t, 0)),
        out_shape=jax.ShapeDtypeStruct((n, d), F32),
        compiler_params=_cparams(("arbitrary",)),
        name="mla_oproj",
    )(o, w_o, h, mod)


def _route_kernel(h_ref, g_ref, mod_ref, wr_ref, br_ref, xl_ref, route_ref):
    xl = _norm_mod(h_ref[...], g_ref[...], mod_ref[...], 3, 4)
    xl_ref[...] = xl
    logits = jnp.dot(xl, wr_ref[...], precision=HIGHEST, preferred_element_type=F32) + br_ref[...]
    lane_i = lax.broadcasted_iota(jnp.int32, logits.shape, 1)
    lane = lane_i.astype(F32)
    lane_grp = (lane_i >> 3).astype(F32)
    neg = -jnp.inf
    big = 1e6
    gl = jnp.where(jnp.logical_and(lane_i >= N_EXPERTS, lane_i < N_EXPERTS + N_GROUPS), logits, neg)
    gmax = jnp.max(gl, axis=-1, keepdims=True)
    gsum = jnp.sum(jnp.exp(gl - gmax), axis=-1, keepdims=True)
    pg = 1.0 / gsum
    gidx = jnp.min(jnp.where(gl == gmax, lane - N_EXPERTS, big), axis=-1, keepdims=True)
    in_grp = jnp.logical_and(lane_i < N_EXPERTS, lane_grp == gidx)
    el = jnp.where(in_grp, logits, neg)
    emax = jnp.max(el, axis=-1, keepdims=True)
    esum = jnp.sum(jnp.exp(el - emax), axis=-1, keepdims=True)
    i1 = jnp.min(jnp.where(el == emax, lane, big), axis=-1, keepdims=True)
    el2 = jnp.where(lane == i1, neg, el)
    emax2 = jnp.max(el2, axis=-1, keepdims=True)
    i2 = jnp.min(jnp.where(el2 == emax2, lane, big), axis=-1, keepdims=True)
    pe1 = 1.0 / esum
    pe2 = jnp.exp(emax2 - emax) / esum
    den = pe1 + pe2
    w1 = pg * pe1 / den
    w2 = pg * pe2 / den
    route_ref[...] = jnp.where(lane_i == 0, i1,
                               jnp.where(lane_i == 1, i2,
                                         jnp.where(lane_i == 2, w1, jnp.where(lane_i == 3, w2, 0.0))))


def _route(h, g, mod, modrow, wr, br):
    n, d = h.shape
    return pl.pallas_call(
        _route_kernel,
        grid=(n // TM,),
        in_specs=[pl.BlockSpec((TM, d), lambda t: (t, 0)),
                  pl.BlockSpec((1, d), lambda t: (0, 0)),
                  pl.BlockSpec((None, N_MOD, d), lambda t: (modrow(t), 0, 0)),
                  pl.BlockSpec(wr.shape, lambda t: (0, 0)),
                  pl.BlockSpec(br.shape, lambda t: (0, 0))],
        out_specs=[pl.BlockSpec((TM, d), lambda t: (t, 0)),
                   pl.BlockSpec((TM, 128), lambda t: (t, 0))],
        out_shape=[jax.ShapeDtypeStruct((n, d), F32), jax.ShapeDtypeStruct((n, 128), F32)],
        compiler_params=_cparams(("arbitrary",)),
        name="moe_route",
    )(h, g.reshape(1, d), mod, wr, br)


def _expert_kernel(te_ref, nt_ref, x_ref, wg_ref, wu_ref, wd_ref, o_ref, wg_s, wu_s, wd_s):
    t = pl.program_id(0)

    @pl.when(t < nt_ref[0])
    def _():
        changed = jnp.logical_or(t == 0, te_ref[t] != te_ref[jnp.maximum(t - 1, 0)])

        @pl.when(changed)
        def _():
            wg_s[...] = wg_ref[...].astype(BF16)
            wu_s[...] = wu_ref[...].astype(BF16)
            wd_s[...] = wd_ref[...].astype(BF16)

        x = x_ref[...].astype(BF16)
        h1 = jnp.dot(x, wg_s[...], preferred_element_type=F32)
        h2 = jnp.dot(x, wu_s[...], preferred_element_type=F32)
        hid = ((h1 * jax.nn.sigmoid(h1)) * h2).astype(BF16)
        o_ref[...] = jnp.dot(hid, wd_s[...], preferred_element_type=F32)

    @pl.when(t >= nt_ref[0])
    def _():
        o_ref[...] = jnp.zeros_like(o_ref)


def _experts(tile_e, ntiles, xs, w_gate, w_up, w_down, layer):
    rows, d = xs.shape
    hid = w_gate.shape[-1]
    grid_spec = pltpu.PrefetchScalarGridSpec(
        num_scalar_prefetch=2,
        grid=(rows // TMOE,),
        in_specs=[pl.BlockSpec((TMOE, d), lambda t, te, nt: (t, 0)),
                  pl.BlockSpec((None, None, d, hid), lambda t, te, nt: (layer, te[t], 0, 0)),
                  pl.BlockSpec((None, None, d, hid), lambda t, te, nt: (layer, te[t], 0, 0)),
                  pl.BlockSpec((None, None, hid, d), lambda t, te, nt: (layer, te[t], 0, 0))],
        out_specs=pl.BlockSpec((TMOE, d), lambda t, te, nt: (t, 0)),
        scratch_shapes=[pltpu.VMEM((d, hid), BF16), pltpu.VMEM((d, hid), BF16), pltpu.VMEM((hid, d), BF16)],
    )
    return pl.pallas_call(
        _expert_kernel,
        grid_spec=grid_spec,
        out_shape=jax.ShapeDtypeStruct((rows, d), F32),
        compiler_params=_cparams(("arbitrary",)),
        name="moe_experts",
    )(tile_e, ntiles, xs, w_gate, w_up, w_down)


def _combine_kernel(y1_ref, y2_ref, route_ref, h_ref, mod_ref, fg_ref, out_ref, *, final):
    route = route_ref[...]
    y = route[:, 2:3] * y1_ref[...] + route[:, 3:4] * y2_ref[...]
    hn = h_ref[...] + mod_ref[5:6, :] * y
    if final:
        hn = hn * lax.rsqrt(jnp.mean(hn * hn, axis=-1, keepdims=True) + EPS) * fg_ref[...]
    out_ref[...] = hn


def _combine(y1, y2, route, h, mod, modrow, fg, final):
    n, d = h.shape
    tok = pl.BlockSpec((TM, d), lambda t: (t, 0))
    return pl.pallas_call(
        functools.partial(_combine_kernel, final=final),
        grid=(n // TM,),
        in_specs=[tok, tok, pl.BlockSpec((TM, 128), lambda t: (t, 0)), tok,
                  pl.BlockSpec((None, N_MOD, d), lambda t: (modrow(t), 0, 0)),
                  pl.BlockSpec((1, d), lambda t: (0, 0))],
        out_specs=tok,
        out_shape=jax.ShapeDtypeStruct((n, d), F32),
        compiler_params=_cparams(("arbitrary",)),
        name="moe_combine",
    )(y1, y2, route, h, mod, fg.reshape(1, d))


def _moe(h, g, mod, modrow, wr, br, w_gate, w_up, w_down, layer, fg, final):
    n, d = h.shape
    xl, route = _route(h, g, mod, modrow, wr, br)
    e = route[:, :2].astype(jnp.int32).reshape(-1)
    onehot = (e[:, None] == jnp.arange(N_EXPERTS, dtype=jnp.int32)[None, :]).astype(jnp.int32)
    csum = jnp.cumsum(onehot, axis=0)
    counts = csum[-1]
    rank = jnp.sum(csum * onehot, axis=1) - 1
    padded = ((counts + TMOE - 1) // TMOE) * TMOE
    pend = jnp.cumsum(padded)
    pos = (pend - padded)[e] + rank
    rows = 2 * n + N_EXPERTS * TMOE
    src = (jnp.arange(rows, dtype=jnp.int32) % n).at[pos].set(jnp.arange(2 * n, dtype=jnp.int32) // 2,
                                                     mode="promise_in_bounds", unique_indices=True)
    ntile = rows // TMOE
    nvalid = (pend[-1] // TMOE).astype(jnp.int32)
    tstart = jnp.arange(ntile, dtype=jnp.int32) * TMOE
    tile_e = jnp.sum((tstart[:, None] >= pend[None, :]).astype(jnp.int32), axis=1)
    last_e = jnp.sum((((nvalid - 1) * TMOE) >= pend).astype(jnp.int32))
    tile_e = jnp.where(tstart < pend[-1], tile_e, last_e).astype(jnp.int32)
    xs = xl.at[src].get(mode="promise_in_bounds")
    ys = _experts(tile_e, nvalid.reshape(1), xs, w_gate, w_up, w_down, layer)
    pos2 = pos.reshape(n, 2)
    y1 = ys.at[pos2[:, 0]].get(mode="promise_in_bounds", unique_indices=True)
    y2 = ys.at[pos2[:, 1]].get(mode="promise_in_bounds", unique_indices=True)
    return _combine(y1, y2, route, h, mod, modrow, fg, final)


def _block_ones(width, seg):
    idx = np.arange(width) // seg
    return jnp.asarray((idx[:, None] == idx[None, :]).astype(np.float32))


def _rope_tables(lc, l):
    rows = l // GRID_W
    row = np.repeat(np.arange(rows, dtype=np.float32), GRID_W)
    col = np.tile(np.arange(GRID_W, dtype=np.float32), rows)
    n_freq = QK_ROPE // 4
    inv_freq = jnp.asarray(ROPE_BASE, F32) ** (-jnp.arange(n_freq, dtype=F32) / n_freq)
    ang_r = jnp.asarray(row)[:, None] * inv_freq
    ang_c = jnp.asarray(col)[:, None] * inv_freq
    cos = jnp.concatenate([jnp.cos(ang_r), jnp.cos(ang_r), jnp.cos(ang_c), jnp.cos(ang_c)], axis=-1)
    sin = jnp.concatenate([jnp.sin(ang_r), jnp.sin(ang_r), jnp.sin(ang_c), jnp.sin(ang_c)], axis=-1)
    cos = jnp.concatenate([jnp.ones((lc, QK_ROPE), F32), cos], axis=0)
    sin = jnp.concatenate([jnp.zeros((lc, QK_ROPE), F32), sin], axis=0)
    return cos, sin


def _rot_cols(pe):
    q = QK_ROPE // 4
    return jnp.concatenate([-pe[..., q:2 * q], pe[..., 0:q], -pe[..., 3 * q:4 * q], pe[..., 2 * q:3 * q]], axis=-1)


def _pad_cols(w, n):
    return jnp.pad(w, ((0, 0), (0, n - w.shape[1])))


def kernel(x, c, ctx, c_ctx, ada_w, ada_b, norm1_g, norm2_g, final_g, ev_w_in, ev_shift_mu, rwkv_w0, rwkv_w2, rwkv_a0, rwkv_a2, rwkv_g2, rwkv_k_k, rwkv_k_a, rwkv_r_k, rwkv_lnx_w, rwkv_lnx_b, ev_w_out, mla_w_in, mla_q_norm_g, mla_w_uq, mla_kv_norm_g, mla_w_ukv, mla_w_o, moe_w_grp, moe_b_grp, moe_w_exp, moe_b_exp, moe_w_gate, moe_w_up, moe_w_down):
    b, l, d = x.shape
    lc = ctx.shape[1]
    s = lc + l
    n = b * s
    tpb = s // TM
    nct = lc // TM
    nlt = l // TM
    assert lc % TM == 0 and l % TM == 0 and b + 1 <= MOD_ROWS and ada_w.shape[0] == 2

    def modrow_all(t):
        return jnp.where(t % tpb < nct, b, t // tpb)

    def modrow_lat(t):
        return t // nlt

    def lat_tile(t):
        return (t // nlt) * tpb + nct + t % nlt

    c_all = jnp.concatenate([c, c_ctx[None], jnp.zeros((MOD_ROWS - b - 1, d), F32)], axis=0)
    mod = _ada_table(c_all, ada_w, ada_b)
    h = jnp.concatenate([ctx, x], axis=1).reshape(n, d)

    def router_weights(layer):
        wr = jnp.concatenate([moe_w_exp[layer], moe_w_grp[layer]], axis=1)
        br = jnp.concatenate([moe_b_exp[layer], moe_b_grp[layer]])[None]
        return _pad_cols(wr, 128), _pad_cols(br, 128)

    w_ = RWKV_W
    ret_w = 2 * RET_HEADS * RET_DK + RET_HEADS * RET_DV
    w_in = _pad_cols(ev_w_in[0], 3584).astype(BF16)
    qkv, gret, rw = _norm_proj(h, norm1_g[0], mod[0], modrow_all, w_in,
                               (ret_w, RET_HEADS * RET_DV, 2048), (BF16, F32, F32))

    lg = jnp.log1p(-jnp.exp2(-5.0 - jnp.arange(RET_HEADS, dtype=F32)))
    o_ret = _retention(qkv, jnp.broadcast_to(lg[:, None, None], (RET_HEADS, 1, 128)), b, lc, l)

    ones8 = _block_ones(w_, RWKV_N)
    zero = jnp.zeros((64, w_), F32)
    w2bd = jnp.concatenate([jnp.concatenate([rwkv_w2[0, 0], zero], axis=1),
                            jnp.concatenate([zero, rwkv_w2[0, 1]], axis=1)], axis=0)
    a2bd = jnp.concatenate([jnp.concatenate([rwkv_a2[0, 0], zero], axis=1),
                            jnp.concatenate([zero, rwkv_a2[0, 1]], axis=1)], axis=0)
    g2p = jnp.pad(rwkv_g2[0], ((0, 256 - rwkv_g2.shape[1]), (0, 0)))
    mu = _pad_cols(ev_shift_mu[0], 2048)
    feat_params = (mu, rwkv_w0[0].reshape(1, 2 * w_), w2bd, rwkv_a0[0].reshape(1, 2 * w_), a2bd, g2p,
                   rwkv_k_k[0][None], rwkv_k_a[0][None], rwkv_r_k[0].reshape(1, w_), ones8)
    r_t, kk_t, v_t, w_t, kka_t, kt_t, gate, bonus = _rwkv_features(rw, feat_params, tpb, nct)

    def bshn(t):
        return t.reshape(t.shape[:-2] + (b, s, RWKV_HEADS, RWKV_N))

    y_f, y_b = _rwkv_scan(bshn(kk_t), bshn(w_t), bshn(kka_t), bshn(kt_t), bshn(r_t), bshn(v_t), lc)
    y2 = (y_f.reshape(n, w_), y_b.reshape(n, w_))

    h = _even_out(o_ret, gret, y2, bonus, gate, rwkv_lnx_w[0][None], rwkv_lnx_b[0][None], ones8,
                  ev_w_out[0].astype(BF16), h, mod[0], modrow_all, tpb)
    wr, br = router_weights(0)
    h = _moe(h, norm2_g[0], mod[0], modrow_all, wr, br, moe_w_gate, moe_w_up, moe_w_down, 0, final_g, False)

    w_in1 = jnp.concatenate([mla_w_in[0], _rot_cols(mla_w_in[0][:, Q_RANK + KV_RANK:])], axis=1).astype(BF16)
    (a1,) = _norm_proj(h, norm1_g[1], mod[1], modrow_all, w_in1, (w_in1.shape[1],), (F32,))
    wq = mla_w_uq[0].reshape(Q_RANK, MLA_HEADS, QK_NOPE + QK_ROPE)
    wq = jnp.concatenate([wq, _rot_cols(wq[..., QK_NOPE:])], axis=-1).reshape(Q_RANK, -1).astype(BF16)
    cos, sin = _rope_tables(lc, l)
    qn, qp, kn, vv, kp = _mla_proj(a1, mla_q_norm_g[0][None], mla_kv_norm_g[0][None], wq,
                                   mla_w_ukv[0].astype(BF16), cos, sin, tpb, nct)
    o = _attention(qn, qp, kn, kp, vv, b, lc, l)
    h = _oproj(o, mla_w_o[0].astype(BF16), h, mod[1], lat_tile, modrow_lat)
    wr, br = router_weights(1)
    h = _moe(h, norm2_g[1], mod[1], modrow_lat, wr, br, moe_w_gate, moe_w_up, moe_w_down, 1, final_g, True)
    return h.reshape(b, l, d)
```

```python
import functools

import jax
import jax.numpy as jnp
import numpy as np
from jax import lax
from jax.experimental import pallas as pl
from jax.experimental.pallas import tpu as pltpu

F32 = jnp.float32
BF16 = jnp.bfloat16
HIGHEST = lax.Precision.HIGHEST

NB = 8
TM = 256
RET_TQ = 256
TQ_ATTN = 256
TMOE = 512
EPS = 1e-6
GN_EPS = 64e-5
GRID_W = 64
ROPE_BASE = 10000.0

RET_HEADS, RET_DK, RET_DV = 4, 64, 128
RWKV_HEADS, RWKV_N = 8, 64
RWKV_W = RWKV_HEADS * RWKV_N
MLA_HEADS, Q_RANK, KV_RANK, QK_NOPE, QK_ROPE, V_HEAD = 8, 384, 256, 128, 64, 128
MLA_SCALE = (QK_NOPE + QK_ROPE) ** -0.5
N_GROUPS, EXPERTS_PER_GROUP = 4, 8
N_EXPERTS = N_GROUPS * EXPERTS_PER_GROUP
N_MOD = 6
MOD_ROWS = 16


def _cparams(sem):
    return pltpu.CompilerParams(dimension_semantics=sem)


def _rows8(x, fn):
    tm, d = x.shape
    return fn(x.reshape(tm // NB, NB, d)).reshape(tm, d)


def _norm_mod(x, g, shift8, scale8):
    var = jnp.mean(x * x, axis=-1, keepdims=True)
    y = x * lax.rsqrt(var + EPS) * g
    return _rows8(y, lambda y3: y3 * (1.0 + scale8[None]) + shift8[None])


def _gated_residual(h, gate8, y):
    return h + _rows8(y, lambda y3: y3 * gate8[None])


def _mod_spec(d, sel):
    return pl.BlockSpec((None, N_MOD, NB, d), lambda t: (sel(t), 0, 0, 0))


def _split_bf16(x):
    hi = x.astype(BF16)
    return hi, (x - hi.astype(F32)).astype(BF16)


def _seg_sum(x, ones):
    hi, lo = _split_bf16(x)
    return jnp.dot(hi, ones, preferred_element_type=F32) + jnp.dot(lo, ones, preferred_element_type=F32)


def _dot3(x, w_ref):
    hi, lo = _split_bf16(x)
    w_hi = w_ref[0]
    return (jnp.dot(hi, w_hi, preferred_element_type=F32) + jnp.dot(lo, w_hi, preferred_element_type=F32)
            + jnp.dot(hi, w_ref[1], preferred_element_type=F32))


def _ada_kernel(c_ref, w_ref, b_ref, o_ref):
    s = c_ref[...]
    s = s * jax.nn.sigmoid(s)
    o_ref[...] = jnp.dot(s, w_ref[...], precision=HIGHEST, preferred_element_type=F32) + b_ref[...]


def _ada_table(c_all, ada_w, ada_b):
    depth, d, nd = ada_w.shape
    out = pl.pallas_call(
        _ada_kernel,
        grid=(depth, nd // d),
        in_specs=[pl.BlockSpec((MOD_ROWS, d), lambda l, j: (0, 0)),
                  pl.BlockSpec((None, d, d), lambda l, j: (l, 0, j)),
                  pl.BlockSpec((None, 1, d), lambda l, j: (l, 0, j))],
        out_specs=pl.BlockSpec((None, MOD_ROWS, d), lambda l, j: (l, 0, j)),
        out_shape=jax.ShapeDtypeStruct((depth, MOD_ROWS, nd), F32),
        compiler_params=_cparams(("arbitrary", "arbitrary")),
        name="ada_table",
    )(c_all, ada_w, ada_b.reshape(depth, 1, nd))
    return out.reshape(depth, MOD_ROWS, N_MOD, d)


def _proj_kernel(x_ref, g_ref, mod_ref, w_ref, *o_refs, splits):
    xm = _norm_mod(x_ref[...], g_ref[...], mod_ref[0], mod_ref[1]).astype(BF16)
    off = 0
    for o_ref, n in zip(o_refs, splits):
        for j in range(0, n, 512):
            c = min(512, n - j)
            o_ref[:, j:j + c] = jnp.dot(xm, w_ref[:, off + j:off + j + c],
                                        preferred_element_type=F32).astype(o_ref.dtype)
        off += n


def _norm_proj(h, g, mod, sel, w_bf16, splits, dtypes):
    n, d = h.shape
    nout = w_bf16.shape[1]
    return pl.pallas_call(
        functools.partial(_proj_kernel, splits=splits),
        grid=(n // TM,),
        in_specs=[pl.BlockSpec((TM, d), lambda t: (t, 0)),
                  pl.BlockSpec((1, d), lambda t: (0, 0)),
                  _mod_spec(d, sel),
                  pl.BlockSpec((d, nout), lambda t: (0, 0))],
        out_specs=[pl.BlockSpec((TM, s), lambda t: (t, 0)) for s in splits],
        out_shape=[jax.ShapeDtypeStruct((n, s), dt) for s, dt in zip(splits, dtypes)],
        compiler_params=_cparams(("arbitrary",)),
        name="norm_proj",
    )(h, g.reshape(1, d), mod, w_bf16)


def _ret_kernel(lg_ref, qkv_ref, o_ref, g_ref, *, lc, l, tq):
    hp = pl.program_id(1)
    qi = pl.program_id(2)
    nct = lc // tq
    nk = (lc + l) // tq
    dn = (((1,), (1,)), ((), ()))
    scale = RET_DK ** -0.5
    qk_w = 2 * RET_DK
    v_w = 2 * RET_DV

    @pl.when(qi == 0)
    def _():
        rel = (lax.broadcasted_iota(jnp.int32, (tq, tq), 0)
               - lax.broadcasted_iota(jnp.int32, (tq, tq), 1)).astype(F32)
        for hh in range(2):
            lg = lg_ref[hh][0:1, 0:1]
            g_ref[hh, 0] = jnp.exp(lg * rel)
            g_ref[hh, 1] = jnp.exp(-(lg * rel))
            g_ref[hh, 2] = jnp.exp(lg * jnp.abs(rel))

    q_lat = qi >= nct
    q2 = qkv_ref[pl.ds(pl.multiple_of(qi * tq, tq), tq), pl.ds(pl.multiple_of(hp * qk_w, qk_w), qk_w)]
    k2 = qkv_ref[:, pl.ds(pl.multiple_of(RET_HEADS * RET_DK + hp * qk_w, qk_w), qk_w)]
    v2 = qkv_ref[:, pl.ds(pl.multiple_of(2 * RET_HEADS * RET_DK + hp * v_w, v_w), v_w)]
    for hh in range(2):
        lg = lg_ref[hh][0:1, 0:1]
        s = lax.dot_general(q2[:, hh * RET_DK:(hh + 1) * RET_DK], k2[:, hh * RET_DK:(hh + 1) * RET_DK], dn,
                            preferred_element_type=F32)
        pieces = []
        for kj in range(nk):
            d = qi - kj
            idx = jnp.where(d > 0, 0, jnp.where(d < 0, 1, 2))
            dabs = jnp.full((1, 1), jnp.abs(d) * tq, jnp.int32).astype(F32)
            sig = jnp.exp(lg * dabs) * scale
            if kj < nct:
                dback = jnp.full((1, 1), l + lc - d * tq, jnp.int32).astype(F32)
                sig2 = jnp.where(q_lat, jnp.exp(lg * dback) * scale, 0.0)
                m = g_ref[hh, idx] * sig + g_ref[hh, 1] * sig2
            else:
                m = g_ref[hh, idx] * jnp.where(q_lat, sig, 0.0)
            pieces.append((s[:, kj * tq:(kj + 1) * tq] * m).astype(BF16))
        p = jnp.concatenate(pieces, axis=1)
        o_ref[:, hh * RET_DV:(hh + 1) * RET_DV] = jnp.dot(p, v2[:, hh * RET_DV:(hh + 1) * RET_DV],
                                                          preferred_element_type=F32)


def _retention(qkv, lg, b, lc, l):
    s = lc + l
    wid = qkv.shape[1]
    tq = RET_TQ if lc % RET_TQ == 0 else 128
    v_w = 2 * RET_DV
    ow = RET_HEADS * RET_DV
    out = pl.pallas_call(
        functools.partial(_ret_kernel, lc=lc, l=l, tq=tq),
        grid=(b, RET_HEADS // 2, s // tq),
        in_specs=[pl.BlockSpec((2, 1, 128), lambda bi, hp, qi: (hp, 0, 0)),
                  pl.BlockSpec((s, wid), lambda bi, hp, qi: (0, bi))],
        out_specs=pl.BlockSpec((tq, v_w), lambda bi, hp, qi: (qi, bi * (ow // v_w) + hp)),
        out_shape=jax.ShapeDtypeStruct((s, b * ow), F32),
        scratch_shapes=[pltpu.VMEM((2, 3, tq, tq), F32)],
        compiler_params=_cparams(("arbitrary", "arbitrary", "arbitrary")),
        name="retention",
    )(lg, qkv.reshape(s, b * wid))
    return out.reshape(s * b, ow)


def _head_pairs(qa, qb):
    tm = qa.shape[0]
    lo = lax.broadcasted_iota(jnp.int32, (tm, 128), 1) < RWKV_N
    out = []
    for c in range(RWKV_W // 128):
        a = qa[:, c * 128:(c + 1) * 128]
        b = qb[:, c * 128:(c + 1) * 128]
        out.append(jnp.where(lo, a, pltpu.roll(b, RWKV_N, 1)))
        out.append(jnp.where(lo, pltpu.roll(a, RWKV_N, 1), b))
    return out


def _feat_kernel(rw_ref, prev_ref, next_ref, mu_ref, w0_ref, w2_ref, a0_ref, a2_ref, g2_ref,
                 kkw_ref, ka_ref, rk_ref, ones_ref,
                 p1_o, p2f_o, p2b_o, p3f_o, p3b_o, gate_o, bonus_o, *, nct, ntile):
    t = pl.program_id(0)
    first = jnp.logical_or(t == 0, t == nct)
    last = jnp.logical_or(t == nct - 1, t == ntile - 1)
    y = rw_ref[...]
    tm, wid = y.shape
    grp = lax.broadcasted_iota(jnp.int32, (tm // NB, 1, 1), 0)
    prow = jnp.where(first, 0.0, prev_ref[...])
    nrow = jnp.where(last, 0.0, next_ref[...])
    prev = jnp.where(grp == 0, prow[None], pltpu.roll(y, NB, 0).reshape(tm // NB, NB, wid)).reshape(tm, wid)
    nxt = jnp.where(grp == tm // NB - 1, nrow[None],
                    pltpu.roll(y, tm - NB, 0).reshape(tm // NB, NB, wid)).reshape(tm, wid)
    ys = y + mu_ref[0:1, :] * (prev - y) + mu_ref[1:2, :] * (nxt - y)

    w_ = RWKV_W
    r = ys[:, 0:w_]
    kr = ys[:, w_:2 * w_]
    vr = ys[:, 2 * w_:3 * w_]
    wd = ys[:, 3 * w_:3 * w_ + 128]
    ad = ys[:, 3 * w_ + 128:3 * w_ + 256]
    gd = ys[:, 3 * w_ + 256:3 * w_ + 512]
    ones = ones_ref[...]

    kk = kr * kkw_ref[...]
    ss = _seg_sum(kk * kk, ones)
    kk = kk / jnp.maximum(jnp.sqrt(ss), 1e-12)
    zw = w0_ref[...] + _dot3(jnp.tanh(wd), w2_ref)
    wlog = -(jnp.maximum(-zw, 0.0) + jnp.log1p(jnp.exp(-jnp.abs(zw)))) - 0.5
    decay = jnp.exp(-jnp.exp(wlog))
    a = jax.nn.sigmoid(a0_ref[...] + _dot3(ad, a2_ref))
    gate_o[...] = _dot3(jax.nn.sigmoid(gd), g2_ref)
    ka = ka_ref[...]

    def emit(o_ref, qa, qb):
        for hh, slab in enumerate(_head_pairs(qa, qb)):
            o_ref[hh] = slab

    emit(p1_o, kk, r)
    ktsum = None
    for d, (p2_o, p3_o) in enumerate(((p2f_o, p3f_o), (p2b_o, p3b_o))):
        a_d = a[:, d * w_:(d + 1) * w_]
        kt = kr * (1.0 + (a_d - 1.0) * ka)
        emit(p2_o, decay[:, d * w_:(d + 1) * w_], kk * a_d)
        emit(p3_o, kt, vr)
        ktsum = kt if ktsum is None else ktsum + kt
    bonus_o[...] = _seg_sum(r * ktsum * rk_ref[...], ones) * vr


def _rwkv_features(rw, params, nct):
    n, wid = rw.shape
    ntile = n // TM
    rb = TM // 8
    nrb = n // 8
    w_ = RWKV_W
    tok = pl.BlockSpec((TM, w_), lambda t: (t, 0))
    pair = pl.BlockSpec((RWKV_HEADS, TM, 128), lambda t: (0, t, 0))

    def full(a):
        return pl.BlockSpec(a.shape, lambda t: (0,) * a.ndim)

    one = jax.ShapeDtypeStruct((n, w_), F32)
    pshape = jax.ShapeDtypeStruct((RWKV_HEADS, n, 128), F32)
    return pl.pallas_call(
        functools.partial(_feat_kernel, nct=nct, ntile=ntile),
        grid=(ntile,),
        in_specs=[pl.BlockSpec((TM, wid), lambda t: (t, 0)),
                  pl.BlockSpec((8, wid), lambda t: (jnp.maximum(t * rb - 1, 0), 0)),
                  pl.BlockSpec((8, wid), lambda t: (jnp.minimum((t + 1) * rb, nrb - 1), 0))]
                 + [full(a) for a in params],
        out_specs=[pair] * 5 + [tok, tok],
        out_shape=[pshape] * 5 + [one, one],
        compiler_params=_cparams(("arbitrary",)),
        name="rwkv_features",
    )(rw, rw, rw, *params)


def _scan_kernel(p1f, p1b, p2f, p2b, p3f, p3b, yf_ref, yb_ref, s_ref, t_ref, y_buf):
    @pl.when(pl.program_id(0) == 0)
    def _():
        s_ref[...] = jnp.zeros_like(s_ref)
        y_buf[...] = jnp.zeros_like(y_buf)

    nkey = s_ref.shape[0]
    nh = p1f.shape[0]
    tc = p1f.shape[1] // NB
    half = NB * nh
    lanes = 2 * half
    kblk = 8
    pairs = ((p1f, p1b), (p2f, p2b), (p3f, p3b))
    kk_q, r_q, w_q, kka_q, kt_q, v_q = (0, 0), (0, nkey), (1, 0), (1, nkey), (2, 0), (2, nkey)

    def rows_of(i):
        return pl.ds(pl.multiple_of(i * NB, NB), NB)

    def relayout(i, dst):
        for p, (f_ref, b_ref) in enumerate(pairs):
            rows = ([f_ref[hh, rows_of(i), :] for hh in range(nh)]
                    + [b_ref[hh, rows_of(tc - 1 - i), :] for hh in range(nh)])
            dst[p] = jnp.concatenate(rows, axis=0).T

    def emit(i):
        y = y_buf[...].reshape(nkey, lanes)
        yt = jnp.concatenate([y, y], axis=0).T
        for hh in range(nh):
            yf_ref[hh, rows_of(i), :] = yt[hh * NB:(hh + 1) * NB]
            yb_ref[hh, rows_of(tc - 1 - i), :] = yt[half + hh * NB:half + (hh + 1) * NB]

    def step(i, cur, nxt):
        def row(q, k):
            return cur[q[0], pl.ds(q[1] + k, 1), :][None]

        sa = s_ref[0] * row(kk_q, 0)
        for k in range(1, nkey):
            sa = sa + s_ref[k] * row(kk_q, k)
        relayout(jnp.minimum(i + 1, tc - 1), nxt)
        emit(jnp.maximum(i - 1, 0))
        v = cur[v_q[0], v_q[1]:v_q[1] + nkey, :].reshape(nkey // 8, 8, lanes)

        def upd_body(kb, y):
            base = pl.multiple_of(kb * kblk, kblk)
            for j in range(kblk):
                k = base + j
                s_new = s_ref[k] * row(w_q, k) + (v * row(kt_q, k) - sa * row(kka_q, k))
                s_ref[k] = s_new
                y = y + s_new * row(r_q, k)
            return y

        y_buf[...] = lax.fori_loop(0, nkey // kblk, upd_body, jnp.zeros((nkey // 8, 8, lanes), F32))

    relayout(0, t_ref.at[0])

    def two_steps(j, carry):
        step(2 * j, t_ref.at[0], t_ref.at[1])
        step(2 * j + 1, t_ref.at[1], t_ref.at[0])
        return carry

    lax.fori_loop(0, tc // 2, two_steps, 0)
    emit(tc - 1)


def _rwkv_scan(p1, p2f, p2b, p3f, p3b, nct):
    nh, n, _ = p1.shape
    nkey = RWKV_N
    ntb = n // TM

    def mirror(g):
        return jnp.where(g < nct, nct - 1 - g, nct + ntb - 1 - g)

    sf = pl.BlockSpec((nh, TM, 128), lambda g: (0, g, 0))
    sb = pl.BlockSpec((nh, TM, 128), lambda g: (0, mirror(g), 0))
    out = jax.ShapeDtypeStruct((nh, n, 128), F32)
    lanes = 2 * NB * nh
    return pl.pallas_call(
        _scan_kernel,
        grid=(ntb,),
        in_specs=[sf, sb, sf, sb, sf, sb],
        out_specs=[sf, sb],
        out_shape=[out, out],
        scratch_shapes=[pltpu.VMEM((nkey, nkey // 8, 8, lanes), F32),
                        pltpu.VMEM((2, 3, 2 * nkey, lanes), F32),
                        pltpu.VMEM((nkey // 8, 8, lanes), F32)],
        compiler_params=_cparams(("arbitrary",)),
        name="rwkv_scan",
    )(p1, p1, p2f, p2b, p3f, p3b)


def _even_out_kernel(o_ref, g_ref, yf_ref, yb_ref, bonus_ref, gate_ref, lnw_ref, lnb_ref, ones_ref,
                     w_ref, h_ref, mod_ref, out_ref):
    parts = []
    for hh in range(RET_HEADS):
        o = o_ref[:, hh * RET_DV:(hh + 1) * RET_DV]
        o = o * lax.rsqrt(jnp.mean(o * o, axis=-1, keepdims=True) + EPS)
        gg = g_ref[:, hh * RET_DV:(hh + 1) * RET_DV]
        parts.append(o * (gg * jax.nn.sigmoid(gg)))
    ones = ones_ref[...]
    tm = h_ref.shape[0]
    lo = lax.broadcasted_iota(jnp.int32, (tm, 128), 1) < RWKV_N
    chunks = []
    for c in range(RWKV_HEADS // 2):
        even = yf_ref[2 * c] + yb_ref[2 * c]
        odd = yf_ref[2 * c + 1] + yb_ref[2 * c + 1]
        chunks.append(jnp.where(lo, even, pltpu.roll(odd, RWKV_N, 1)))
    y = jnp.concatenate(chunks, axis=-1)
    mu = _seg_sum(y, ones) * (1.0 / RWKV_N)
    yc = y - mu
    var = _seg_sum(yc * yc, ones) * (1.0 / RWKV_N)
    yn = yc * lax.rsqrt(var + GN_EPS) * lnw_ref[...] + lnb_ref[...]
    parts.append((yn + bonus_ref[...]) * gate_ref[...])
    cat = jnp.concatenate(parts, axis=-1).astype(BF16)
    mix = jnp.dot(cat, w_ref[...], preferred_element_type=F32)
    out_ref[...] = _gated_residual(h_ref[...], mod_ref[2], mix)


def _even_out(o_ret, g, y2, bonus, gate, lnw, lnb, ones, w_out, h, mod, sel):
    n, d = h.shape
    w_ = RWKV_W
    tok = pl.BlockSpec((TM, w_), lambda t: (t, 0))
    ysp = pl.BlockSpec((RWKV_HEADS, TM, 128), lambda t: (0, t, 0))

    def full(a):
        return pl.BlockSpec(a.shape, lambda t: (0,) * a.ndim)

    return pl.pallas_call(
        _even_out_kernel,
        grid=(n // TM,),
        in_specs=[tok, tok, ysp, ysp, tok, tok, full(lnw), full(lnb), full(ones), full(w_out),
                  pl.BlockSpec((TM, d), lambda t: (t, 0)),
                  _mod_spec(d, sel)],
        out_specs=pl.BlockSpec((TM, d), lambda t: (t, 0)),
        out_shape=jax.ShapeDtypeStruct((n, d), F32),
        compiler_params=_cparams(("arbitrary",)),
        name="even_out",
    )(o_ret, g, y2[0], y2[1], bonus, gate, lnw, lnb, ones, w_out, h, mod)


def _mla_proj_kernel(a_ref, qg_ref, kvg_ref, wuq_ref, wukv_ref, cos_ref, sin_ref,
                     qn_o, qp_o, kn_o, v_o, kp_o):
    a = a_ref[...]
    cos = cos_ref[...]
    sin = sin_ref[...]
    zpad = jnp.zeros((a.shape[0], 128 - QK_ROPE), F32)

    def rms(t, g):
        return (t * lax.rsqrt(jnp.mean(t * t, axis=-1, keepdims=True) + EPS) * g).astype(BF16)

    def pad128(pe):
        return jnp.concatenate([pe, zpad], axis=1).astype(BF16)

    cq = rms(a[:, :Q_RANK], qg_ref[...])
    ckv = rms(a[:, Q_RANK:Q_RANK + KV_RANK], kvg_ref[...])
    pe0 = Q_RANK + KV_RANK
    kp_o[...] = pad128(a[:, pe0:pe0 + QK_ROPE] * cos + a[:, pe0 + QK_ROPE:pe0 + 2 * QK_ROPE] * sin)
    hw = QK_NOPE + 2 * QK_ROPE
    for hh in range(MLA_HEADS):
        qh = jnp.dot(cq, wuq_ref[:, hh * hw:(hh + 1) * hw], preferred_element_type=F32)
        qn_o[:, hh * 128:(hh + 1) * 128] = (qh[:, :QK_NOPE] * MLA_SCALE).astype(BF16)
        qp_o[:, hh * 128:(hh + 1) * 128] = pad128(
            (qh[:, QK_NOPE:QK_NOPE + QK_ROPE] * cos + qh[:, QK_NOPE + QK_ROPE:] * sin) * MLA_SCALE)
        kvh = jnp.dot(ckv, wukv_ref[:, hh * hw:(hh + 1) * hw], preferred_element_type=F32)
        kn_o[:, hh * 128:(hh + 1) * 128] = kvh[:, :QK_NOPE].astype(BF16)
        v_o[:, hh * 128:(hh + 1) * 128] = kvh[:, QK_NOPE:].astype(BF16)


def _mla_proj(a, qg, kvg, wuq, wukv, cos, sin, nct):
    n, wid = a.shape
    hw = MLA_HEADS * 128
    nlat = n // TM - nct

    def full(x):
        return pl.BlockSpec(x.shape, lambda t: (0,) * x.ndim)

    def lat_blk(t):
        return jnp.maximum(t - nct, 0)

    allrows = pl.BlockSpec((TM, hw), lambda t: (t, 0))
    latrows = pl.BlockSpec((TM, hw), lambda t: (lat_blk(t), 0))
    rope = pl.BlockSpec((TM, QK_ROPE), lambda t: (t, 0))
    return pl.pallas_call(
        _mla_proj_kernel,
        grid=(n // TM,),
        in_specs=[pl.BlockSpec((TM, wid), lambda t: (t, 0)), full(qg), full(kvg), full(wuq), full(wukv),
                  rope, rope],
        out_specs=[latrows, latrows, allrows, allrows, pl.BlockSpec((TM, 128), lambda t: (t, 0))],
        out_shape=[jax.ShapeDtypeStruct((nlat * TM, hw), BF16),
                   jax.ShapeDtypeStruct((nlat * TM, hw), BF16),
                   jax.ShapeDtypeStruct((n, hw), BF16),
                   jax.ShapeDtypeStruct((n, hw), BF16),
                   jax.ShapeDtypeStruct((n, 128), BF16)],
        compiler_params=_cparams(("arbitrary",)),
        name="mla_proj",
    )(a, qg, kvg, wuq, wukv, cos, sin)


def _attn_kernel(qn_ref, qp_ref, kn_ref, kp_ref, v_ref, o_ref):
    hi = pl.program_id(1)
    dn = (((1,), (1,)), ((), ()))
    hsl = pl.ds(pl.multiple_of(hi * 128, 128), 128)
    s = (lax.dot_general(qn_ref[...], kn_ref[:, hsl], dn, preferred_element_type=F32)
         + lax.dot_general(qp_ref[...], kp_ref[...], dn, preferred_element_type=F32))
    m = jnp.max(s, axis=-1, keepdims=True)
    p = jnp.exp(s - m)
    l = jnp.sum(p, axis=-1, keepdims=True)
    o = jnp.dot(p.astype(BF16), v_ref[:, hsl], preferred_element_type=F32)
    o_ref[...] = (o / l).astype(o_ref.dtype)


def _attention(qn, qp, kn, kp, v, b, lc, l):
    hd = MLA_HEADS
    hw = hd * 128
    s = lc + l
    tq = TQ_ATTN if l % TQ_ATTN == 0 else 128
    qspec = pl.BlockSpec((tq, 128), lambda bi, hi, qi: (qi, bi * hd + hi))
    kvspec = pl.BlockSpec((s, hw), lambda bi, hi, qi: (0, bi))
    out = pl.pallas_call(
        _attn_kernel,
        grid=(b, hd, l // tq),
        in_specs=[qspec, qspec, kvspec,
                  pl.BlockSpec((s, 128), lambda bi, hi, qi: (0, bi)),
                  kvspec],
        out_specs=qspec,
        out_shape=jax.ShapeDtypeStruct((l, b * hw), BF16),
        compiler_params=_cparams(("arbitrary", "arbitrary", "arbitrary")),
        name="mla_attention",
    )(qn.reshape(l, b * hw), qp.reshape(l, b * hw), kn.reshape(s, b * hw), kp.reshape(s, b * 128),
      v.reshape(s, b * hw))
    return out.reshape(l * b, hw)


def _oproj_kernel(o_ref, w_ref, h_ref, mod_ref, out_ref):
    mix = jnp.dot(o_ref[...], w_ref[...], preferred_element_type=F32)
    out_ref[...] = _gated_residual(h_ref[...], mod_ref[2], mix)


def _oproj(o, w_o, h, mod, hrow, sel):
    n, d = o.shape[0], h.shape[1]
    return pl.pallas_call(
        _oproj_kernel,
        grid=(n // TM,),
        in_specs=[pl.BlockSpec((TM, o.shape[1]), lambda t: (t, 0)),
                  pl.BlockSpec(w_o.shape, lambda t: (0, 0)),
                  pl.BlockSpec((TM, d), lambda t: (hrow(t), 0)),
                  _mod_spec(d, sel)],
        out_specs=pl.BlockSpec((TM, d), lambda t: (t, 0)),
        out_shape=jax.ShapeDtypeStruct((n, d), F32),
        compiler_params=_cparams(("arbitrary",)),
        name="mla_oproj",
    )(o, w_o, h, mod)


def _route_kernel(h_ref, g_ref, mod_ref, wr_ref, br_ref, xl_ref, route_ref):
    xl = _norm_mod(h_ref[...], g_ref[...], mod_ref[3], mod_ref[4])
    xl_ref[...] = xl
    logits = jnp.dot(xl, wr_ref[...], precision=HIGHEST, preferred_element_type=F32) + br_ref[...]
    lane_i = lax.broadcasted_iota(jnp.int32, logits.shape, 1)
    lane = lane_i.astype(F32)
    lane_grp = (lane_i >> 3).astype(F32)
    neg = -jnp.inf
    big = 1e6
    gl = jnp.where(jnp.logical_and(lane_i >= N_EXPERTS, lane_i < N_EXPERTS + N_GROUPS), logits, neg)
    gmax = jnp.max(gl, axis=-1, keepdims=True)
    gsum = jnp.sum(jnp.exp(gl - gmax), axis=-1, keepdims=True)
    pg = 1.0 / gsum
    gidx = jnp.min(jnp.where(gl == gmax, lane - N_EXPERTS, big), axis=-1, keepdims=True)
    in_grp = jnp.logical_and(lane_i < N_EXPERTS, lane_grp == gidx)
    el = jnp.where(in_grp, logits, neg)
    emax = jnp.max(el, axis=-1, keepdims=True)
    esum = jnp.sum(jnp.exp(el - emax), axis=-1, keepdims=True)
    i1 = jnp.min(jnp.where(el == emax, lane, big), axis=-1, keepdims=True)
    el2 = jnp.where(lane == i1, neg, el)
    emax2 = jnp.max(el2, axis=-1, keepdims=True)
    i2 = jnp.min(jnp.where(el2 == emax2, lane, big), axis=-1, keepdims=True)
    pe1 = 1.0 / esum
    pe2 = jnp.exp(emax2 - emax) / esum
    den = pe1 + pe2
    w1 = pg * pe1 / den
    w2 = pg * pe2 / den
    route_ref[...] = jnp.where(lane_i == 0, i1,
                               jnp.where(lane_i == 1, i2,
                                         jnp.where(lane_i == 2, w1, jnp.where(lane_i == 3, w2, 0.0))))


def _route(h, g, mod, sel, wr, br):
    n, d = h.shape
    return pl.pallas_call(
        _route_kernel,
        grid=(n // TM,),
        in_specs=[pl.BlockSpec((TM, d), lambda t: (t, 0)),
                  pl.BlockSpec((1, d), lambda t: (0, 0)),
                  _mod_spec(d, sel),
                  pl.BlockSpec(wr.shape, lambda t: (0, 0)),
                  pl.BlockSpec(br.shape, lambda t: (0, 0))],
        out_specs=[pl.BlockSpec((TM, d), lambda t: (t, 0)),
                   pl.BlockSpec((TM, 128), lambda t: (t, 0))],
        out_shape=[jax.ShapeDtypeStruct((n, d), F32), jax.ShapeDtypeStruct((n, 128), F32)],
        compiler_params=_cparams(("arbitrary",)),
        name="moe_route",
    )(h, g.reshape(1, d), mod, wr, br)


def _expert_kernel(te_ref, nt_ref, x_ref, wg_ref, wu_ref, wd_ref, o_ref, wg_s, wu_s, wd_s):
    t = pl.program_id(0)

    @pl.when(t < nt_ref[0])
    def _():
        changed = jnp.logical_or(t == 0, te_ref[t] != te_ref[jnp.maximum(t - 1, 0)])

        @pl.when(changed)
        def _():
            wg_s[...] = wg_ref[...].astype(BF16)
            wu_s[...] = wu_ref[...].astype(BF16)
            wd_s[...] = wd_ref[...].astype(BF16)

        x = x_ref[...].astype(BF16)
        h1 = jnp.dot(x, wg_s[...], preferred_element_type=F32)
        h2 = jnp.dot(x, wu_s[...], preferred_element_type=F32)
        hid = ((h1 * jax.nn.sigmoid(h1)) * h2).astype(BF16)
        o_ref[...] = jnp.dot(hid, wd_s[...], preferred_element_type=F32)

    @pl.when(t >= nt_ref[0])
    def _():
        o_ref[...] = jnp.zeros_like(o_ref)


def _experts(tile_e, ntiles, xs, w_gate, w_up, w_down, layer):
    rows, d = xs.shape
    hid = w_gate.shape[-1]
    grid_spec = pltpu.PrefetchScalarGridSpec(
        num_scalar_prefetch=2,
        grid=(rows // TMOE,),
        in_specs=[pl.BlockSpec((TMOE, d), lambda t, te, nt: (t, 0)),
                  pl.BlockSpec((None, None, d, hid), lambda t, te, nt: (layer, te[t], 0, 0)),
                  pl.BlockSpec((None, None, d, hid), lambda t, te, nt: (layer, te[t], 0, 0)),
                  pl.BlockSpec((None, None, hid, d), lambda t, te, nt: (layer, te[t], 0, 0))],
        out_specs=pl.BlockSpec((TMOE, d), lambda t, te, nt: (t, 0)),
        scratch_shapes=[pltpu.VMEM((d, hid), BF16), pltpu.VMEM((d, hid), BF16), pltpu.VMEM((hid, d), BF16)],
    )
    return pl.pallas_call(
        _expert_kernel,
        grid_spec=grid_spec,
        out_shape=jax.ShapeDtypeStruct((rows, d), F32),
        compiler_params=_cparams(("arbitrary",)),
        name="moe_experts",
    )(tile_e, ntiles, xs, w_gate, w_up, w_down)


def _combine_kernel(y1_ref, y2_ref, route_ref, h_ref, mod_ref, fg_ref, out_ref, *, final):
    route = route_ref[...]
    y = route[:, 2:3] * y1_ref[...] + route[:, 3:4] * y2_ref[...]
    hn = _gated_residual(h_ref[...], mod_ref[5], y)
    if final:
        hn = hn * lax.rsqrt(jnp.mean(hn * hn, axis=-1, keepdims=True) + EPS) * fg_ref[...]
    out_ref[...] = hn


def _combine(y1, y2, route, h, mod, sel, fg, final):
    n, d = h.shape
    tok = pl.BlockSpec((TM, d), lambda t: (t, 0))
    return pl.pallas_call(
        functools.partial(_combine_kernel, final=final),
        grid=(n // TM,),
        in_specs=[tok, tok, pl.BlockSpec((TM, 128), lambda t: (t, 0)), tok,
                  _mod_spec(d, sel),
                  pl.BlockSpec((1, d), lambda t: (0, 0))],
        out_specs=tok,
        out_shape=jax.ShapeDtypeStruct((n, d), F32),
        compiler_params=_cparams(("arbitrary",)),
        name="moe_combine",
    )(y1, y2, route, h, mod, fg.reshape(1, d))


def _moe(h, g, mod, sel, wr, br, w_gate, w_up, w_down, layer, fg, final):
    n, d = h.shape
    xl, route = _route(h, g, mod, sel, wr, br)
    e = route[:, :2].astype(jnp.int32).reshape(-1)
    onehot = (e[:, None] == jnp.arange(N_EXPERTS, dtype=jnp.int32)[None, :]).astype(jnp.int32)
    csum = jnp.cumsum(onehot, axis=0)
    counts = csum[-1]
    rank = jnp.sum(csum * onehot, axis=1) - 1
    padded = ((counts + TMOE - 1) // TMOE) * TMOE
    pend = jnp.cumsum(padded)
    pos = (pend - padded)[e] + rank
    rows = 2 * n + N_EXPERTS * TMOE
    src = (jnp.arange(rows, dtype=jnp.int32) % n).at[pos].set(jnp.arange(2 * n, dtype=jnp.int32) // 2,
                                                     mode="promise_in_bounds", unique_indices=True)
    ntile = rows // TMOE
    nvalid = (pend[-1] // TMOE).astype(jnp.int32)
    tstart = jnp.arange(ntile, dtype=jnp.int32) * TMOE
    tile_e = jnp.sum((tstart[:, None] >= pend[None, :]).astype(jnp.int32), axis=1)
    last_e = jnp.sum((((nvalid - 1) * TMOE) >= pend).astype(jnp.int32))
    tile_e = jnp.where(tstart < pend[-1], tile_e, last_e).astype(jnp.int32)
    xs = xl.at[src].get(mode="promise_in_bounds")
    ys = _experts(tile_e, nvalid.reshape(1), xs, w_gate, w_up, w_down, layer)
    pos2 = pos.reshape(n, 2)
    y1 = ys.at[pos2[:, 0]].get(mode="promise_in_bounds", unique_indices=True)
    y2 = ys.at[pos2[:, 1]].get(mode="promise_in_bounds", unique_indices=True)
    return _combine(y1, y2, route, h, mod, sel, fg, final)


def _block_ones(width, seg):
    idx = np.arange(width) // seg
    return jnp.asarray((idx[:, None] == idx[None, :]).astype(np.float32))


def _rope_tables(lc, l):
    rows = l // GRID_W
    row = np.repeat(np.arange(rows, dtype=np.float32), GRID_W)
    col = np.tile(np.arange(GRID_W, dtype=np.float32), rows)
    n_freq = QK_ROPE // 4
    inv_freq = jnp.asarray(ROPE_BASE, F32) ** (-jnp.arange(n_freq, dtype=F32) / n_freq)
    ang_r = jnp.asarray(row)[:, None] * inv_freq
    ang_c = jnp.asarray(col)[:, None] * inv_freq
    cos = jnp.concatenate([jnp.cos(ang_r), jnp.cos(ang_r), jnp.cos(ang_c), jnp.cos(ang_c)], axis=-1)
    sin = jnp.concatenate([jnp.sin(ang_r), jnp.sin(ang_r), jnp.sin(ang_c), jnp.sin(ang_c)], axis=-1)
    cos = jnp.concatenate([jnp.ones((lc, QK_ROPE), F32), cos], axis=0)
    sin = jnp.concatenate([jnp.zeros((lc, QK_ROPE), F32), sin], axis=0)
    return jnp.repeat(cos, NB, axis=0), jnp.repeat(sin, NB, axis=0)


def _rot_cols(pe):
    q = QK_ROPE // 4
    return jnp.concatenate([-pe[..., q:2 * q], pe[..., 0:q], -pe[..., 3 * q:4 * q], pe[..., 2 * q:3 * q]], axis=-1)


def _pad_cols(w, n):
    return jnp.pad(w, ((0, 0), (0, n - w.shape[1])))


def kernel(x, c, ctx, c_ctx, ada_w, ada_b, norm1_g, norm2_g, final_g, ev_w_in, ev_shift_mu, rwkv_w0, rwkv_w2, rwkv_a0, rwkv_a2, rwkv_g2, rwkv_k_k, rwkv_k_a, rwkv_r_k, rwkv_lnx_w, rwkv_lnx_b, ev_w_out, mla_w_in, mla_q_norm_g, mla_w_uq, mla_kv_norm_g, mla_w_ukv, mla_w_o, moe_w_grp, moe_b_grp, moe_w_exp, moe_b_exp, moe_w_gate, moe_w_up, moe_w_down):
    b, l, d = x.shape
    lc = ctx.shape[1]
    s = lc + l
    n = b * s
    steps = TM // NB
    nct = lc // steps
    assert b == NB and lc % steps == 0 and l % steps == 0 and ada_w.shape[0] == 2

    def sel_all(t):
        return (t >= nct).astype(jnp.int32)

    def sel_lat(t):
        return 1

    c_all = jnp.concatenate([c, c_ctx[None], jnp.zeros((MOD_ROWS - b - 1, d), F32)], axis=0)
    ada = _ada_table(c_all, ada_w, ada_b)
    mod = jnp.stack([jnp.broadcast_to(ada[:, b][:, :, None, :], (2, N_MOD, NB, d)),
                     ada[:, :b].transpose(0, 2, 1, 3)], axis=1)
    h = jnp.concatenate([ctx, x], axis=1).transpose(1, 0, 2).reshape(n, d)

    def router_weights(layer):
        wr = jnp.concatenate([moe_w_exp[layer], moe_w_grp[layer]], axis=1)
        br = jnp.concatenate([moe_b_exp[layer], moe_b_grp[layer]])[None]
        return _pad_cols(wr, 128), _pad_cols(br, 128)

    def two_term(w):
        hi = w.astype(BF16)
        return jnp.stack([hi, (w - hi.astype(F32)).astype(BF16)])

    w_ = RWKV_W
    ret_w = 2 * RET_HEADS * RET_DK + RET_HEADS * RET_DV
    w_in = _pad_cols(ev_w_in[0], 3584).astype(BF16)
    qkv, gret, rw = _norm_proj(h, norm1_g[0], mod[0], sel_all, w_in,
                               (ret_w, RET_HEADS * RET_DV, 2048), (BF16, F32, F32))
    lg = jnp.log1p(-jnp.exp2(-5.0 - jnp.arange(RET_HEADS, dtype=F32)))
    o_ret = _retention(qkv, jnp.broadcast_to(lg[:, None, None], (RET_HEADS, 1, 128)), b, lc, l)

    ones8 = _block_ones(w_, RWKV_N).astype(BF16)
    zero = jnp.zeros((64, w_), F32)
    w2bd = jnp.concatenate([jnp.concatenate([rwkv_w2[0, 0], zero], axis=1),
                            jnp.concatenate([zero, rwkv_w2[0, 1]], axis=1)], axis=0)
    a2bd = jnp.concatenate([jnp.concatenate([rwkv_a2[0, 0], zero], axis=1),
                            jnp.concatenate([zero, rwkv_a2[0, 1]], axis=1)], axis=0)
    g2p = jnp.pad(rwkv_g2[0], ((0, 256 - rwkv_g2.shape[1]), (0, 0)))
    mu = _pad_cols(ev_shift_mu[0], 2048)
    feat_params = (mu, rwkv_w0[0].reshape(1, 2 * w_), two_term(w2bd), rwkv_a0[0].reshape(1, 2 * w_),
                   two_term(a2bd), two_term(g2p),
                   rwkv_k_k[0][None], rwkv_k_a[0][None], rwkv_r_k[0].reshape(1, w_), ones8)
    p1, p2f, p2b, p3f, p3b, gate, bonus = _rwkv_features(rw, feat_params, nct)
    y2 = _rwkv_scan(p1, p2f, p2b, p3f, p3b, nct)

    h = _even_out(o_ret, gret, y2, bonus, gate, rwkv_lnx_w[0][None], rwkv_lnx_b[0][None], ones8,
                  ev_w_out[0].astype(BF16), h, mod[0], sel_all)
    wr, br = router_weights(0)
    h = _moe(h, norm2_g[0], mod[0], sel_all, wr, br, moe_w_gate, moe_w_up, moe_w_down, 0, final_g, False)

    w_in1 = jnp.concatenate([mla_w_in[0], _rot_cols(mla_w_in[0][:, Q_RANK + KV_RANK:])], axis=1).astype(BF16)
    (a1,) = _norm_proj(h, norm1_g[1], mod[1], sel_all, w_in1, (w_in1.shape[1],), (F32,))
    wq = mla_w_uq[0].reshape(Q_RANK, MLA_HEADS, QK_NOPE + QK_ROPE)
    wq = jnp.concatenate([wq, _rot_cols(wq[..., QK_NOPE:])], axis=-1).reshape(Q_RANK, -1).astype(BF16)
    cos, sin = _rope_tables(lc, l)
    qn, qp, kn, vv, kp = _mla_proj(a1, mla_q_norm_g[0][None], mla_kv_norm_g[0][None], wq,
                                   mla_w_ukv[0].astype(BF16), cos, sin, nct)
    o = _attention(qn, qp, kn, kp, vv, b, lc, l)
    h = _oproj(o, mla_w_o[0].astype(BF16), h, mod[1], lambda t: t + nct, sel_lat)
    wr, br = router_weights(1)
    h = _moe(h, norm2_g[1], mod[1], sel_lat, wr, br, moe_w_gate, moe_w_up, moe_w_down, 1, final_g, True)
    return h.reshape(l, b, d).transpose(1, 0, 2)
```

```python
import functools

import jax
import jax.numpy as jnp
import numpy as np
from jax import lax
from jax.experimental import pallas as pl
from jax.experimental.pallas import tpu as pltpu

F32 = jnp.float32
BF16 = jnp.bfloat16
HIGHEST = lax.Precision.HIGHEST

NB = 8
TM = 256
RET_TQ = 256
TQ_ATTN = 256
TMOE = 512
EPS = 1e-6
GN_EPS = 64e-5
GRID_W = 64
ROPE_BASE = 10000.0

RET_HEADS, RET_DK, RET_DV = 4, 64, 128
RWKV_HEADS, RWKV_N = 8, 64
RWKV_W = RWKV_HEADS * RWKV_N
MLA_HEADS, Q_RANK, KV_RANK, QK_NOPE, QK_ROPE, V_HEAD = 8, 384, 256, 128, 64, 128
MLA_SCALE = (QK_NOPE + QK_ROPE) ** -0.5
N_GROUPS, EXPERTS_PER_GROUP = 4, 8
N_EXPERTS = N_GROUPS * EXPERTS_PER_GROUP
N_MOD = 6
MOD_ROWS = 16


def _cparams(sem):
    return pltpu.CompilerParams(dimension_semantics=sem)


def _rows8(x, fn):
    tm, d = x.shape
    return fn(x.reshape(tm // NB, NB, d)).reshape(tm, d)


def _norm_mod(x, g, shift8, scale8):
    var = jnp.mean(x * x, axis=-1, keepdims=True)
    y = x * lax.rsqrt(var + EPS) * g
    return _rows8(y, lambda y3: y3 * (1.0 + scale8[None]) + shift8[None])


def _gated_residual(h, gate8, y):
    return h + _rows8(y, lambda y3: y3 * gate8[None])


def _mod_spec(d, sel):
    return pl.BlockSpec((None, N_MOD, NB, d), lambda t: (sel(t), 0, 0, 0))


def _split_bf16(x):
    hi = x.astype(BF16)
    return hi, (x - hi.astype(F32)).astype(BF16)


def _seg_sum(x, ones):
    hi, lo = _split_bf16(x)
    return jnp.dot(hi, ones, preferred_element_type=F32) + jnp.dot(lo, ones, preferred_element_type=F32)


def _seg_sum_left(sel, x):
    hi, lo = _split_bf16(x)
    return jnp.dot(sel, hi, preferred_element_type=F32) + jnp.dot(sel, lo, preferred_element_type=F32)


def _dot3(x, w_ref):
    hi, lo = _split_bf16(x)
    w_hi = w_ref[0]
    return (jnp.dot(hi, w_hi, preferred_element_type=F32) + jnp.dot(lo, w_hi, preferred_element_type=F32)
            + jnp.dot(hi, w_ref[1], preferred_element_type=F32))


def _ada_kernel(c_ref, w_ref, b_ref, o_ref):
    s = c_ref[...]
    s = s * jax.nn.sigmoid(s)
    o_ref[...] = jnp.dot(s, w_ref[...], precision=HIGHEST, preferred_element_type=F32) + b_ref[...]


def _ada_table(c_all, ada_w, ada_b):
    depth, d, nd = ada_w.shape
    out = pl.pallas_call(
        _ada_kernel,
        grid=(depth, nd // d),
        in_specs=[pl.BlockSpec((MOD_ROWS, d), lambda l, j: (0, 0)),
                  pl.BlockSpec((None, d, d), lambda l, j: (l, 0, j)),
                  pl.BlockSpec((None, 1, d), lambda l, j: (l, 0, j))],
        out_specs=pl.BlockSpec((None, MOD_ROWS, d), lambda l, j: (l, 0, j)),
        out_shape=jax.ShapeDtypeStruct((depth, MOD_ROWS, nd), F32),
        compiler_params=_cparams(("arbitrary", "arbitrary")),
        name="ada_table",
    )(c_all, ada_w, ada_b.reshape(depth, 1, nd))
    return out.reshape(depth, MOD_ROWS, N_MOD, d)


def _proj_kernel(x_ref, g_ref, mod_ref, w_ref, p_ref, *o_refs, splits, nperm):
    xm = _norm_mod(x_ref[...], g_ref[...], mod_ref[0], mod_ref[1]).astype(BF16)
    tm = xm.shape[0]
    xp = jnp.dot(p_ref[...], xm, preferred_element_type=F32).astype(BF16) if nperm else None
    off = 0
    for idx, (o_ref, n) in enumerate(zip(o_refs, splits)):
        for j in range(0, n, 512):
            c = min(512, n - j)
            w = w_ref[:, off + j:off + j + c]
            if idx < nperm:
                res = jnp.dot(xp, w, preferred_element_type=F32).astype(o_ref.dtype)
                o_ref[:, :, j:j + c] = res.reshape(NB, tm // NB, c)
            else:
                o_ref[:, j:j + c] = jnp.dot(xm, w, preferred_element_type=F32).astype(o_ref.dtype)
        off += n


def _norm_proj(h, g, mod, sel, w_bf16, perm, splits, dtypes, nperm):
    n, d = h.shape
    nout = w_bf16.shape[1]
    steps = TM // NB
    out_specs, out_shape = [], []
    for idx, (s, dt) in enumerate(zip(splits, dtypes)):
        if idx < nperm:
            out_specs.append(pl.BlockSpec((NB, steps, s), lambda t: (0, t, 0)))
            out_shape.append(jax.ShapeDtypeStruct((NB, n // NB, s), dt))
        else:
            out_specs.append(pl.BlockSpec((TM, s), lambda t: (t, 0)))
            out_shape.append(jax.ShapeDtypeStruct((n, s), dt))
    return pl.pallas_call(
        functools.partial(_proj_kernel, splits=splits, nperm=nperm),
        grid=(n // TM,),
        in_specs=[pl.BlockSpec((TM, d), lambda t: (t, 0)),
                  pl.BlockSpec((1, d), lambda t: (0, 0)),
                  _mod_spec(d, sel),
                  pl.BlockSpec((d, nout), lambda t: (0, 0)),
                  pl.BlockSpec(perm.shape, lambda t: (0, 0))],
        out_specs=out_specs,
        out_shape=out_shape,
        compiler_params=_cparams(("arbitrary",)),
        name="norm_proj",
    )(h, g.reshape(1, d), mod, w_bf16, perm)


def _ret_kernel(lg_ref, q_ref, k_ref, v_ref, o_ref, g_ref, *, lc, l, tq):
    qi = pl.program_id(2)
    nct = lc // tq
    nk = (lc + l) // tq
    dn = (((1,), (1,)), ((), ()))
    scale = RET_DK ** -0.5

    @pl.when(qi == 0)
    def _():
        rel = (lax.broadcasted_iota(jnp.int32, (tq, tq), 0)
               - lax.broadcasted_iota(jnp.int32, (tq, tq), 1)).astype(F32)
        for hh in range(2):
            lg = lg_ref[hh][0:1, 0:1]
            g_ref[hh, 0] = jnp.exp(lg * rel)
            g_ref[hh, 1] = jnp.exp(-(lg * rel))
            g_ref[hh, 2] = jnp.exp(lg * jnp.abs(rel))

    q_lat = qi >= nct
    q2 = q_ref[...]
    k2 = k_ref[...]
    v2 = v_ref[...]
    for hh in range(2):
        lg = lg_ref[hh][0:1, 0:1]
        s = lax.dot_general(q2[:, hh * RET_DK:(hh + 1) * RET_DK], k2[:, hh * RET_DK:(hh + 1) * RET_DK], dn,
                            preferred_element_type=F32)
        pieces = []
        for kj in range(nk):
            d = qi - kj
            idx = jnp.where(d > 0, 0, jnp.where(d < 0, 1, 2))
            dabs = jnp.full((1, 1), jnp.abs(d) * tq, jnp.int32).astype(F32)
            sig = jnp.exp(lg * dabs) * scale
            if kj < nct:
                dback = jnp.full((1, 1), l + lc - d * tq, jnp.int32).astype(F32)
                sig2 = jnp.where(q_lat, jnp.exp(lg * dback) * scale, 0.0)
                m = g_ref[hh, idx] * sig + g_ref[hh, 1] * sig2
            else:
                m = g_ref[hh, idx] * jnp.where(q_lat, sig, 0.0)
            pieces.append((s[:, kj * tq:(kj + 1) * tq] * m).astype(BF16))
        p = jnp.concatenate(pieces, axis=1)
        o_ref[:, hh * RET_DV:(hh + 1) * RET_DV] = jnp.dot(p, v2[:, hh * RET_DV:(hh + 1) * RET_DV],
                                                          preferred_element_type=F32)


def _retention(qkv, lg, lc, l):
    b, s, _ = qkv.shape
    tq = RET_TQ if lc % RET_TQ == 0 else 128
    qk_w = 2 * RET_DK
    v_w = 2 * RET_DV
    k_blk0 = RET_HEADS * RET_DK // qk_w
    v_blk0 = 2 * RET_HEADS * RET_DK // v_w
    return pl.pallas_call(
        functools.partial(_ret_kernel, lc=lc, l=l, tq=tq),
        grid=(b, RET_HEADS // 2, s // tq),
        in_specs=[pl.BlockSpec((2, 1, 128), lambda bi, hp, qi: (hp, 0, 0)),
                  pl.BlockSpec((None, tq, qk_w), lambda bi, hp, qi: (bi, qi, hp)),
                  pl.BlockSpec((None, s, qk_w), lambda bi, hp, qi: (bi, 0, k_blk0 + hp)),
                  pl.BlockSpec((None, s, v_w), lambda bi, hp, qi: (bi, 0, v_blk0 + hp))],
        out_specs=pl.BlockSpec((None, tq, v_w), lambda bi, hp, qi: (bi, qi, hp)),
        out_shape=jax.ShapeDtypeStruct((b, s, RET_HEADS * RET_DV), F32),
        scratch_shapes=[pltpu.VMEM((2, 3, tq, tq), F32)],
        compiler_params=_cparams(("arbitrary", "arbitrary", "arbitrary")),
        name="retention",
    )(lg, qkv, qkv, qkv)


def _head_pairs(qa, qb):
    tm = qa.shape[0]
    lo = lax.broadcasted_iota(jnp.int32, (tm, 128), 1) < RWKV_N
    out = []
    for c in range(RWKV_W // 128):
        a = qa[:, c * 128:(c + 1) * 128]
        b = qb[:, c * 128:(c + 1) * 128]
        out.append(jnp.where(lo, a, pltpu.roll(b, RWKV_N, 1)))
        out.append(jnp.where(lo, pltpu.roll(a, RWKV_N, 1), b))
    return out


def _feat_kernel(rw_ref, prev_ref, next_ref, mu_ref, w0_ref, w2_ref, a0_ref, a2_ref, g2_ref,
                 kkw_ref, ka_ref, rk_ref, ones_ref,
                 p1_o, p2f_o, p2b_o, p3f_o, p3b_o, gate_o, bonus_o, *, nct, ntile):
    t = pl.program_id(0)
    first = jnp.logical_or(t == 0, t == nct)
    last = jnp.logical_or(t == nct - 1, t == ntile - 1)
    y = rw_ref[...]
    tm, wid = y.shape
    grp = lax.broadcasted_iota(jnp.int32, (tm // NB, 1, 1), 0)
    prow = jnp.where(first, 0.0, prev_ref[...])
    nrow = jnp.where(last, 0.0, next_ref[...])
    prev = jnp.where(grp == 0, prow[None], pltpu.roll(y, NB, 0).reshape(tm // NB, NB, wid)).reshape(tm, wid)
    nxt = jnp.where(grp == tm // NB - 1, nrow[None],
                    pltpu.roll(y, tm - NB, 0).reshape(tm // NB, NB, wid)).reshape(tm, wid)
    ys = y + mu_ref[0:1, :] * (prev - y) + mu_ref[1:2, :] * (nxt - y)

    w_ = RWKV_W
    r = ys[:, 0:w_]
    kr = ys[:, w_:2 * w_]
    vr = ys[:, 2 * w_:3 * w_]
    wd = ys[:, 3 * w_:3 * w_ + 128]
    ad = ys[:, 3 * w_ + 128:3 * w_ + 256]
    gd = ys[:, 3 * w_ + 256:3 * w_ + 512]
    ones = ones_ref[...]

    kk = kr * kkw_ref[...]
    ss = _seg_sum(kk * kk, ones)
    kk = kk / jnp.maximum(jnp.sqrt(ss), 1e-12)
    zw = w0_ref[...] + _dot3(jnp.tanh(wd), w2_ref)
    wlog = -(jnp.maximum(-zw, 0.0) + jnp.log1p(jnp.exp(-jnp.abs(zw)))) - 0.5
    decay = jnp.exp(-jnp.exp(wlog))
    a = jax.nn.sigmoid(a0_ref[...] + _dot3(ad, a2_ref))
    gate_o[...] = _dot3(jax.nn.sigmoid(gd), g2_ref)
    ka = ka_ref[...]

    def emit(o_ref, qa, qb):
        for hh, slab in enumerate(_head_pairs(qa, qb)):
            o_ref[hh] = slab

    emit(p1_o, kk, r)
    ktsum = None
    for d, (p2_o, p3_o) in enumerate(((p2f_o, p3f_o), (p2b_o, p3b_o))):
        a_d = a[:, d * w_:(d + 1) * w_]
        kt = kr * (1.0 + (a_d - 1.0) * ka)
        emit(p2_o, decay[:, d * w_:(d + 1) * w_], kk * a_d)
        emit(p3_o, kt, vr)
        ktsum = kt if ktsum is None else ktsum + kt
    bonus_o[...] = _seg_sum(r * ktsum * rk_ref[...], ones) * vr


def _rwkv_features(rw, params, nct):
    n, wid = rw.shape
    ntile = n // TM
    rb = TM // 8
    nrb = n // 8
    w_ = RWKV_W
    tok = pl.BlockSpec((TM, w_), lambda t: (t, 0))
    pair = pl.BlockSpec((RWKV_HEADS, TM, 128), lambda t: (0, t, 0))

    def full(a):
        return pl.BlockSpec(a.shape, lambda t: (0,) * a.ndim)

    one = jax.ShapeDtypeStruct((n, w_), F32)
    pshape = jax.ShapeDtypeStruct((RWKV_HEADS, n, 128), F32)
    return pl.pallas_call(
        functools.partial(_feat_kernel, nct=nct, ntile=ntile),
        grid=(ntile,),
        in_specs=[pl.BlockSpec((TM, wid), lambda t: (t, 0)),
                  pl.BlockSpec((8, wid), lambda t: (jnp.maximum(t * rb - 1, 0), 0)),
                  pl.BlockSpec((8, wid), lambda t: (jnp.minimum((t + 1) * rb, nrb - 1), 0))]
                 + [full(a) for a in params],
        out_specs=[pair] * 5 + [tok, tok],
        out_shape=[pshape] * 5 + [one, one],
        compiler_params=_cparams(("arbitrary",)),
        name="rwkv_features",
    )(rw, rw, rw, *params)


def _scan_kernel(p1f, p1b, p2f, p2b, p3f, p3b, yf_ref, yb_ref, s_ref, t_ref, y_buf):
    @pl.when(pl.program_id(0) == 0)
    def _():
        s_ref[...] = jnp.zeros_like(s_ref)
        y_buf[...] = jnp.zeros_like(y_buf)

    nkey = s_ref.shape[0]
    nh = p1f.shape[0]
    tc = p1f.shape[1] // NB
    half = NB * nh
    lanes = 2 * half
    kblk = 8
    pairs = ((p1f, p1b), (p2f, p2b), (p3f, p3b))
    kk_q, r_q, w_q, kka_q, kt_q, v_q = (0, 0), (0, nkey), (1, 0), (1, nkey), (2, 0), (2, nkey)

    def rows_of(i):
        return pl.ds(pl.multiple_of(i * NB, NB), NB)

    def relayout(i, dst):
        for p, (f_ref, b_ref) in enumerate(pairs):
            rows = ([f_ref[hh, rows_of(i), :] for hh in range(nh)]
                    + [b_ref[hh, rows_of(tc - 1 - i), :] for hh in range(nh)])
            dst[p] = jnp.concatenate(rows, axis=0).T

    def emit(i):
        y = y_buf[...].reshape(nkey, lanes)
        yt = jnp.concatenate([y, y], axis=0).T
        for hh in range(nh):
            yf_ref[hh, rows_of(i), :] = yt[hh * NB:(hh + 1) * NB]
            yb_ref[hh, rows_of(tc - 1 - i), :] = yt[half + hh * NB:half + (hh + 1) * NB]

    def step(i, cur, nxt):
        def row(q, k):
            return cur[q[0], pl.ds(q[1] + k, 1), :][None]

        sa = s_ref[0] * row(kk_q, 0)
        for k in range(1, nkey):
            sa = sa + s_ref[k] * row(kk_q, k)
        relayout(jnp.minimum(i + 1, tc - 1), nxt)
        emit(jnp.maximum(i - 1, 0))
        v = cur[v_q[0], v_q[1]:v_q[1] + nkey, :].reshape(nkey // 8, 8, lanes)

        def upd_body(kb, y):
            base = pl.multiple_of(kb * kblk, kblk)
            for j in range(kblk):
                k = base + j
                s_new = s_ref[k] * row(w_q, k) + (v * row(kt_q, k) - sa * row(kka_q, k))
                s_ref[k] = s_new
                y = y + s_new * row(r_q, k)
            return y

        y_buf[...] = lax.fori_loop(0, nkey // kblk, upd_body, jnp.zeros((nkey // 8, 8, lanes), F32))

    relayout(0, t_ref.at[0])

    def two_steps(j, carry):
        step(2 * j, t_ref.at[0], t_ref.at[1])
        step(2 * j + 1, t_ref.at[1], t_ref.at[0])
        return carry

    lax.fori_loop(0, tc // 2, two_steps, 0)
    emit(tc - 1)


def _rwkv_scan(p1, p2f, p2b, p3f, p3b, nct):
    nh, n, _ = p1.shape
    nkey = RWKV_N
    ntb = n // TM

    def mirror(g):
        return jnp.where(g < nct, nct - 1 - g, nct + ntb - 1 - g)

    sf = pl.BlockSpec((nh, TM, 128), lambda g: (0, g, 0))
    sb = pl.BlockSpec((nh, TM, 128), lambda g: (0, mirror(g), 0))
    out = jax.ShapeDtypeStruct((nh, n, 128), F32)
    lanes = 2 * NB * nh
    return pl.pallas_call(
        _scan_kernel,
        grid=(ntb,),
        in_specs=[sf, sb, sf, sb, sf, sb],
        out_specs=[sf, sb],
        out_shape=[out, out],
        scratch_shapes=[pltpu.VMEM((nkey, nkey // 8, 8, lanes), F32),
                        pltpu.VMEM((2, 3, 2 * nkey, lanes), F32),
                        pltpu.VMEM((nkey // 8, 8, lanes), F32)],
        compiler_params=_cparams(("arbitrary",)),
        name="rwkv_scan",
    )(p1, p1, p2f, p2b, p3f, p3b)


def _even_out_kernel(o_ref, g_ref, yf_ref, yb_ref, bonus_ref, gate_ref, lnw_ref, lnb_ref, ones_ref, pt_ref,
                     w_ref, h_ref, mod_ref, out_ref):
    tm = h_ref.shape[0]
    o_all = _seg_sum_left(pt_ref[...], o_ref[...].reshape(tm, RET_HEADS * RET_DV))
    parts = []
    for hh in range(RET_HEADS):
        o = o_all[:, hh * RET_DV:(hh + 1) * RET_DV]
        o = o * lax.rsqrt(jnp.mean(o * o, axis=-1, keepdims=True) + EPS)
        gg = g_ref[:, hh * RET_DV:(hh + 1) * RET_DV]
        parts.append(o * (gg * jax.nn.sigmoid(gg)))
    ones = ones_ref[...]
    lo = lax.broadcasted_iota(jnp.int32, (tm, 128), 1) < RWKV_N
    chunks = []
    for c in range(RWKV_HEADS // 2):
        even = yf_ref[2 * c] + yb_ref[2 * c]
        odd = yf_ref[2 * c + 1] + yb_ref[2 * c + 1]
        chunks.append(jnp.where(lo, even, pltpu.roll(odd, RWKV_N, 1)))
    y = jnp.concatenate(chunks, axis=-1)
    mu = _seg_sum(y, ones) * (1.0 / RWKV_N)
    yc = y - mu
    var = _seg_sum(yc * yc, ones) * (1.0 / RWKV_N)
    yn = yc * lax.rsqrt(var + GN_EPS) * lnw_ref[...] + lnb_ref[...]
    parts.append((yn + bonus_ref[...]) * gate_ref[...])
    cat = jnp.concatenate(parts, axis=-1).astype(BF16)
    mix = jnp.dot(cat, w_ref[...], preferred_element_type=F32)
    out_ref[...] = _gated_residual(h_ref[...], mod_ref[2], mix)


def _even_out(o_ret, g, y2, bonus, gate, lnw, lnb, ones, perm_t, w_out, h, mod, sel):
    n, d = h.shape
    w_ = RWKV_W
    tok = pl.BlockSpec((TM, w_), lambda t: (t, 0))
    ysp = pl.BlockSpec((RWKV_HEADS, TM, 128), lambda t: (0, t, 0))
    osp = pl.BlockSpec((NB, TM // NB, o_ret.shape[2]), lambda t: (0, t, 0))

    def full(a):
        return pl.BlockSpec(a.shape, lambda t: (0,) * a.ndim)

    return pl.pallas_call(
        _even_out_kernel,
        grid=(n // TM,),
        in_specs=[osp, tok, ysp, ysp, tok, tok, full(lnw), full(lnb), full(ones), full(perm_t), full(w_out),
                  pl.BlockSpec((TM, d), lambda t: (t, 0)),
                  _mod_spec(d, sel)],
        out_specs=pl.BlockSpec((TM, d), lambda t: (t, 0)),
        out_shape=jax.ShapeDtypeStruct((n, d), F32),
        compiler_params=_cparams(("arbitrary",)),
        name="even_out",
    )(o_ret, g, y2[0], y2[1], bonus, gate, lnw, lnb, ones, perm_t, w_out, h, mod)


def _mla_proj_kernel(a_ref, qg_ref, kvg_ref, wuq_ref, wukv_ref, cos_ref, sin_ref,
                     qn_o, qp_o, kn_o, v_o, kp_o):
    nb, steps, wid = a_ref.shape
    tm = nb * steps
    a = a_ref[...].reshape(tm, wid)
    cos = jnp.broadcast_to(cos_ref[...][None], (nb, steps, QK_ROPE)).reshape(tm, QK_ROPE)
    sin = jnp.broadcast_to(sin_ref[...][None], (nb, steps, QK_ROPE)).reshape(tm, QK_ROPE)
    zpad = jnp.zeros((tm, 128 - QK_ROPE), F32)

    def rms(t, g):
        return (t * lax.rsqrt(jnp.mean(t * t, axis=-1, keepdims=True) + EPS) * g).astype(BF16)

    def pad128(pe):
        return jnp.concatenate([pe, zpad], axis=1).astype(BF16).reshape(nb, steps, 128)

    cq = rms(a[:, :Q_RANK], qg_ref[...])
    ckv = rms(a[:, Q_RANK:Q_RANK + KV_RANK], kvg_ref[...])
    pe0 = Q_RANK + KV_RANK
    kp_o[...] = pad128(a[:, pe0:pe0 + QK_ROPE] * cos + a[:, pe0 + QK_ROPE:pe0 + 2 * QK_ROPE] * sin)
    hw = QK_NOPE + 2 * QK_ROPE
    for hh in range(MLA_HEADS):
        hs = slice(hh * 128, (hh + 1) * 128)
        qh = jnp.dot(cq, wuq_ref[:, hh * hw:(hh + 1) * hw], preferred_element_type=F32)
        qn_o[:, :, hs] = (qh[:, :QK_NOPE] * MLA_SCALE).astype(BF16).reshape(nb, steps, 128)
        qp_o[:, :, hs] = pad128(
            (qh[:, QK_NOPE:QK_NOPE + QK_ROPE] * cos + qh[:, QK_NOPE + QK_ROPE:] * sin) * MLA_SCALE)
        kvh = jnp.dot(ckv, wukv_ref[:, hh * hw:(hh + 1) * hw], preferred_element_type=F32)
        kn_o[:, :, hs] = kvh[:, :QK_NOPE].astype(BF16).reshape(nb, steps, 128)
        v_o[:, :, hs] = kvh[:, QK_NOPE:].astype(BF16).reshape(nb, steps, 128)


def _mla_proj(a, qg, kvg, wuq, wukv, cos, sin, nct):
    nb, s, wid = a.shape
    hw = MLA_HEADS * 128
    steps = TM // NB
    ntile = s // steps
    l = (ntile - nct) * steps

    def full(x):
        return pl.BlockSpec(x.shape, lambda t: (0,) * x.ndim)

    def lat_blk(t):
        return jnp.maximum(t - nct, 0)

    allrows = pl.BlockSpec((nb, steps, hw), lambda t: (0, t, 0))
    latrows = pl.BlockSpec((nb, steps, hw), lambda t: (0, lat_blk(t), 0))
    rope = pl.BlockSpec((steps, QK_ROPE), lambda t: (t, 0))
    return pl.pallas_call(
        _mla_proj_kernel,
        grid=(ntile,),
        in_specs=[pl.BlockSpec((nb, steps, wid), lambda t: (0, t, 0)), full(qg), full(kvg), full(wuq),
                  full(wukv), rope, rope],
        out_specs=[latrows, latrows, allrows, allrows, pl.BlockSpec((nb, steps, 128), lambda t: (0, t, 0))],
        out_shape=[jax.ShapeDtypeStruct((nb, l, hw), BF16),
                   jax.ShapeDtypeStruct((nb, l, hw), BF16),
                   jax.ShapeDtypeStruct((nb, s, hw), BF16),
                   jax.ShapeDtypeStruct((nb, s, hw), BF16),
                   jax.ShapeDtypeStruct((nb, s, 128), BF16)],
        compiler_params=_cparams(("arbitrary",)),
        name="mla_proj",
    )(a, qg, kvg, wuq, wukv, cos, sin)


def _attn_kernel(qn_ref, qp_ref, kn_ref, kp_ref, v_ref, o_ref):
    hi = pl.program_id(1)
    dn = (((1,), (1,)), ((), ()))
    hsl = pl.ds(pl.multiple_of(hi * 128, 128), 128)
    q = jnp.concatenate([qn_ref[...], qp_ref[...]], axis=1)
    k = jnp.concatenate([kn_ref[:, hsl], kp_ref[...]], axis=1)
    s = lax.dot_general(q, k, dn, preferred_element_type=F32)
    m = jnp.max(s, axis=-1, keepdims=True)
    p = jnp.exp(s - m)
    l = jnp.sum(p, axis=-1, keepdims=True)
    o = jnp.dot(p.astype(BF16), v_ref[:, hsl], preferred_element_type=F32)
    o_ref[...] = (o / l).astype(o_ref.dtype)


def _attention(qn, qp, kn, kp, v):
    b, l, hw = qn.shape
    s = kn.shape[1]
    hd = MLA_HEADS
    tq = TQ_ATTN if l % TQ_ATTN == 0 else 128
    qspec = pl.BlockSpec((None, tq, 128), lambda bi, hi, qi: (bi, qi, hi))
    kvspec = pl.BlockSpec((None, s, hw), lambda bi, hi, qi: (bi, 0, 0))
    return pl.pallas_call(
        _attn_kernel,
        grid=(b, hd, l // tq),
        in_specs=[qspec, qspec, kvspec,
                  pl.BlockSpec((None, s, 128), lambda bi, hi, qi: (bi, 0, 0)),
                  kvspec],
        out_specs=qspec,
        out_shape=jax.ShapeDtypeStruct((b, l, hw), BF16),
        compiler_params=_cparams(("arbitrary", "arbitrary", "arbitrary")),
        name="mla_attention",
    )(qn, qp, kn, kp, v)


def _oproj_kernel(o_ref, pt_ref, w_ref, h_ref, mod_ref, out_ref):
    nb, steps, wid = o_ref.shape
    o = jnp.dot(pt_ref[...], o_ref[...].reshape(nb * steps, wid), preferred_element_type=F32).astype(BF16)
    mix = jnp.dot(o, w_ref[...], preferred_element_type=F32)
    out_ref[...] = _gated_residual(h_ref[...], mod_ref[2], mix)


def _oproj(o, perm_t, w_o, h, mod, hrow, sel):
    nb, l, wid = o.shape
    d = h.shape[1]
    steps = TM // NB
    return pl.pallas_call(
        _oproj_kernel,
        grid=(l // steps,),
        in_specs=[pl.BlockSpec((nb, steps, wid), lambda t: (0, t, 0)),
                  pl.BlockSpec(perm_t.shape, lambda t: (0, 0)),
                  pl.BlockSpec(w_o.shape, lambda t: (0, 0)),
                  pl.BlockSpec((TM, d), lambda t: (hrow(t), 0)),
                  _mod_spec(d, sel)],
        out_specs=pl.BlockSpec((TM, d), lambda t: (t, 0)),
        out_shape=jax.ShapeDtypeStruct((nb * l, d), F32),
        compiler_params=_cparams(("arbitrary",)),
        name="mla_oproj",
    )(o, perm_t, w_o, h, mod)


def _route_kernel(h_ref, g_ref, mod_ref, wr_ref, br_ref, xl_ref, route_ref):
    xl = _norm_mod(h_ref[...], g_ref[...], mod_ref[3], mod_ref[4])
    xl_ref[...] = xl
    logits = jnp.dot(xl, wr_ref[...], precision=HIGHEST, preferred_element_type=F32) + br_ref[...]
    lane_i = lax.broadcasted_iota(jnp.int32, logits.shape, 1)
    lane = lane_i.astype(F32)
    lane_grp = (lane_i >> 3).astype(F32)
    neg = -jnp.inf
    big = 1e6
    gl = jnp.where(jnp.logical_and(lane_i >= N_EXPERTS, lane_i < N_EXPERTS + N_GROUPS), logits, neg)
    gmax = jnp.max(gl, axis=-1, keepdims=True)
    gsum = jnp.sum(jnp.exp(gl - gmax), axis=-1, keepdims=True)
    pg = 1.0 / gsum
    gidx = jnp.min(jnp.where(gl == gmax, lane - N_EXPERTS, big), axis=-1, keepdims=True)
    in_grp = jnp.logical_and(lane_i < N_EXPERTS, lane_grp == gidx)
    el = jnp.where(in_grp, logits, neg)
    emax = jnp.max(el, axis=-1, keepdims=True)
    esum = jnp.sum(jnp.exp(el - emax), axis=-1, keepdims=True)
    i1 = jnp.min(jnp.where(el == emax, lane, big), axis=-1, keepdims=True)
    el2 = jnp.where(lane == i1, neg, el)
    emax2 = jnp.max(el2, axis=-1, keepdims=True)
    i2 = jnp.min(jnp.where(el2 == emax2, lane, big), axis=-1, keepdims=True)
    pe1 = 1.0 / esum
    pe2 = jnp.exp(emax2 - emax) / esum
    den = pe1 + pe2
    w1 = pg * pe1 / den
    w2 = pg * pe2 / den
    route_ref[...] = jnp.where(lane_i == 0, i1,
                               jnp.where(lane_i == 1, i2,
                                         jnp.where(lane_i == 2, w1, jnp.where(lane_i == 3, w2, 0.0))))


def _route(h, g, mod, sel, wr, br):
    n, d = h.shape
    return pl.pallas_call(
        _route_kernel,
        grid=(n // TM,),
        in_specs=[pl.BlockSpec((TM, d), lambda t: (t, 0)),
                  pl.BlockSpec((1, d), lambda t: (0, 0)),
                  _mod_spec(d, sel),
                  pl.BlockSpec(wr.shape, lambda t: (0, 0)),
                  pl.BlockSpec(br.shape, lambda t: (0, 0))],
        out_specs=[pl.BlockSpec((TM, d), lambda t: (t, 0)),
                   pl.BlockSpec((TM, 128), lambda t: (t, 0))],
        out_shape=[jax.ShapeDtypeStruct((n, d), F32), jax.ShapeDtypeStruct((n, 128), F32)],
        compiler_params=_cparams(("arbitrary",)),
        name="moe_route",
    )(h, g.reshape(1, d), mod, wr, br)


def _expert_kernel(te_ref, nt_ref, x_ref, wg_ref, wu_ref, wd_ref, o_ref, wg_s, wu_s, wd_s):
    t = pl.program_id(0)

    @pl.when(t < nt_ref[0])
    def _():
        changed = jnp.logical_or(t == 0, te_ref[t] != te_ref[jnp.maximum(t - 1, 0)])

        @pl.when(changed)
        def _():
            wg_s[...] = wg_ref[...].astype(BF16)
            wu_s[...] = wu_ref[...].astype(BF16)
            wd_s[...] = wd_ref[...].astype(BF16)

        x = x_ref[...].astype(BF16)
        h1 = jnp.dot(x, wg_s[...], preferred_element_type=F32)
        h2 = jnp.dot(x, wu_s[...], preferred_element_type=F32)
        hid = ((h1 * jax.nn.sigmoid(h1)) * h2).astype(BF16)
        o_ref[...] = jnp.dot(hid, wd_s[...], preferred_element_type=F32)

    @pl.when(t >= nt_ref[0])
    def _():
        o_ref[...] = jnp.zeros_like(o_ref)


def _experts(tile_e, ntiles, xs, w_gate, w_up, w_down, layer):
    rows, d = xs.shape
    hid = w_gate.shape[-1]
    grid_spec = pltpu.PrefetchScalarGridSpec(
        num_scalar_prefetch=2,
        grid=(rows // TMOE,),
        in_specs=[pl.BlockSpec((TMOE, d), lambda t, te, nt: (t, 0)),
                  pl.BlockSpec((None, None, d, hid), lambda t, te, nt: (layer, te[t], 0, 0)),
                  pl.BlockSpec((None, None, d, hid), lambda t, te, nt: (layer, te[t], 0, 0)),
                  pl.BlockSpec((None, None, hid, d), lambda t, te, nt: (layer, te[t], 0, 0))],
        out_specs=pl.BlockSpec((TMOE, d), lambda t, te, nt: (t, 0)),
        scratch_shapes=[pltpu.VMEM((d, hid), BF16), pltpu.VMEM((d, hid), BF16), pltpu.VMEM((hid, d), BF16)],
    )
    return pl.pallas_call(
        _expert_kernel,
        grid_spec=grid_spec,
        out_shape=jax.ShapeDtypeStruct((rows, d), F32),
        compiler_params=_cparams(("arbitrary",)),
        name="moe_experts",
    )(tile_e, ntiles, xs, w_gate, w_up, w_down)


def _combine_kernel(y1_ref, y2_ref, route_ref, h_ref, mod_ref, fg_ref, out_ref, *, final):
    route = route_ref[...]
    y = route[:, 2:3] * y1_ref[...] + route[:, 3:4] * y2_ref[...]
    hn = _gated_residual(h_ref[...], mod_ref[5], y)
    if final:
        hn = hn * lax.rsqrt(jnp.mean(hn * hn, axis=-1, keepdims=True) + EPS) * fg_ref[...]
    out_ref[...] = hn


def _combine(y1, y2, route, h, mod, sel, fg, final):
    n, d = h.shape
    tok = pl.BlockSpec((TM, d), lambda t: (t, 0))
    return pl.pallas_call(
        functools.partial(_combine_kernel, final=final),
        grid=(n // TM,),
        in_specs=[tok, tok, pl.BlockSpec((TM, 128), lambda t: (t, 0)), tok,
                  _mod_spec(d, sel),
                  pl.BlockSpec((1, d), lambda t: (0, 0))],
        out_specs=tok,
        out_shape=jax.ShapeDtypeStruct((n, d), F32),
        compiler_params=_cparams(("arbitrary",)),
        name="moe_combine",
    )(y1, y2, route, h, mod, fg.reshape(1, d))


def _moe(h, g, mod, sel, wr, br, w_gate, w_up, w_down, layer, fg, final):
    n, d = h.shape
    xl, route = _route(h, g, mod, sel, wr, br)
    e = route[:, :2].astype(jnp.int32).reshape(-1)
    onehot = (e[:, None] == jnp.arange(N_EXPERTS, dtype=jnp.int32)[None, :]).astype(jnp.int32)
    csum = jnp.cumsum(onehot, axis=0)
    counts = csum[-1]
    rank = jnp.sum(csum * onehot, axis=1) - 1
    padded = ((counts + TMOE - 1) // TMOE) * TMOE
    pend = jnp.cumsum(padded)
    pos = (pend - padded)[e] + rank
    rows = 2 * n + N_EXPERTS * TMOE
    src = (jnp.arange(rows, dtype=jnp.int32) % n).at[pos].set(jnp.arange(2 * n, dtype=jnp.int32) // 2,
                                                     mode="promise_in_bounds", unique_indices=True)
    ntile = rows // TMOE
    nvalid = (pend[-1] // TMOE).astype(jnp.int32)
    tstart = jnp.arange(ntile, dtype=jnp.int32) * TMOE
    tile_e = jnp.sum((tstart[:, None] >= pend[None, :]).astype(jnp.int32), axis=1)
    last_e = jnp.sum((((nvalid - 1) * TMOE) >= pend).astype(jnp.int32))
    tile_e = jnp.where(tstart < pend[-1], tile_e, last_e).astype(jnp.int32)
    xs = xl.at[src].get(mode="promise_in_bounds")
    ys = _experts(tile_e, nvalid.reshape(1), xs, w_gate, w_up, w_down, layer)
    pos2 = pos.reshape(n, 2)
    y1 = ys.at[pos2[:, 0]].get(mode="promise_in_bounds", unique_indices=True)
    y2 = ys.at[pos2[:, 1]].get(mode="promise_in_bounds", unique_indices=True)
    return _combine(y1, y2, route, h, mod, sel, fg, final)


def _block_ones(width, seg):
    idx = np.arange(width) // seg
    return jnp.asarray((idx[:, None] == idx[None, :]).astype(np.float32))


def _rope_tables(lc, l):
    rows = l // GRID_W
    row = np.repeat(np.arange(rows, dtype=np.float32), GRID_W)
    col = np.tile(np.arange(GRID_W, dtype=np.float32), rows)
    n_freq = QK_ROPE // 4
    inv_freq = jnp.asarray(ROPE_BASE, F32) ** (-jnp.arange(n_freq, dtype=F32) / n_freq)
    ang_r = jnp.asarray(row)[:, None] * inv_freq
    ang_c = jnp.asarray(col)[:, None] * inv_freq
    cos = jnp.concatenate([jnp.cos(ang_r), jnp.cos(ang_r), jnp.cos(ang_c), jnp.cos(ang_c)], axis=-1)
    sin = jnp.concatenate([jnp.sin(ang_r), jnp.sin(ang_r), jnp.sin(ang_c), jnp.sin(ang_c)], axis=-1)
    cos = jnp.concatenate([jnp.ones((lc, QK_ROPE), F32), cos], axis=0)
    sin = jnp.concatenate([jnp.zeros((lc, QK_ROPE), F32), sin], axis=0)
    return cos, sin


def _batch_major_perm(tm):
    steps = tm // NB
    r_out = np.arange(tm)
    r_in = (r_out % steps) * NB + r_out // steps
    return jnp.asarray((r_in[:, None] == np.arange(tm)[None, :]).astype(np.float32)).astype(BF16)


def _rot_cols(pe):
    q = QK_ROPE // 4
    return jnp.concatenate([-pe[..., q:2 * q], pe[..., 0:q], -pe[..., 3 * q:4 * q], pe[..., 2 * q:3 * q]], axis=-1)


def _pad_cols(w, n):
    return jnp.pad(w, ((0, 0), (0, n - w.shape[1])))


def kernel(x, c, ctx, c_ctx, ada_w, ada_b, norm1_g, norm2_g, final_g, ev_w_in, ev_shift_mu, rwkv_w0, rwkv_w2, rwkv_a0, rwkv_a2, rwkv_g2, rwkv_k_k, rwkv_k_a, rwkv_r_k, rwkv_lnx_w, rwkv_lnx_b, ev_w_out, mla_w_in, mla_q_norm_g, mla_w_uq, mla_kv_norm_g, mla_w_ukv, mla_w_o, moe_w_grp, moe_b_grp, moe_w_exp, moe_b_exp, moe_w_gate, moe_w_up, moe_w_down):
    b, l, d = x.shape
    lc = ctx.shape[1]
    s = lc + l
    n = b * s
    steps = TM // NB
    nct = lc // steps
    assert b == NB and lc % steps == 0 and l % steps == 0 and ada_w.shape[0] == 2

    def sel_all(t):
        return (t >= nct).astype(jnp.int32)

    def sel_lat(t):
        return 1

    c_all = jnp.concatenate([c, c_ctx[None], jnp.zeros((MOD_ROWS - b - 1, d), F32)], axis=0)
    ada = _ada_table(c_all, ada_w, ada_b)
    mod = jnp.stack([jnp.broadcast_to(ada[:, b][:, :, None, :], (2, N_MOD, NB, d)),
                     ada[:, :b].transpose(0, 2, 1, 3)], axis=1)
    h = jnp.concatenate([ctx, x], axis=1).transpose(1, 0, 2).reshape(n, d)

    def router_weights(layer):
        wr = jnp.concatenate([moe_w_exp[layer], moe_w_grp[layer]], axis=1)
        br = jnp.concatenate([moe_b_exp[layer], moe_b_grp[layer]])[None]
        return _pad_cols(wr, 128), _pad_cols(br, 128)

    def two_term(w):
        hi = w.astype(BF16)
        return jnp.stack([hi, (w - hi.astype(F32)).astype(BF16)])

    w_ = RWKV_W
    ret_w = 2 * RET_HEADS * RET_DK + RET_HEADS * RET_DV
    w_in = _pad_cols(ev_w_in[0], 3584).astype(BF16)
    perm = _batch_major_perm(TM)
    perm_t = perm.T
    qkv, gret, rw = _norm_proj(h, norm1_g[0], mod[0], sel_all, w_in, perm,
                               (ret_w, RET_HEADS * RET_DV, 2048), (BF16, F32, F32), 1)
    lg = jnp.log1p(-jnp.exp2(-5.0 - jnp.arange(RET_HEADS, dtype=F32)))
    o_ret = _retention(qkv, jnp.broadcast_to(lg[:, None, None], (RET_HEADS, 1, 128)), lc, l)

    ones8 = _block_ones(w_, RWKV_N).astype(BF16)
    zero = jnp.zeros((64, w_), F32)
    w2bd = jnp.concatenate([jnp.concatenate([rwkv_w2[0, 0], zero], axis=1),
                            jnp.concatenate([zero, rwkv_w2[0, 1]], axis=1)], axis=0)
    a2bd = jnp.concatenate([jnp.concatenate([rwkv_a2[0, 0], zero], axis=1),
                            jnp.concatenate([zero, rwkv_a2[0, 1]], axis=1)], axis=0)
    g2p = jnp.pad(rwkv_g2[0], ((0, 256 - rwkv_g2.shape[1]), (0, 0)))
    mu = _pad_cols(ev_shift_mu[0], 2048)
    feat_params = (mu, rwkv_w0[0].reshape(1, 2 * w_), two_term(w2bd), rwkv_a0[0].reshape(1, 2 * w_),
                   two_term(a2bd), two_term(g2p),
                   rwkv_k_k[0][None], rwkv_k_a[0][None], rwkv_r_k[0].reshape(1, w_), ones8)
    p1, p2f, p2b, p3f, p3b, gate, bonus = _rwkv_features(rw, feat_params, nct)
    y2 = _rwkv_scan(p1, p2f, p2b, p3f, p3b, nct)

    h = _even_out(o_ret, gret, y2, bonus, gate, rwkv_lnx_w[0][None], rwkv_lnx_b[0][None], ones8, perm_t,
                  ev_w_out[0].astype(BF16), h, mod[0], sel_all)
    wr, br = router_weights(0)
    h = _moe(h, norm2_g[0], mod[0], sel_all, wr, br, moe_w_gate, moe_w_up, moe_w_down, 0, final_g, False)

    w_in1 = jnp.concatenate([mla_w_in[0], _rot_cols(mla_w_in[0][:, Q_RANK + KV_RANK:])], axis=1).astype(BF16)
    (a1,) = _norm_proj(h, norm1_g[1], mod[1], sel_all, w_in1, perm, (w_in1.shape[1],), (F32,), 1)
    wq = mla_w_uq[0].reshape(Q_RANK, MLA_HEADS, QK_NOPE + QK_ROPE)
    wq = jnp.concatenate([wq, _rot_cols(wq[..., QK_NOPE:])], axis=-1).reshape(Q_RANK, -1).astype(BF16)
    cos, sin = _rope_tables(lc, l)
    qn, qp, kn, vv, kp = _mla_proj(a1, mla_q_norm_g[0][None], mla_kv_norm_g[0][None], wq,
                                   mla_w_ukv[0].astype(BF16), cos, sin, nct)
    o = _attention(qn, qp, kn, kp, vv)
    h = _oproj(o, perm_t, mla_w_o[0].astype(BF16), h, mod[1], lambda t: t + nct, sel_lat)
    wr, br = router_weights(1)
    h = _moe(h, norm2_g[1], mod[1], sel_lat, wr, br, moe_w_gate, moe_w_up, moe_w_down, 1, final_g, True)
    return h.reshape(l, b, d).transpose(1, 0, 2)
```

```python
import functools

import jax
import jax.numpy as jnp
import numpy as np
from jax import lax
from jax.experimental import pallas as pl
from jax.experimental.pallas import tpu as pltpu

F32 = jnp.float32
BF16 = jnp.bfloat16
HIGHEST = lax.Precision.HIGHEST

NB = 8
TM = 256
RET_TQ = 256
TQ_ATTN = 256
TMOE = 512
EPS = 1e-6
GN_EPS = 64e-5
GRID_W = 64
ROPE_BASE = 10000.0

RET_HEADS, RET_DK, RET_DV = 4, 64, 128
RWKV_HEADS, RWKV_N = 8, 64
RWKV_W = RWKV_HEADS * RWKV_N
MLA_HEADS, Q_RANK, KV_RANK, QK_NOPE, QK_ROPE, V_HEAD = 8, 384, 256, 128, 64, 128
MLA_SCALE = (QK_NOPE + QK_ROPE) ** -0.5
N_GROUPS, EXPERTS_PER_GROUP = 4, 8
N_EXPERTS = N_GROUPS * EXPERTS_PER_GROUP
N_MOD = 6
MOD_ROWS = 16


def _cparams(sem):
    return pltpu.CompilerParams(dimension_semantics=sem)


def _rows8(x, fn):
    tm, d = x.shape
    return fn(x.reshape(tm // NB, NB, d)).reshape(tm, d)


def _norm_mod(x, g, shift8, scale8):
    var = jnp.mean(x * x, axis=-1, keepdims=True)
    y = x * lax.rsqrt(var + EPS) * g
    return _rows8(y, lambda y3: y3 * (1.0 + scale8[None]) + shift8[None])


def _gated_residual(h, gate8, y):
    return h + _rows8(y, lambda y3: y3 * gate8[None])


def _mod_spec(d, sel):
    return pl.BlockSpec((None, N_MOD, NB, d), lambda t: (sel(t), 0, 0, 0))


def _split_bf16(x):
    hi = x.astype(BF16)
    return hi, (x - hi.astype(F32)).astype(BF16)


def _seg_sum(x, ones):
    hi, lo = _split_bf16(x)
    return jnp.dot(hi, ones, preferred_element_type=F32) + jnp.dot(lo, ones, preferred_element_type=F32)


def _seg_sum_left(sel, x):
    hi, lo = _split_bf16(x)
    return jnp.dot(sel, hi, preferred_element_type=F32) + jnp.dot(sel, lo, preferred_element_type=F32)


def _seg_sum_left3(sel, x):
    hi, lo = _split_bf16(x)
    rest = (x - hi.astype(F32)) - lo.astype(F32)
    return (jnp.dot(sel, hi, preferred_element_type=F32) + jnp.dot(sel, lo, preferred_element_type=F32)
            + jnp.dot(sel, rest.astype(BF16), preferred_element_type=F32))


def _dot3(x, w_ref):
    hi, lo = _split_bf16(x)
    w_hi = w_ref[0]
    return (jnp.dot(hi, w_hi, preferred_element_type=F32) + jnp.dot(lo, w_hi, preferred_element_type=F32)
            + jnp.dot(hi, w_ref[1], preferred_element_type=F32))


def _ada_kernel(c_ref, w_ref, b_ref, o_ref):
    s = c_ref[...]
    s = s * jax.nn.sigmoid(s)
    o_ref[...] = jnp.dot(s, w_ref[...], precision=HIGHEST, preferred_element_type=F32) + b_ref[...]


def _ada_table(c_all, ada_w, ada_b):
    depth, d, nd = ada_w.shape
    out = pl.pallas_call(
        _ada_kernel,
        grid=(depth, nd // d),
        in_specs=[pl.BlockSpec((MOD_ROWS, d), lambda l, j: (0, 0)),
                  pl.BlockSpec((None, d, d), lambda l, j: (l, 0, j)),
                  pl.BlockSpec((None, 1, d), lambda l, j: (l, 0, j))],
        out_specs=pl.BlockSpec((None, MOD_ROWS, d), lambda l, j: (l, 0, j)),
        out_shape=jax.ShapeDtypeStruct((depth, MOD_ROWS, nd), F32),
        compiler_params=_cparams(("arbitrary", "arbitrary")),
        name="ada_table",
    )(c_all, ada_w, ada_b.reshape(depth, 1, nd))
    return out.reshape(depth, MOD_ROWS, N_MOD, d)


def _proj_kernel(x_ref, g_ref, mod_ref, w_ref, p_ref, *o_refs, splits, nperm):
    xm = _norm_mod(x_ref[...], g_ref[...], mod_ref[0], mod_ref[1]).astype(BF16)
    tm = xm.shape[0]
    xp = jnp.dot(p_ref[...], xm, preferred_element_type=F32).astype(BF16) if nperm else None
    off = 0
    for idx, (o_ref, n) in enumerate(zip(o_refs, splits)):
        for j in range(0, n, 512):
            c = min(512, n - j)
            w = w_ref[:, off + j:off + j + c]
            if idx < nperm:
                res = jnp.dot(xp, w, preferred_element_type=F32).astype(o_ref.dtype)
                o_ref[:, :, j:j + c] = res.reshape(NB, tm // NB, c)
            else:
                o_ref[:, j:j + c] = jnp.dot(xm, w, preferred_element_type=F32).astype(o_ref.dtype)
        off += n


def _norm_proj(h, g, mod, sel, w_bf16, perm, splits, dtypes, nperm):
    n, d = h.shape
    nout = w_bf16.shape[1]
    steps = TM // NB
    out_specs, out_shape = [], []
    for idx, (s, dt) in enumerate(zip(splits, dtypes)):
        if idx < nperm:
            out_specs.append(pl.BlockSpec((NB, steps, s), lambda t: (0, t, 0)))
            out_shape.append(jax.ShapeDtypeStruct((NB, n // NB, s), dt))
        else:
            out_specs.append(pl.BlockSpec((TM, s), lambda t: (t, 0)))
            out_shape.append(jax.ShapeDtypeStruct((n, s), dt))
    return pl.pallas_call(
        functools.partial(_proj_kernel, splits=splits, nperm=nperm),
        grid=(n // TM,),
        in_specs=[pl.BlockSpec((TM, d), lambda t: (t, 0)),
                  pl.BlockSpec((1, d), lambda t: (0, 0)),
                  _mod_spec(d, sel),
                  pl.BlockSpec((d, nout), lambda t: (0, 0)),
                  pl.BlockSpec(perm.shape, lambda t: (0, 0))],
        out_specs=out_specs,
        out_shape=out_shape,
        compiler_params=_cparams(("arbitrary",)),
        name="norm_proj",
    )(h, g.reshape(1, d), mod, w_bf16, perm)


def _ret_kernel(lg_ref, q_ref, k_ref, v_ref, o_ref, g_ref, *, lc, l, tq):
    nct = lc // tq
    nk = (lc + l) // tq
    dn = (((1,), (1,)), ((), ()))
    scale = RET_DK ** -0.5

    rel = (lax.broadcasted_iota(jnp.int32, (tq, tq), 0)
           - lax.broadcasted_iota(jnp.int32, (tq, tq), 1)).astype(F32)
    for hh in range(2):
        lg = lg_ref[hh][0:1, 0:1]
        g_ref[hh, 0] = jnp.exp(lg * rel)
        g_ref[hh, 1] = jnp.exp(-(lg * rel))
        g_ref[hh, 2] = jnp.exp(lg * jnp.abs(rel))

    k2 = k_ref[...]
    v2 = v_ref[...]

    def tile(qi, carry):
        rows = pl.ds(pl.multiple_of(qi * tq, tq), tq)
        q_lat = qi >= nct
        q2 = q_ref[rows, :]
        for hh in range(2):
            lg = lg_ref[hh][0:1, 0:1]
            s = lax.dot_general(q2[:, hh * RET_DK:(hh + 1) * RET_DK], k2[:, hh * RET_DK:(hh + 1) * RET_DK], dn,
                                preferred_element_type=F32)
            pieces = []
            for kj in range(nk):
                d = qi - kj
                idx = jnp.where(d > 0, 0, jnp.where(d < 0, 1, 2))
                dabs = jnp.full((1, 1), jnp.abs(d) * tq, jnp.int32).astype(F32)
                sig = jnp.exp(lg * dabs) * scale
                if kj < nct:
                    dback = jnp.full((1, 1), l + lc - d * tq, jnp.int32).astype(F32)
                    sig2 = jnp.where(q_lat, jnp.exp(lg * dback) * scale, 0.0)
                    m = g_ref[hh, idx] * sig + g_ref[hh, 1] * sig2
                else:
                    m = g_ref[hh, idx] * jnp.where(q_lat, sig, 0.0)
                pieces.append((s[:, kj * tq:(kj + 1) * tq] * m).astype(BF16))
            p = jnp.concatenate(pieces, axis=1)
            o_ref[rows, hh * RET_DV:(hh + 1) * RET_DV] = jnp.dot(p, v2[:, hh * RET_DV:(hh + 1) * RET_DV],
                                                                preferred_element_type=F32)
        return carry

    lax.fori_loop(0, nk, tile, 0, unroll=3)


def _retention(qkv, lg, lc, l):
    b, s, _ = qkv.shape
    tq = RET_TQ if lc % RET_TQ == 0 else 128
    qk_w = 2 * RET_DK
    v_w = 2 * RET_DV
    k_blk0 = RET_HEADS * RET_DK // qk_w
    v_blk0 = 2 * RET_HEADS * RET_DK // v_w
    return pl.pallas_call(
        functools.partial(_ret_kernel, lc=lc, l=l, tq=tq),
        grid=(b, RET_HEADS // 2),
        in_specs=[pl.BlockSpec((2, 1, 128), lambda bi, hp: (hp, 0, 0)),
                  pl.BlockSpec((None, s, qk_w), lambda bi, hp: (bi, 0, hp)),
                  pl.BlockSpec((None, s, qk_w), lambda bi, hp: (bi, 0, k_blk0 + hp)),
                  pl.BlockSpec((None, s, v_w), lambda bi, hp: (bi, 0, v_blk0 + hp))],
        out_specs=pl.BlockSpec((None, s, v_w), lambda bi, hp: (bi, 0, hp)),
        out_shape=jax.ShapeDtypeStruct((b, s, RET_HEADS * RET_DV), F32),
        scratch_shapes=[pltpu.VMEM((2, 3, tq, tq), F32)],
        compiler_params=_cparams(("arbitrary", "arbitrary")),
        name="retention",
    )(lg, qkv, qkv, qkv)


def _head_pairs(qa, qb):
    tm = qa.shape[0]
    lo = lax.broadcasted_iota(jnp.int32, (tm, 128), 1) < RWKV_N
    out = []
    for c in range(RWKV_W // 128):
        a = qa[:, c * 128:(c + 1) * 128]
        b = qb[:, c * 128:(c + 1) * 128]
        out.append(jnp.where(lo, a, pltpu.roll(b, RWKV_N, 1)))
        out.append(jnp.where(lo, pltpu.roll(a, RWKV_N, 1), b))
    return out


def _feat_kernel(rw_ref, prev_ref, next_ref, mu_ref, w0_ref, w2_ref, a0_ref, a2_ref, g2_ref,
                 kkw_ref, ka_ref, rk_ref, ones_ref,
                 p1_o, p2f_o, p2b_o, p3f_o, p3b_o, gate_o, bonus_o, *, nct, ntile):
    t = pl.program_id(0)
    first = jnp.logical_or(t == 0, t == nct)
    last = jnp.logical_or(t == nct - 1, t == ntile - 1)
    y = rw_ref[...]
    tm, wid = y.shape
    grp = lax.broadcasted_iota(jnp.int32, (tm // NB, 1, 1), 0)
    prow = jnp.where(first, 0.0, prev_ref[...])
    nrow = jnp.where(last, 0.0, next_ref[...])
    prev = jnp.where(grp == 0, prow[None], pltpu.roll(y, NB, 0).reshape(tm // NB, NB, wid)).reshape(tm, wid)
    nxt = jnp.where(grp == tm // NB - 1, nrow[None],
                    pltpu.roll(y, tm - NB, 0).reshape(tm // NB, NB, wid)).reshape(tm, wid)
    ys = y + mu_ref[0:1, :] * (prev - y) + mu_ref[1:2, :] * (nxt - y)

    w_ = RWKV_W
    r = ys[:, 0:w_]
    kr = ys[:, w_:2 * w_]
    vr = ys[:, 2 * w_:3 * w_]
    wd = ys[:, 3 * w_:3 * w_ + 128]
    ad = ys[:, 3 * w_ + 128:3 * w_ + 256]
    gd = ys[:, 3 * w_ + 256:3 * w_ + 512]
    ones = ones_ref[...]

    kk = kr * kkw_ref[...]
    ss = _seg_sum(kk * kk, ones)
    kk = kk / jnp.maximum(jnp.sqrt(ss), 1e-12)
    zw = w0_ref[...] + _dot3(jnp.tanh(wd), w2_ref)
    wlog = -(jnp.maximum(-zw, 0.0) + jnp.log1p(jnp.exp(-jnp.abs(zw)))) - 0.5
    decay = jnp.exp(-jnp.exp(wlog))
    a = jax.nn.sigmoid(a0_ref[...] + _dot3(ad, a2_ref))
    gate_o[...] = _dot3(jax.nn.sigmoid(gd), g2_ref)
    ka = ka_ref[...]

    def emit(o_ref, qa, qb):
        for hh, slab in enumerate(_head_pairs(qa, qb)):
            o_ref[hh] = slab

    emit(p1_o, kk, r)
    ktsum = None
    for d, (p2_o, p3_o) in enumerate(((p2f_o, p3f_o), (p2b_o, p3b_o))):
        a_d = a[:, d * w_:(d + 1) * w_]
        kt = kr * (1.0 + (a_d - 1.0) * ka)
        emit(p2_o, decay[:, d * w_:(d + 1) * w_], kk * a_d)
        emit(p3_o, kt, vr)
        ktsum = kt if ktsum is None else ktsum + kt
    bonus_o[...] = _seg_sum(r * ktsum * rk_ref[...], ones) * vr


def _rwkv_features(rw, params, nct):
    n, wid = rw.shape
    ntile = n // TM
    rb = TM // 8
    nrb = n // 8
    w_ = RWKV_W
    tok = pl.BlockSpec((TM, w_), lambda t: (t, 0))
    pair = pl.BlockSpec((RWKV_HEADS, TM, 128), lambda t: (0, t, 0))

    def full(a):
        return pl.BlockSpec(a.shape, lambda t: (0,) * a.ndim)

    one = jax.ShapeDtypeStruct((n, w_), F32)
    pshape = jax.ShapeDtypeStruct((RWKV_HEADS, n, 128), F32)
    return pl.pallas_call(
        functools.partial(_feat_kernel, nct=nct, ntile=ntile),
        grid=(ntile,),
        in_specs=[pl.BlockSpec((TM, wid), lambda t: (t, 0)),
                  pl.BlockSpec((8, wid), lambda t: (jnp.maximum(t * rb - 1, 0), 0)),
                  pl.BlockSpec((8, wid), lambda t: (jnp.minimum((t + 1) * rb, nrb - 1), 0))]
                 + [full(a) for a in params],
        out_specs=[pair] * 5 + [tok, tok],
        out_shape=[pshape] * 5 + [one, one],
        compiler_params=_cparams(("arbitrary",)),
        name="rwkv_features",
    )(rw, rw, rw, *params)


def _scan_kernel(p1f, p1b, p2f, p2b, p3f, p3b, yf_ref, yb_ref, s_ref, t_ref, y_buf):
    @pl.when(pl.program_id(0) == 0)
    def _():
        s_ref[...] = jnp.zeros_like(s_ref)
        y_buf[...] = jnp.zeros_like(y_buf)

    nkey = s_ref.shape[0]
    nh = p1f.shape[0]
    tc = p1f.shape[1] // NB
    half = NB * nh
    lanes = 2 * half
    kblk = 8
    pairs = ((p1f, p1b), (p2f, p2b), (p3f, p3b))
    kk_q, r_q, w_q, kka_q, kt_q, v_q = (0, 0), (0, nkey), (1, 0), (1, nkey), (2, 0), (2, nkey)

    def rows_of(i):
        return pl.ds(pl.multiple_of(i * NB, NB), NB)

    def relayout(i, dst):
        for p, (f_ref, b_ref) in enumerate(pairs):
            rows = ([f_ref[hh, rows_of(i), :] for hh in range(nh)]
                    + [b_ref[hh, rows_of(tc - 1 - i), :] for hh in range(nh)])
            dst[p] = jnp.concatenate(rows, axis=0).T

    def emit(i):
        y = y_buf[...].reshape(nkey, lanes)
        yt = jnp.concatenate([y, y], axis=0).T
        for hh in range(nh):
            yf_ref[hh, rows_of(i), :] = yt[hh * NB:(hh + 1) * NB]
            yb_ref[hh, rows_of(tc - 1 - i), :] = yt[half + hh * NB:half + (hh + 1) * NB]

    def step(i, cur, nxt):
        def row(q, k):
            return cur[q[0], pl.ds(q[1] + k, 1), :][None]

        sa = s_ref[0] * row(kk_q, 0)
        for k in range(1, nkey):
            sa = sa + s_ref[k] * row(kk_q, k)
        relayout(jnp.minimum(i + 1, tc - 1), nxt)
        emit(jnp.maximum(i - 1, 0))
        v = cur[v_q[0], v_q[1]:v_q[1] + nkey, :].reshape(nkey // 8, 8, lanes)

        def upd_body(kb, y):
            base = pl.multiple_of(kb * kblk, kblk)
            for j in range(kblk):
                k = base + j
                s_new = s_ref[k] * row(w_q, k) + (v * row(kt_q, k) - sa * row(kka_q, k))
                s_ref[k] = s_new
                y = y + s_new * row(r_q, k)
            return y

        y_buf[...] = lax.fori_loop(0, nkey // kblk, upd_body, jnp.zeros((nkey // 8, 8, lanes), F32))

    relayout(0, t_ref.at[0])

    def two_steps(j, carry):
        step(2 * j, t_ref.at[0], t_ref.at[1])
        step(2 * j + 1, t_ref.at[1], t_ref.at[0])
        return carry

    lax.fori_loop(0, tc // 2, two_steps, 0)
    emit(tc - 1)


def _rwkv_scan(p1, p2f, p2b, p3f, p3b, nct):
    nh, n, _ = p1.shape
    nkey = RWKV_N
    ntb = n // TM

    def mirror(g):
        return jnp.where(g < nct, nct - 1 - g, nct + ntb - 1 - g)

    sf = pl.BlockSpec((nh, TM, 128), lambda g: (0, g, 0))
    sb = pl.BlockSpec((nh, TM, 128), lambda g: (0, mirror(g), 0))
    out = jax.ShapeDtypeStruct((nh, n, 128), F32)
    lanes = 2 * NB * nh
    return pl.pallas_call(
        _scan_kernel,
        grid=(ntb,),
        in_specs=[sf, sb, sf, sb, sf, sb],
        out_specs=[sf, sb],
        out_shape=[out, out],
        scratch_shapes=[pltpu.VMEM((nkey, nkey // 8, 8, lanes), F32),
                        pltpu.VMEM((2, 3, 2 * nkey, lanes), F32),
                        pltpu.VMEM((nkey // 8, 8, lanes), F32)],
        compiler_params=_cparams(("arbitrary",)),
        name="rwkv_scan",
    )(p1, p1, p2f, p2b, p3f, p3b)


def _even_out_kernel(o_ref, g_ref, yf_ref, yb_ref, bonus_ref, gate_ref, lnw_ref, lnb_ref, ones_ref, pt_ref,
                     w_ref, h_ref, mod_ref, out_ref):
    tm = h_ref.shape[0]
    o_all = _seg_sum_left(pt_ref[...], o_ref[...].reshape(tm, RET_HEADS * RET_DV))
    parts = []
    for hh in range(RET_HEADS):
        o = o_all[:, hh * RET_DV:(hh + 1) * RET_DV]
        o = o * lax.rsqrt(jnp.mean(o * o, axis=-1, keepdims=True) + EPS)
        gg = g_ref[:, hh * RET_DV:(hh + 1) * RET_DV]
        parts.append(o * (gg * jax.nn.sigmoid(gg)))
    ones = ones_ref[...]
    lo = lax.broadcasted_iota(jnp.int32, (tm, 128), 1) < RWKV_N
    chunks = []
    for c in range(RWKV_HEADS // 2):
        even = yf_ref[2 * c] + yb_ref[2 * c]
        odd = yf_ref[2 * c + 1] + yb_ref[2 * c + 1]
        chunks.append(jnp.where(lo, even, pltpu.roll(odd, RWKV_N, 1)))
    y = jnp.concatenate(chunks, axis=-1)
    mu = _seg_sum(y, ones) * (1.0 / RWKV_N)
    yc = y - mu
    var = _seg_sum(yc * yc, ones) * (1.0 / RWKV_N)
    yn = yc * lax.rsqrt(var + GN_EPS) * lnw_ref[...] + lnb_ref[...]
    parts.append((yn + bonus_ref[...]) * gate_ref[...])
    cat = jnp.concatenate(parts, axis=-1).astype(BF16)
    mix = jnp.dot(cat, w_ref[...], preferred_element_type=F32)
    out_ref[...] = _gated_residual(h_ref[...], mod_ref[2], mix)


def _even_out(o_ret, g, y2, bonus, gate, lnw, lnb, ones, perm_t, w_out, h, mod, sel):
    n, d = h.shape
    w_ = RWKV_W
    tok = pl.BlockSpec((TM, w_), lambda t: (t, 0))
    ysp = pl.BlockSpec((RWKV_HEADS, TM, 128), lambda t: (0, t, 0))
    osp = pl.BlockSpec((NB, TM // NB, o_ret.shape[2]), lambda t: (0, t, 0))

    def full(a):
        return pl.BlockSpec(a.shape, lambda t: (0,) * a.ndim)

    return pl.pallas_call(
        _even_out_kernel,
        grid=(n // TM,),
        in_specs=[osp, tok, ysp, ysp, tok, tok, full(lnw), full(lnb), full(ones), full(perm_t), full(w_out),
                  pl.BlockSpec((TM, d), lambda t: (t, 0)),
                  _mod_spec(d, sel)],
        out_specs=pl.BlockSpec((TM, d), lambda t: (t, 0)),
        out_shape=jax.ShapeDtypeStruct((n, d), F32),
        compiler_params=_cparams(("arbitrary",)),
        name="even_out",
    )(o_ret, g, y2[0], y2[1], bonus, gate, lnw, lnb, ones, perm_t, w_out, h, mod)


def _mla_proj_kernel(a_ref, qg_ref, kvg_ref, wuq_ref, wukv_ref, cos_ref, sin_ref,
                     qn_o, qp_o, kn_o, v_o, kp_o):
    nb, steps, wid = a_ref.shape
    tm = nb * steps
    a = a_ref[...].reshape(tm, wid)
    cos = jnp.broadcast_to(cos_ref[...][None], (nb, steps, QK_ROPE)).reshape(tm, QK_ROPE)
    sin = jnp.broadcast_to(sin_ref[...][None], (nb, steps, QK_ROPE)).reshape(tm, QK_ROPE)
    zpad = jnp.zeros((tm, 128 - QK_ROPE), F32)

    def rms(t, g):
        return (t * lax.rsqrt(jnp.mean(t * t, axis=-1, keepdims=True) + EPS) * g).astype(BF16)

    def pad128(pe):
        return jnp.concatenate([pe, zpad], axis=1).astype(BF16).reshape(nb, steps, 128)

    cq = rms(a[:, :Q_RANK], qg_ref[...])
    ckv = rms(a[:, Q_RANK:Q_RANK + KV_RANK], kvg_ref[...])
    pe0 = Q_RANK + KV_RANK
    kp_o[...] = pad128(a[:, pe0:pe0 + QK_ROPE] * cos + a[:, pe0 + QK_ROPE:pe0 + 2 * QK_ROPE] * sin)
    hw = QK_NOPE + 2 * QK_ROPE
    for hh in range(MLA_HEADS):
        hs = slice(hh * 128, (hh + 1) * 128)
        qh = jnp.dot(cq, wuq_ref[:, hh * hw:(hh + 1) * hw], preferred_element_type=F32)
        qn_o[:, :, hs] = (qh[:, :QK_NOPE] * MLA_SCALE).astype(BF16).reshape(nb, steps, 128)
        qp_o[:, :, hs] = pad128(
            (qh[:, QK_NOPE:QK_NOPE + QK_ROPE] * cos + qh[:, QK_NOPE + QK_ROPE:] * sin) * MLA_SCALE)
        kvh = jnp.dot(ckv, wukv_ref[:, hh * hw:(hh + 1) * hw], preferred_element_type=F32)
        kn_o[:, :, hs] = kvh[:, :QK_NOPE].astype(BF16).reshape(nb, steps, 128)
        v_o[:, :, hs] = kvh[:, QK_NOPE:].astype(BF16).reshape(nb, steps, 128)


def _mla_proj(a, qg, kvg, wuq, wukv, cos, sin, nct):
    nb, s, wid = a.shape
    hw = MLA_HEADS * 128
    steps = TM // NB
    ntile = s // steps
    l = (ntile - nct) * steps

    def full(x):
        return pl.BlockSpec(x.shape, lambda t: (0,) * x.ndim)

    def lat_blk(t):
        return jnp.maximum(t - nct, 0)

    allrows = pl.BlockSpec((nb, steps, hw), lambda t: (0, t, 0))
    latrows = pl.BlockSpec((nb, steps, hw), lambda t: (0, lat_blk(t), 0))
    rope = pl.BlockSpec((steps, QK_ROPE), lambda t: (t, 0))
    return pl.pallas_call(
        _mla_proj_kernel,
        grid=(ntile,),
        in_specs=[pl.BlockSpec((nb, steps, wid), lambda t: (0, t, 0)), full(qg), full(kvg), full(wuq),
                  full(wukv), rope, rope],
        out_specs=[latrows, latrows, allrows, allrows, pl.BlockSpec((nb, steps, 128), lambda t: (0, t, 0))],
        out_shape=[jax.ShapeDtypeStruct((nb, l, hw), BF16),
                   jax.ShapeDtypeStruct((nb, l, hw), BF16),
                   jax.ShapeDtypeStruct((nb, s, hw), BF16),
                   jax.ShapeDtypeStruct((nb, s, hw), BF16),
                   jax.ShapeDtypeStruct((nb, s, 128), BF16)],
        compiler_params=_cparams(("arbitrary",)),
        name="mla_proj",
    )(a, qg, kvg, wuq, wukv, cos, sin)


def _attn_kernel(qn_ref, qp_ref, kn_ref, kp_ref, v_ref, o_ref, *, tq):
    hi = pl.program_id(1)
    dn = (((1,), (1,)), ((), ()))
    hsl = pl.ds(pl.multiple_of(hi * 128, 128), 128)
    k = jnp.concatenate([kn_ref[:, hsl], kp_ref[...]], axis=1)
    v = v_ref[:, hsl]

    def tile(i, carry):
        rows = pl.ds(pl.multiple_of(i * tq, tq), tq)
        q = jnp.concatenate([qn_ref[rows, :], qp_ref[rows, :]], axis=1)
        s = lax.dot_general(q, k, dn, preferred_element_type=F32)
        m = jnp.max(s, axis=-1, keepdims=True)
        p = jnp.exp(s - m)
        l = jnp.sum(p, axis=-1, keepdims=True)
        o = jnp.dot(p.astype(BF16), v, preferred_element_type=F32)
        o_ref[rows, :] = (o / l).astype(o_ref.dtype)
        return carry

    lax.fori_loop(0, qn_ref.shape[0] // tq, tile, 0, unroll=4)


def _attention(qn, qp, kn, kp, v):
    b, l, hw = qn.shape
    s = kn.shape[1]
    hd = MLA_HEADS
    tq = TQ_ATTN if l % TQ_ATTN == 0 else 128
    qspec = pl.BlockSpec((None, l, 128), lambda bi, hi: (bi, 0, hi))
    kvspec = pl.BlockSpec((None, s, hw), lambda bi, hi: (bi, 0, 0))
    return pl.pallas_call(
        functools.partial(_attn_kernel, tq=tq),
        grid=(b, hd),
        in_specs=[qspec, qspec, kvspec,
                  pl.BlockSpec((None, s, 128), lambda bi, hi: (bi, 0, 0)),
                  kvspec],
        out_specs=qspec,
        out_shape=jax.ShapeDtypeStruct((b, l, hw), BF16),
        compiler_params=_cparams(("arbitrary", "arbitrary")),
        name="mla_attention",
    )(qn, qp, kn, kp, v)


def _oproj_kernel(o_ref, pt_ref, w_ref, h_ref, mod_ref, out_ref):
    nb, steps, wid = o_ref.shape
    o = jnp.dot(pt_ref[...], o_ref[...].reshape(nb * steps, wid), preferred_element_type=F32).astype(BF16)
    mix = jnp.dot(o, w_ref[...], preferred_element_type=F32)
    out_ref[...] = _gated_residual(h_ref[...], mod_ref[2], mix)


def _oproj(o, perm_t, w_o, h, mod, hrow, sel):
    nb, l, wid = o.shape
    d = h.shape[1]
    steps = TM // NB
    return pl.pallas_call(
        _oproj_kernel,
        grid=(l // steps,),
        in_specs=[pl.BlockSpec((nb, steps, wid), lambda t: (0, t, 0)),
                  pl.BlockSpec(perm_t.shape, lambda t: (0, 0)),
                  pl.BlockSpec(w_o.shape, lambda t: (0, 0)),
                  pl.BlockSpec((TM, d), lambda t: (hrow(t), 0)),
                  _mod_spec(d, sel)],
        out_specs=pl.BlockSpec((TM, d), lambda t: (t, 0)),
        out_shape=jax.ShapeDtypeStruct((nb * l, d), F32),
        compiler_params=_cparams(("arbitrary",)),
        name="mla_oproj",
    )(o, perm_t, w_o, h, mod)


def _route_kernel(h_ref, g_ref, mod_ref, wr_ref, br_ref, xl_ref, route_ref):
    xl = _norm_mod(h_ref[...], g_ref[...], mod_ref[3], mod_ref[4])
    xl_ref[...] = xl
    logits = jnp.dot(xl, wr_ref[...], precision=HIGHEST, preferred_element_type=F32) + br_ref[...]
    lane_i = lax.broadcasted_iota(jnp.int32, logits.shape, 1)
    lane = lane_i.astype(F32)
    lane_grp = (lane_i >> 3).astype(F32)
    neg = -jnp.inf
    big = 1e6
    gl = jnp.where(jnp.logical_and(lane_i >= N_EXPERTS, lane_i < N_EXPERTS + N_GROUPS), logits, neg)
    gmax = jnp.max(gl, axis=-1, keepdims=True)
    gsum = jnp.sum(jnp.exp(gl - gmax), axis=-1, keepdims=True)
    pg = 1.0 / gsum
    gidx = jnp.min(jnp.where(gl == gmax, lane - N_EXPERTS, big), axis=-1, keepdims=True)
    in_grp = jnp.logical_and(lane_i < N_EXPERTS, lane_grp == gidx)
    el = jnp.where(in_grp, logits, neg)
    emax = jnp.max(el, axis=-1, keepdims=True)
    esum = jnp.sum(jnp.exp(el - emax), axis=-1, keepdims=True)
    i1 = jnp.min(jnp.where(el == emax, lane, big), axis=-1, keepdims=True)
    el2 = jnp.where(lane == i1, neg, el)
    emax2 = jnp.max(el2, axis=-1, keepdims=True)
    i2 = jnp.min(jnp.where(el2 == emax2, lane, big), axis=-1, keepdims=True)
    pe1 = 1.0 / esum
    pe2 = jnp.exp(emax2 - emax) / esum
    den = pe1 + pe2
    w1 = pg * pe1 / den
    w2 = pg * pe2 / den
    route_ref[...] = jnp.where(lane_i == 0, i1,
                               jnp.where(lane_i == 1, i2,
                                         jnp.where(lane_i == 2, w1, jnp.where(lane_i == 3, w2, 0.0))))


def _route(h, g, mod, sel, wr, br):
    n, d = h.shape
    return pl.pallas_call(
        _route_kernel,
        grid=(n // TM,),
        in_specs=[pl.BlockSpec((TM, d), lambda t: (t, 0)),
                  pl.BlockSpec((1, d), lambda t: (0, 0)),
                  _mod_spec(d, sel),
                  pl.BlockSpec(wr.shape, lambda t: (0, 0)),
                  pl.BlockSpec(br.shape, lambda t: (0, 0))],
        out_specs=[pl.BlockSpec((TM, d), lambda t: (t, 0)),
                   pl.BlockSpec((TM, 128), lambda t: (t, 0))],
        out_shape=[jax.ShapeDtypeStruct((n, d), F32), jax.ShapeDtypeStruct((n, 128), F32)],
        compiler_params=_cparams(("arbitrary",)),
        name="moe_route",
    )(h, g.reshape(1, d), mod, wr, br)


def _expert_kernel(te_ref, nt_ref, x_ref, wg_ref, wu_ref, wd_ref, o_ref, wg_s, wu_s, wd_s):
    t = pl.program_id(0)

    @pl.when(t < nt_ref[0])
    def _():
        changed = jnp.logical_or(t == 0, te_ref[t] != te_ref[jnp.maximum(t - 1, 0)])

        @pl.when(changed)
        def _():
            wg_s[...] = wg_ref[...].astype(BF16)
            wu_s[...] = wu_ref[...].astype(BF16)
            wd_s[...] = wd_ref[...].astype(BF16)

        x = x_ref[...].astype(BF16)
        h1 = jnp.dot(x, wg_s[...], preferred_element_type=F32)
        h2 = jnp.dot(x, wu_s[...], preferred_element_type=F32)
        hid = ((h1 * jax.nn.sigmoid(h1)) * h2).astype(BF16)
        o_ref[...] = jnp.dot(hid, wd_s[...], preferred_element_type=F32)

    @pl.when(t >= nt_ref[0])
    def _():
        o_ref[...] = jnp.zeros_like(o_ref)


def _experts(tile_e, ntiles, xs, w_gate, w_up, w_down, layer):
    rows, d = xs.shape
    hid = w_gate.shape[-1]
    grid_spec = pltpu.PrefetchScalarGridSpec(
        num_scalar_prefetch=2,
        grid=(rows // TMOE,),
        in_specs=[pl.BlockSpec((TMOE, d), lambda t, te, nt: (jnp.minimum(t, nt[0] - 1), 0)),
                  pl.BlockSpec((None, None, d, hid), lambda t, te, nt: (layer, te[t], 0, 0)),
                  pl.BlockSpec((None, None, d, hid), lambda t, te, nt: (layer, te[t], 0, 0)),
                  pl.BlockSpec((None, None, hid, d), lambda t, te, nt: (layer, te[t], 0, 0))],
        out_specs=pl.BlockSpec((TMOE, d), lambda t, te, nt: (t, 0)),
        scratch_shapes=[pltpu.VMEM((d, hid), BF16), pltpu.VMEM((d, hid), BF16), pltpu.VMEM((hid, d), BF16)],
    )
    return pl.pallas_call(
        _expert_kernel,
        grid_spec=grid_spec,
        out_shape=jax.ShapeDtypeStruct((rows, d), F32),
        compiler_params=_cparams(("arbitrary",)),
        name="moe_experts",
    )(tile_e, ntiles, xs, w_gate, w_up, w_down)


def _combine_kernel(y1_ref, y2_ref, route_ref, h_ref, mod_ref, fg_ref, p_ref, out_ref, *, final):
    route = route_ref[...]
    y = route[:, 2:3] * y1_ref[...] + route[:, 3:4] * y2_ref[...]
    hn = _gated_residual(h_ref[...], mod_ref[5], y)
    if final:
        hn = hn * lax.rsqrt(jnp.mean(hn * hn, axis=-1, keepdims=True) + EPS) * fg_ref[...]
        out_ref[...] = _seg_sum_left3(p_ref[...], hn).reshape(out_ref.shape)
    else:
        out_ref[...] = hn


def _combine(y1, y2, route, h, mod, sel, fg, perm, final):
    n, d = h.shape
    tok = pl.BlockSpec((TM, d), lambda t: (t, 0))
    if final:
        out_spec = pl.BlockSpec((NB, TM // NB, d), lambda t: (0, t, 0))
        out_shape = jax.ShapeDtypeStruct((NB, n // NB, d), F32)
    else:
        out_spec, out_shape = tok, jax.ShapeDtypeStruct((n, d), F32)
    return pl.pallas_call(
        functools.partial(_combine_kernel, final=final),
        grid=(n // TM,),
        in_specs=[tok, tok, pl.BlockSpec((TM, 128), lambda t: (t, 0)), tok,
                  _mod_spec(d, sel),
                  pl.BlockSpec((1, d), lambda t: (0, 0)),
                  pl.BlockSpec(perm.shape, lambda t: (0, 0))],
        out_specs=out_spec,
        out_shape=out_shape,
        compiler_params=_cparams(("arbitrary",)),
        name="moe_combine",
    )(y1, y2, route, h, mod, fg.reshape(1, d), perm)


def _moe(h, g, mod, sel, wr, br, w_gate, w_up, w_down, layer, fg, perm, final):
    n, d = h.shape
    xl, route = _route(h, g, mod, sel, wr, br)
    e = route[:, :2].astype(jnp.int32).reshape(-1)
    onehot = (e[:, None] == jnp.arange(N_EXPERTS, dtype=jnp.int32)[None, :]).astype(jnp.int32)
    csum = jnp.cumsum(onehot, axis=0)
    counts = csum[-1]
    rank = jnp.sum(csum * onehot, axis=1) - 1
    padded = ((counts + TMOE - 1) // TMOE) * TMOE
    pend = jnp.cumsum(padded)
    pos = (pend - padded)[e] + rank
    rows = 2 * n + N_EXPERTS * TMOE
    hit = jnp.zeros((rows,), jnp.int32).at[pos].add(jnp.arange(2 * n, dtype=jnp.int32) // 2 + 1,
                                                    mode="promise_in_bounds", unique_indices=True)
    src = jnp.where(hit > 0, hit - 1, jnp.arange(rows, dtype=jnp.int32) % n)
    ntile = rows // TMOE
    nvalid = (pend[-1] // TMOE).astype(jnp.int32)
    tstart = jnp.arange(ntile, dtype=jnp.int32) * TMOE
    tile_e = jnp.sum((tstart[:, None] >= pend[None, :]).astype(jnp.int32), axis=1)
    last_e = jnp.sum((((nvalid - 1) * TMOE) >= pend).astype(jnp.int32))
    tile_e = jnp.where(tstart < pend[-1], tile_e, last_e).astype(jnp.int32)
    xs = xl.at[src].get(mode="promise_in_bounds")
    ys = _experts(tile_e, nvalid.reshape(1), xs, w_gate, w_up, w_down, layer)
    pos2 = pos.reshape(n, 2)
    y1 = ys.at[pos2[:, 0]].get(mode="promise_in_bounds", unique_indices=True)
    y2 = ys.at[pos2[:, 1]].get(mode="promise_in_bounds", unique_indices=True)
    return _combine(y1, y2, route, h, mod, sel, fg, perm, final)


def _block_ones(width, seg):
    idx = np.arange(width) // seg
    return jnp.asarray((idx[:, None] == idx[None, :]).astype(np.float32))


def _rope_tables(lc, l):
    rows = l // GRID_W
    row = np.repeat(np.arange(rows, dtype=np.float32), GRID_W)
    col = np.tile(np.arange(GRID_W, dtype=np.float32), rows)
    n_freq = QK_ROPE // 4
    inv_freq = jnp.asarray(ROPE_BASE, F32) ** (-jnp.arange(n_freq, dtype=F32) / n_freq)
    ang_r = jnp.asarray(row)[:, None] * inv_freq
    ang_c = jnp.asarray(col)[:, None] * inv_freq
    cos = jnp.concatenate([jnp.cos(ang_r), jnp.cos(ang_r), jnp.cos(ang_c), jnp.cos(ang_c)], axis=-1)
    sin = jnp.concatenate([jnp.sin(ang_r), jnp.sin(ang_r), jnp.sin(ang_c), jnp.sin(ang_c)], axis=-1)
    cos = jnp.concatenate([jnp.ones((lc, QK_ROPE), F32), cos], axis=0)
    sin = jnp.concatenate([jnp.zeros((lc, QK_ROPE), F32), sin], axis=0)
    return cos, sin


def _batch_major_perm(tm):
    steps = tm // NB
    r_out = np.arange(tm)
    r_in = (r_out % steps) * NB + r_out // steps
    return jnp.asarray((r_in[:, None] == np.arange(tm)[None, :]).astype(np.float32)).astype(BF16)


def _rot_cols(pe):
    q = QK_ROPE // 4
    return jnp.concatenate([-pe[..., q:2 * q], pe[..., 0:q], -pe[..., 3 * q:4 * q], pe[..., 2 * q:3 * q]], axis=-1)


def _pad_cols(w, n):
    return jnp.pad(w, ((0, 0), (0, n - w.shape[1])))


def kernel(x, c, ctx, c_ctx, ada_w, ada_b, norm1_g, norm2_g, final_g, ev_w_in, ev_shift_mu, rwkv_w0, rwkv_w2, rwkv_a0, rwkv_a2, rwkv_g2, rwkv_k_k, rwkv_k_a, rwkv_r_k, rwkv_lnx_w, rwkv_lnx_b, ev_w_out, mla_w_in, mla_q_norm_g, mla_w_uq, mla_kv_norm_g, mla_w_ukv, mla_w_o, moe_w_grp, moe_b_grp, moe_w_exp, moe_b_exp, moe_w_gate, moe_w_up, moe_w_down):
    b, l, d = x.shape
    lc = ctx.shape[1]
    s = lc + l
    n = b * s
    steps = TM // NB
    nct = lc // steps
    assert b == NB and lc % steps == 0 and l % steps == 0 and ada_w.shape[0] == 2

    def sel_all(t):
        return (t >= nct).astype(jnp.int32)

    def sel_lat(t):
        return 1

    c_all = jnp.concatenate([c, c_ctx[None], jnp.zeros((MOD_ROWS - b - 1, d), F32)], axis=0)
    ada = _ada_table(c_all, ada_w, ada_b)
    mod = jnp.stack([jnp.broadcast_to(ada[:, b][:, :, None, :], (2, N_MOD, NB, d)),
                     ada[:, :b].transpose(0, 2, 1, 3)], axis=1)
    h = jnp.concatenate([ctx, x], axis=1).transpose(1, 0, 2).reshape(n, d)

    def router_weights(layer):
        wr = jnp.concatenate([moe_w_exp[layer], moe_w_grp[layer]], axis=1)
        br = jnp.concatenate([moe_b_exp[layer], moe_b_grp[layer]])[None]
        return _pad_cols(wr, 128), _pad_cols(br, 128)

    def two_term(w):
        hi = w.astype(BF16)
        return jnp.stack([hi, (w - hi.astype(F32)).astype(BF16)])

    w_ = RWKV_W
    ret_w = 2 * RET_HEADS * RET_DK + RET_HEADS * RET_DV
    w_in = _pad_cols(ev_w_in[0], 3584).astype(BF16)
    perm = _batch_major_perm(TM)
    perm_t = perm.T
    qkv, gret, rw = _norm_proj(h, norm1_g[0], mod[0], sel_all, w_in, perm,
                               (ret_w, RET_HEADS * RET_DV, 2048), (BF16, F32, F32), 1)
    lg = jnp.log1p(-jnp.exp2(-5.0 - jnp.arange(RET_HEADS, dtype=F32)))
    o_ret = _retention(qkv, jnp.broadcast_to(lg[:, None, None], (RET_HEADS, 1, 128)), lc, l)

    ones8 = _block_ones(w_, RWKV_N).astype(BF16)
    zero = jnp.zeros((64, w_), F32)
    w2bd = jnp.concatenate([jnp.concatenate([rwkv_w2[0, 0], zero], axis=1),
                            jnp.concatenate([zero, rwkv_w2[0, 1]], axis=1)], axis=0)
    a2bd = jnp.concatenate([jnp.concatenate([rwkv_a2[0, 0], zero], axis=1),
                            jnp.concatenate([zero, rwkv_a2[0, 1]], axis=1)], axis=0)
    g2p = jnp.pad(rwkv_g2[0], ((0, 256 - rwkv_g2.shape[1]), (0, 0)))
    mu = _pad_cols(ev_shift_mu[0], 2048)
    feat_params = (mu, rwkv_w0[0].reshape(1, 2 * w_), two_term(w2bd), rwkv_a0[0].reshape(1, 2 * w_),
                   two_term(a2bd), two_term(g2p),
                   rwkv_k_k[0][None], rwkv_k_a[0][None], rwkv_r_k[0].reshape(1, w_), ones8)
    p1, p2f, p2b, p3f, p3b, gate, bonus = _rwkv_features(rw, feat_params, nct)
    y2 = _rwkv_scan(p1, p2f, p2b, p3f, p3b, nct)

    h = _even_out(o_ret, gret, y2, bonus, gate, rwkv_lnx_w[0][None], rwkv_lnx_b[0][None], ones8, perm_t,
                  ev_w_out[0].astype(BF16), h, mod[0], sel_all)
    wr, br = router_weights(0)
    h = _moe(h, norm2_g[0], mod[0], sel_all, wr, br, moe_w_gate, moe_w_up, moe_w_down, 0, final_g, perm, False)

    w_in1 = jnp.concatenate([mla_w_in[0], _rot_cols(mla_w_in[0][:, Q_RANK + KV_RANK:])], axis=1).astype(BF16)
    (a1,) = _norm_proj(h, norm1_g[1], mod[1], sel_all, w_in1, perm, (w_in1.shape[1],), (F32,), 1)
    wq = mla_w_uq[0].reshape(Q_RANK, MLA_HEADS, QK_NOPE + QK_ROPE)
    wq = jnp.concatenate([wq, _rot_cols(wq[..., QK_NOPE:])], axis=-1).reshape(Q_RANK, -1).astype(BF16)
    cos, sin = _rope_tables(lc, l)
    qn, qp, kn, vv, kp = _mla_proj(a1, mla_q_norm_g[0][None], mla_kv_norm_g[0][None], wq,
                                   mla_w_ukv[0].astype(BF16), cos, sin, nct)
    o = _attention(qn, qp, kn, kp, vv)
    h = _oproj(o, perm_t, mla_w_o[0].astype(BF16), h, mod[1], lambda t: t + nct, sel_lat)
    wr, br = router_weights(1)
    return _moe(h, norm2_g[1], mod[1], sel_lat, wr, br, moe_w_gate, moe_w_up, moe_w_down, 1, final_g, perm, True)
```

```python
import functools

import jax
import jax.numpy as jnp
import numpy as np
from jax import lax
from jax.experimental import pallas as pl
from jax.experimental.pallas import tpu as pltpu

F32 = jnp.float32
BF16 = jnp.bfloat16
HIGHEST = lax.Precision.HIGHEST

NB = 8
TM = 256
RET_TQ = 256
TQ_ATTN = 256
TMOE = 512
EPS = 1e-6
GN_EPS = 64e-5
GRID_W = 64
ROPE_BASE = 10000.0

RET_HEADS, RET_DK, RET_DV = 4, 64, 128
RWKV_HEADS, RWKV_N = 8, 64
RWKV_W = RWKV_HEADS * RWKV_N
MLA_HEADS, Q_RANK, KV_RANK, QK_NOPE, QK_ROPE, V_HEAD = 8, 384, 256, 128, 64, 128
MLA_SCALE = (QK_NOPE + QK_ROPE) ** -0.5
N_GROUPS, EXPERTS_PER_GROUP = 4, 8
N_EXPERTS = N_GROUPS * EXPERTS_PER_GROUP
N_MOD = 6
MOD_ROWS = 16


def _cparams(sem):
    return pltpu.CompilerParams(dimension_semantics=sem)


def _rows8(x, fn):
    tm, d = x.shape
    return fn(x.reshape(tm // NB, NB, d)).reshape(tm, d)


def _norm_mod(x, g, shift8, scale8):
    var = jnp.mean(x * x, axis=-1, keepdims=True)
    y = x * lax.rsqrt(var + EPS) * g
    return _rows8(y, lambda y3: y3 * (1.0 + scale8[None]) + shift8[None])


def _gated_residual(h, gate8, y):
    return h + _rows8(y, lambda y3: y3 * gate8[None])


def _mod_spec(d, sel):
    return pl.BlockSpec((None, N_MOD, NB, d), lambda t: (sel(t), 0, 0, 0))


def _split_bf16(x):
    hi = x.astype(BF16)
    return hi, (x - hi.astype(F32)).astype(BF16)


def _seg_sum(x, ones):
    hi, lo = _split_bf16(x)
    return jnp.dot(hi, ones, preferred_element_type=F32) + jnp.dot(lo, ones, preferred_element_type=F32)


def _seg_sum_left(sel, x):
    hi, lo = _split_bf16(x)
    return jnp.dot(sel, hi, preferred_element_type=F32) + jnp.dot(sel, lo, preferred_element_type=F32)


def _seg_sum_left3(sel, x):
    hi, lo = _split_bf16(x)
    rest = (x - hi.astype(F32)) - lo.astype(F32)
    return (jnp.dot(sel, hi, preferred_element_type=F32) + jnp.dot(sel, lo, preferred_element_type=F32)
            + jnp.dot(sel, rest.astype(BF16), preferred_element_type=F32))


def _dot3(x, w_ref):
    hi, lo = _split_bf16(x)
    w_hi = w_ref[0]
    return (jnp.dot(hi, w_hi, preferred_element_type=F32) + jnp.dot(lo, w_hi, preferred_element_type=F32)
            + jnp.dot(hi, w_ref[1], preferred_element_type=F32))


def _ada_kernel(c_ref, w_ref, b_ref, o_ref):
    s = c_ref[...]
    s = s * jax.nn.sigmoid(s)
    o_ref[...] = jnp.dot(s, w_ref[...], precision=HIGHEST, preferred_element_type=F32) + b_ref[...]


def _ada_table(c_all, ada_w, ada_b):
    depth, d, nd = ada_w.shape
    out = pl.pallas_call(
        _ada_kernel,
        grid=(depth, nd // d),
        in_specs=[pl.BlockSpec((MOD_ROWS, d), lambda l, j: (0, 0)),
                  pl.BlockSpec((None, d, d), lambda l, j: (l, 0, j)),
                  pl.BlockSpec((None, 1, d), lambda l, j: (l, 0, j))],
        out_specs=pl.BlockSpec((None, MOD_ROWS, d), lambda l, j: (l, 0, j)),
        out_shape=jax.ShapeDtypeStruct((depth, MOD_ROWS, nd), F32),
        compiler_params=_cparams(("arbitrary", "arbitrary")),
        name="ada_table",
    )(c_all, ada_w, ada_b.reshape(depth, 1, nd))
    return out.reshape(depth, MOD_ROWS, N_MOD, d)


def _proj_kernel(x_ref, g_ref, mod_ref, w_ref, p_ref, *o_refs, splits, nperm):
    xm = _norm_mod(x_ref[...], g_ref[...], mod_ref[0], mod_ref[1]).astype(BF16)
    tm = xm.shape[0]
    xp = jnp.dot(p_ref[...], xm, preferred_element_type=F32).astype(BF16) if nperm else None
    off = 0
    for idx, (o_ref, n) in enumerate(zip(o_refs, splits)):
        for j in range(0, n, 512):
            c = min(512, n - j)
            w = w_ref[:, off + j:off + j + c]
            if idx < nperm:
                res = jnp.dot(xp, w, preferred_element_type=F32).astype(o_ref.dtype)
                o_ref[:, :, j:j + c] = res.reshape(NB, tm // NB, c)
            else:
                o_ref[:, j:j + c] = jnp.dot(xm, w, preferred_element_type=F32).astype(o_ref.dtype)
        off += n


def _norm_proj(h, g, mod, sel, w_bf16, perm, splits, dtypes, nperm):
    n, d = h.shape
    nout = w_bf16.shape[1]
    steps = TM // NB
    out_specs, out_shape = [], []
    for idx, (s, dt) in enumerate(zip(splits, dtypes)):
        if idx < nperm:
            out_specs.append(pl.BlockSpec((NB, steps, s), lambda t: (0, t, 0)))
            out_shape.append(jax.ShapeDtypeStruct((NB, n // NB, s), dt))
        else:
            out_specs.append(pl.BlockSpec((TM, s), lambda t: (t, 0)))
            out_shape.append(jax.ShapeDtypeStruct((n, s), dt))
    return pl.pallas_call(
        functools.partial(_proj_kernel, splits=splits, nperm=nperm),
        grid=(n // TM,),
        in_specs=[pl.BlockSpec((TM, d), lambda t: (t, 0)),
                  pl.BlockSpec((1, d), lambda t: (0, 0)),
                  _mod_spec(d, sel),
                  pl.BlockSpec((d, nout), lambda t: (0, 0)),
                  pl.BlockSpec(perm.shape, lambda t: (0, 0))],
        out_specs=out_specs,
        out_shape=out_shape,
        compiler_params=_cparams(("arbitrary",)),
        name="norm_proj",
    )(h, g.reshape(1, d), mod, w_bf16, perm)


def _ret_kernel(lg_ref, q_ref, k_ref, v_ref, o_ref, g_ref, *, lc, l, tq):
    nct = lc // tq
    nk = (lc + l) // tq
    dn = (((1,), (1,)), ((), ()))
    scale = RET_DK ** -0.5

    rel = (lax.broadcasted_iota(jnp.int32, (tq, tq), 0)
           - lax.broadcasted_iota(jnp.int32, (tq, tq), 1)).astype(F32)
    for hh in range(2):
        lg = lg_ref[hh][0:1, 0:1]
        g_ref[hh, 0] = jnp.exp(lg * rel)
        g_ref[hh, 1] = jnp.exp(-(lg * rel))
        g_ref[hh, 2] = jnp.exp(lg * jnp.abs(rel))

    k2 = k_ref[...]
    v2 = v_ref[...]

    def tile(qi, carry):
        rows = pl.ds(pl.multiple_of(qi * tq, tq), tq)
        q_lat = qi >= nct
        q2 = q_ref[rows, :]
        for hh in range(2):
            lg = lg_ref[hh][0:1, 0:1]
            s = lax.dot_general(q2[:, hh * RET_DK:(hh + 1) * RET_DK], k2[:, hh * RET_DK:(hh + 1) * RET_DK], dn,
                                preferred_element_type=F32)
            pieces = []
            for kj in range(nk):
                d = qi - kj
                idx = jnp.where(d > 0, 0, jnp.where(d < 0, 1, 2))
                dabs = jnp.full((1, 1), jnp.abs(d) * tq, jnp.int32).astype(F32)
                sig = jnp.exp(lg * dabs) * scale
                if kj < nct:
                    dback = jnp.full((1, 1), l + lc - d * tq, jnp.int32).astype(F32)
                    sig2 = jnp.where(q_lat, jnp.exp(lg * dback) * scale, 0.0)
                    m = g_ref[hh, idx] * sig + g_ref[hh, 1] * sig2
                else:
                    m = g_ref[hh, idx] * jnp.where(q_lat, sig, 0.0)
                pieces.append((s[:, kj * tq:(kj + 1) * tq] * m).astype(BF16))
            p = jnp.concatenate(pieces, axis=1)
            o_ref[rows, hh * RET_DV:(hh + 1) * RET_DV] = jnp.dot(p, v2[:, hh * RET_DV:(hh + 1) * RET_DV],
                                                                preferred_element_type=F32)
        return carry

    lax.fori_loop(0, nk, tile, 0, unroll=3)


def _retention(qkv, lg, lc, l):
    b, s, _ = qkv.shape
    tq = RET_TQ if lc % RET_TQ == 0 else 128
    qk_w = 2 * RET_DK
    v_w = 2 * RET_DV
    k_blk0 = RET_HEADS * RET_DK // qk_w
    v_blk0 = 2 * RET_HEADS * RET_DK // v_w
    return pl.pallas_call(
        functools.partial(_ret_kernel, lc=lc, l=l, tq=tq),
        grid=(b, RET_HEADS // 2),
        in_specs=[pl.BlockSpec((2, 1, 128), lambda bi, hp: (hp, 0, 0)),
                  pl.BlockSpec((None, s, qk_w), lambda bi, hp: (bi, 0, hp)),
                  pl.BlockSpec((None, s, qk_w), lambda bi, hp: (bi, 0, k_blk0 + hp)),
                  pl.BlockSpec((None, s, v_w), lambda bi, hp: (bi, 0, v_blk0 + hp))],
        out_specs=pl.BlockSpec((None, s, v_w), lambda bi, hp: (bi, 0, hp)),
        out_shape=jax.ShapeDtypeStruct((b, s, RET_HEADS * RET_DV), F32),
        scratch_shapes=[pltpu.VMEM((2, 3, tq, tq), F32)],
        compiler_params=_cparams(("arbitrary", "arbitrary")),
        name="retention",
    )(lg, qkv, qkv, qkv)


def _head_pairs(qa, qb):
    tm = qa.shape[0]
    lo = lax.broadcasted_iota(jnp.int32, (tm, 128), 1) < RWKV_N
    out = []
    for c in range(RWKV_W // 128):
        a = qa[:, c * 128:(c + 1) * 128]
        b = qb[:, c * 128:(c + 1) * 128]
        out.append(jnp.where(lo, a, pltpu.roll(b, RWKV_N, 1)))
        out.append(jnp.where(lo, pltpu.roll(a, RWKV_N, 1), b))
    return out


def _feat_kernel(rw_ref, prev_ref, next_ref, mu_ref, w0_ref, w2_ref, a0_ref, a2_ref, g2_ref,
                 kkw_ref, ka_ref, rk_ref, ones_ref,
                 p1_o, p2f_o, p2b_o, p3f_o, p3b_o, gate_o, bonus_o, *, nct, ntile):
    t = pl.program_id(0)
    first = jnp.logical_or(t == 0, t == nct)
    last = jnp.logical_or(t == nct - 1, t == ntile - 1)
    y = rw_ref[...]
    tm, wid = y.shape
    grp = lax.broadcasted_iota(jnp.int32, (tm // NB, 1, 1), 0)
    prow = jnp.where(first, 0.0, prev_ref[...])
    nrow = jnp.where(last, 0.0, next_ref[...])
    prev = jnp.where(grp == 0, prow[None], pltpu.roll(y, NB, 0).reshape(tm // NB, NB, wid)).reshape(tm, wid)
    nxt = jnp.where(grp == tm // NB - 1, nrow[None],
                    pltpu.roll(y, tm - NB, 0).reshape(tm // NB, NB, wid)).reshape(tm, wid)
    ys = y + mu_ref[0:1, :] * (prev - y) + mu_ref[1:2, :] * (nxt - y)

    w_ = RWKV_W
    r = ys[:, 0:w_]
    kr = ys[:, w_:2 * w_]
    vr = ys[:, 2 * w_:3 * w_]
    wd = ys[:, 3 * w_:3 * w_ + 128]
    ad = ys[:, 3 * w_ + 128:3 * w_ + 256]
    gd = ys[:, 3 * w_ + 256:3 * w_ + 512]
    ones = ones_ref[...]

    kk = kr * kkw_ref[...]
    ss = _seg_sum(kk * kk, ones)
    kk = kk / jnp.maximum(jnp.sqrt(ss), 1e-12)
    zw = w0_ref[...] + _dot3(jnp.tanh(wd), w2_ref)
    decay = jnp.exp(-(float(np.exp(-0.5)) * jax.nn.sigmoid(zw)))
    a = jax.nn.sigmoid(a0_ref[...] + _dot3(ad, a2_ref))
    gate_o[...] = _dot3(jax.nn.sigmoid(gd), g2_ref)
    ka = ka_ref[...]

    def emit(o_ref, qa, qb):
        for hh, slab in enumerate(_head_pairs(qa, qb)):
            o_ref[hh] = slab

    emit(p1_o, kk, r)
    ktsum = None
    for d, (p2_o, p3_o) in enumerate(((p2f_o, p3f_o), (p2b_o, p3b_o))):
        a_d = a[:, d * w_:(d + 1) * w_]
        kt = kr * (1.0 + (a_d - 1.0) * ka)
        emit(p2_o, decay[:, d * w_:(d + 1) * w_], kk * a_d)
        emit(p3_o, kt, vr)
        ktsum = kt if ktsum is None else ktsum + kt
    bonus_o[...] = _seg_sum(r * ktsum * rk_ref[...], ones) * vr


def _rwkv_features(rw, params, nct):
    n, wid = rw.shape
    ntile = n // TM
    rb = TM // 8
    nrb = n // 8
    w_ = RWKV_W
    tok = pl.BlockSpec((TM, w_), lambda t: (t, 0))
    pair = pl.BlockSpec((RWKV_HEADS, TM, 128), lambda t: (0, t, 0))

    def full(a):
        return pl.BlockSpec(a.shape, lambda t: (0,) * a.ndim)

    one = jax.ShapeDtypeStruct((n, w_), F32)
    pshape = jax.ShapeDtypeStruct((RWKV_HEADS, n, 128), F32)
    return pl.pallas_call(
        functools.partial(_feat_kernel, nct=nct, ntile=ntile),
        grid=(ntile,),
        in_specs=[pl.BlockSpec((TM, wid), lambda t: (t, 0)),
                  pl.BlockSpec((8, wid), lambda t: (jnp.maximum(t * rb - 1, 0), 0)),
                  pl.BlockSpec((8, wid), lambda t: (jnp.minimum((t + 1) * rb, nrb - 1), 0))]
                 + [full(a) for a in params],
        out_specs=[pair] * 5 + [tok, tok],
        out_shape=[pshape] * 5 + [one, one],
        compiler_params=_cparams(("arbitrary",)),
        name="rwkv_features",
    )(rw, rw, rw, *params)


def _scan_kernel(p1f, p1b, p2f, p2b, p3f, p3b, yf_ref, yb_ref, s_ref, t_ref, y_buf):
    @pl.when(pl.program_id(0) == 0)
    def _():
        s_ref[...] = jnp.zeros_like(s_ref)
        y_buf[...] = jnp.zeros_like(y_buf)

    nkey = s_ref.shape[0]
    nh = p1f.shape[0]
    tc = p1f.shape[1] // NB
    half = NB * nh
    lanes = 2 * half
    kblk = 8
    pairs = ((p1f, p1b), (p2f, p2b), (p3f, p3b))
    kk_q, r_q, w_q, kka_q, kt_q, v_q = (0, 0), (0, nkey), (1, 0), (1, nkey), (2, 0), (2, nkey)

    def rows_of(i):
        return pl.ds(pl.multiple_of(i * NB, NB), NB)

    def relayout(i, dst):
        for p, (f_ref, b_ref) in enumerate(pairs):
            rows = ([f_ref[hh, rows_of(i), :] for hh in range(nh)]
                    + [b_ref[hh, rows_of(tc - 1 - i), :] for hh in range(nh)])
            dst[p] = jnp.concatenate(rows, axis=0).T

    def emit(i):
        y = y_buf[...].reshape(nkey, lanes)
        yt = jnp.concatenate([y, y], axis=0).T
        for hh in range(nh):
            yf_ref[hh, rows_of(i), :] = yt[hh * NB:(hh + 1) * NB]
            yb_ref[hh, rows_of(tc - 1 - i), :] = yt[half + hh * NB:half + (hh + 1) * NB]

    def step(i, cur, nxt):
        def row(q, k):
            return cur[q[0], pl.ds(q[1] + k, 1), :][None]

        sa = s_ref[0] * row(kk_q, 0)
        for k in range(1, nkey):
            sa = sa + s_ref[k] * row(kk_q, k)
        relayout(jnp.minimum(i + 1, tc - 1), nxt)
        emit(jnp.maximum(i - 1, 0))
        v = cur[v_q[0], v_q[1]:v_q[1] + nkey, :].reshape(nkey // 8, 8, lanes)

        def upd_body(kb, y):
            base = pl.multiple_of(kb * kblk, kblk)
            for j in range(kblk):
                k = base + j
                s_new = s_ref[k] * row(w_q, k) + (v * row(kt_q, k) - sa * row(kka_q, k))
                s_ref[k] = s_new
                y = y + s_new * row(r_q, k)
            return y

        y_buf[...] = lax.fori_loop(0, nkey // kblk, upd_body, jnp.zeros((nkey // 8, 8, lanes), F32))

    relayout(0, t_ref.at[0])

    def two_steps(j, carry):
        step(2 * j, t_ref.at[0], t_ref.at[1])
        step(2 * j + 1, t_ref.at[1], t_ref.at[0])
        return carry

    lax.fori_loop(0, tc // 2, two_steps, 0)
    emit(tc - 1)


def _rwkv_scan(p1, p2f, p2b, p3f, p3b, nct):
    nh, n, _ = p1.shape
    nkey = RWKV_N
    ntb = n // TM

    def mirror(g):
        return jnp.where(g < nct, nct - 1 - g, nct + ntb - 1 - g)

    sf = pl.BlockSpec((nh, TM, 128), lambda g: (0, g, 0))
    sb = pl.BlockSpec((nh, TM, 128), lambda g: (0, mirror(g), 0))
    out = jax.ShapeDtypeStruct((nh, n, 128), F32)
    lanes = 2 * NB * nh
    return pl.pallas_call(
        _scan_kernel,
        grid=(ntb,),
        in_specs=[sf, sb, sf, sb, sf, sb],
        out_specs=[sf, sb],
        out_shape=[out, out],
        scratch_shapes=[pltpu.VMEM((nkey, nkey // 8, 8, lanes), F32),
                        pltpu.VMEM((2, 3, 2 * nkey, lanes), F32),
                        pltpu.VMEM((nkey // 8, 8, lanes), F32)],
        compiler_params=_cparams(("arbitrary",)),
        name="rwkv_scan",
    )(p1, p1, p2f, p2b, p3f, p3b)


def _even_out_kernel(o_ref, g_ref, yf_ref, yb_ref, bonus_ref, gate_ref, lnw_ref, lnb_ref, ones_ref, pt_ref,
                     w_ref, h_ref, mod_ref, out_ref):
    tm = h_ref.shape[0]
    o_all = _seg_sum_left(pt_ref[...], o_ref[...].reshape(tm, RET_HEADS * RET_DV))
    parts = []
    for hh in range(RET_HEADS):
        o = o_all[:, hh * RET_DV:(hh + 1) * RET_DV]
        o = o * lax.rsqrt(jnp.mean(o * o, axis=-1, keepdims=True) + EPS)
        gg = g_ref[:, hh * RET_DV:(hh + 1) * RET_DV]
        parts.append(o * (gg * jax.nn.sigmoid(gg)))
    ones = ones_ref[...]
    lo = lax.broadcasted_iota(jnp.int32, (tm, 128), 1) < RWKV_N
    chunks = []
    for c in range(RWKV_HEADS // 2):
        even = yf_ref[2 * c] + yb_ref[2 * c]
        odd = yf_ref[2 * c + 1] + yb_ref[2 * c + 1]
        chunks.append(jnp.where(lo, even, pltpu.roll(odd, RWKV_N, 1)))
    y = jnp.concatenate(chunks, axis=-1)
    mu = _seg_sum(y, ones) * (1.0 / RWKV_N)
    yc = y - mu
    var = _seg_sum(yc * yc, ones) * (1.0 / RWKV_N)
    yn = yc * lax.rsqrt(var + GN_EPS) * lnw_ref[...] + lnb_ref[...]
    parts.append((yn + bonus_ref[...]) * gate_ref[...])
    cat = jnp.concatenate(parts, axis=-1).astype(BF16)
    mix = jnp.dot(cat, w_ref[...], preferred_element_type=F32)
    out_ref[...] = _gated_residual(h_ref[...], mod_ref[2], mix)


def _even_out(o_ret, g, y2, bonus, gate, lnw, lnb, ones, perm_t, w_out, h, mod, sel):
    n, d = h.shape
    w_ = RWKV_W
    tok = pl.BlockSpec((TM, w_), lambda t: (t, 0))
    ysp = pl.BlockSpec((RWKV_HEADS, TM, 128), lambda t: (0, t, 0))
    osp = pl.BlockSpec((NB, TM // NB, o_ret.shape[2]), lambda t: (0, t, 0))

    def full(a):
        return pl.BlockSpec(a.shape, lambda t: (0,) * a.ndim)

    return pl.pallas_call(
        _even_out_kernel,
        grid=(n // TM,),
        in_specs=[osp, tok, ysp, ysp, tok, tok, full(lnw), full(lnb), full(ones), full(perm_t), full(w_out),
                  pl.BlockSpec((TM, d), lambda t: (t, 0)),
                  _mod_spec(d, sel)],
        out_specs=pl.BlockSpec((TM, d), lambda t: (t, 0)),
        out_shape=jax.ShapeDtypeStruct((n, d), F32),
        compiler_params=_cparams(("arbitrary",)),
        name="even_out",
    )(o_ret, g, y2[0], y2[1], bonus, gate, lnw, lnb, ones, perm_t, w_out, h, mod)


def _mla_proj_kernel(a_ref, qg_ref, kvg_ref, wuq_ref, wukv_ref, cos_ref, sin_ref,
                     qn_o, qp_o, kn_o, v_o, kp_o):
    nb, steps, wid = a_ref.shape
    tm = nb * steps
    a = a_ref[...].reshape(tm, wid)
    cos = jnp.broadcast_to(cos_ref[...][None], (nb, steps, QK_ROPE)).reshape(tm, QK_ROPE)
    sin = jnp.broadcast_to(sin_ref[...][None], (nb, steps, QK_ROPE)).reshape(tm, QK_ROPE)
    zpad = jnp.zeros((tm, 128 - QK_ROPE), F32)

    def rms(t, g):
        return (t * lax.rsqrt(jnp.mean(t * t, axis=-1, keepdims=True) + EPS) * g).astype(BF16)

    def pad128(pe):
        return jnp.concatenate([pe, zpad], axis=1).astype(BF16).reshape(nb, steps, 128)

    cq = rms(a[:, :Q_RANK], qg_ref[...])
    ckv = rms(a[:, Q_RANK:Q_RANK + KV_RANK], kvg_ref[...])
    pe0 = Q_RANK + KV_RANK
    kp_o[...] = pad128(a[:, pe0:pe0 + QK_ROPE] * cos + a[:, pe0 + QK_ROPE:pe0 + 2 * QK_ROPE] * sin)
    hw = QK_NOPE + 2 * QK_ROPE
    for hh in range(MLA_HEADS):
        hs = slice(hh * 128, (hh + 1) * 128)
        qh = jnp.dot(cq, wuq_ref[:, hh * hw:(hh + 1) * hw], preferred_element_type=F32)
        qn_o[:, :, hs] = (qh[:, :QK_NOPE] * MLA_SCALE).astype(BF16).reshape(nb, steps, 128)
        qp_o[:, :, hs] = pad128(
            (qh[:, QK_NOPE:QK_NOPE + QK_ROPE] * cos + qh[:, QK_NOPE + QK_ROPE:] * sin) * MLA_SCALE)
        kvh = jnp.dot(ckv, wukv_ref[:, hh * hw:(hh + 1) * hw], preferred_element_type=F32)
        kn_o[:, :, hs] = kvh[:, :QK_NOPE].astype(BF16).reshape(nb, steps, 128)
        v_o[:, :, hs] = kvh[:, QK_NOPE:].astype(BF16).reshape(nb, steps, 128)


def _mla_proj(a, qg, kvg, wuq, wukv, cos, sin, nct):
    nb, s, wid = a.shape
    hw = MLA_HEADS * 128
    steps = TM // NB
    ntile = s // steps
    l = (ntile - nct) * steps

    def full(x):
        return pl.BlockSpec(x.shape, lambda t: (0,) * x.ndim)

    def lat_blk(t):
        return jnp.maximum(t - nct, 0)

    allrows = pl.BlockSpec((nb, steps, hw), lambda t: (0, t, 0))
    latrows = pl.BlockSpec((nb, steps, hw), lambda t: (0, lat_blk(t), 0))
    rope = pl.BlockSpec((steps, QK_ROPE), lambda t: (t, 0))
    return pl.pallas_call(
        _mla_proj_kernel,
        grid=(ntile,),
        in_specs=[pl.BlockSpec((nb, steps, wid), lambda t: (0, t, 0)), full(qg), full(kvg), full(wuq),
                  full(wukv), rope, rope],
        out_specs=[latrows, latrows, allrows, allrows, pl.BlockSpec((nb, steps, 128), lambda t: (0, t, 0))],
        out_shape=[jax.ShapeDtypeStruct((nb, l, hw), BF16),
                   jax.ShapeDtypeStruct((nb, l, hw), BF16),
                   jax.ShapeDtypeStruct((nb, s, hw), BF16),
                   jax.ShapeDtypeStruct((nb, s, hw), BF16),
                   jax.ShapeDtypeStruct((nb, s, 128), BF16)],
        compiler_params=_cparams(("arbitrary",)),
        name="mla_proj",
    )(a, qg, kvg, wuq, wukv, cos, sin)


def _attn_kernel(qn_ref, qp_ref, kn_ref, kp_ref, v_ref, o_ref, *, tq):
    hi = pl.program_id(1)
    dn = (((1,), (1,)), ((), ()))
    hsl = pl.ds(pl.multiple_of(hi * 128, 128), 128)
    k = jnp.concatenate([kn_ref[:, hsl], kp_ref[...]], axis=1)
    v = v_ref[:, hsl]

    def tile(i, carry):
        rows = pl.ds(pl.multiple_of(i * tq, tq), tq)
        q = jnp.concatenate([qn_ref[rows, :], qp_ref[rows, :]], axis=1)
        s = lax.dot_general(q, k, dn, preferred_element_type=F32)
        m = jnp.max(s, axis=-1, keepdims=True)
        p = jnp.exp(s - m)
        l = jnp.sum(p, axis=-1, keepdims=True)
        o = jnp.dot(p.astype(BF16), v, preferred_element_type=F32)
        o_ref[rows, :] = (o / l).astype(o_ref.dtype)
        return carry

    lax.fori_loop(0, qn_ref.shape[0] // tq, tile, 0, unroll=8)


def _attention(qn, qp, kn, kp, v):
    b, l, hw = qn.shape
    s = kn.shape[1]
    hd = MLA_HEADS
    tq = TQ_ATTN if l % TQ_ATTN == 0 else 128
    qspec = pl.BlockSpec((None, l, 128), lambda bi, hi: (bi, 0, hi))
    kvspec = pl.BlockSpec((None, s, hw), lambda bi, hi: (bi, 0, 0))
    return pl.pallas_call(
        functools.partial(_attn_kernel, tq=tq),
        grid=(b, hd),
        in_specs=[qspec, qspec, kvspec,
                  pl.BlockSpec((None, s, 128), lambda bi, hi: (bi, 0, 0)),
                  kvspec],
        out_specs=qspec,
        out_shape=jax.ShapeDtypeStruct((b, l, hw), BF16),
        compiler_params=_cparams(("arbitrary", "arbitrary")),
        name="mla_attention",
    )(qn, qp, kn, kp, v)


def _oproj_kernel(o_ref, pt_ref, w_ref, h_ref, mod_ref, out_ref):
    nb, steps, wid = o_ref.shape
    o = jnp.dot(pt_ref[...], o_ref[...].reshape(nb * steps, wid), preferred_element_type=F32).astype(BF16)
    mix = jnp.dot(o, w_ref[...], preferred_element_type=F32)
    out_ref[...] = _gated_residual(h_ref[...], mod_ref[2], mix)


def _oproj(o, perm_t, w_o, h, mod, hrow, sel):
    nb, l, wid = o.shape
    d = h.shape[1]
    steps = TM // NB
    return pl.pallas_call(
        _oproj_kernel,
        grid=(l // steps,),
        in_specs=[pl.BlockSpec((nb, steps, wid), lambda t: (0, t, 0)),
                  pl.BlockSpec(perm_t.shape, lambda t: (0, 0)),
                  pl.BlockSpec(w_o.shape, lambda t: (0, 0)),
                  pl.BlockSpec((TM, d), lambda t: (hrow(t), 0)),
                  _mod_spec(d, sel)],
        out_specs=pl.BlockSpec((TM, d), lambda t: (t, 0)),
        out_shape=jax.ShapeDtypeStruct((nb * l, d), F32),
        compiler_params=_cparams(("arbitrary",)),
        name="mla_oproj",
    )(o, perm_t, w_o, h, mod)


def _route_kernel(h_ref, g_ref, mod_ref, wr_ref, br_ref, xl_ref, route_ref):
    xl = _norm_mod(h_ref[...], g_ref[...], mod_ref[3], mod_ref[4])
    xl_ref[...] = xl
    logits = jnp.dot(xl, wr_ref[...], precision=HIGHEST, preferred_element_type=F32) + br_ref[...]
    lane_i = lax.broadcasted_iota(jnp.int32, logits.shape, 1)
    lane = lane_i.astype(F32)
    lane_grp = (lane_i >> 3).astype(F32)
    neg = -jnp.inf
    big = 1e6
    gl = jnp.where(jnp.logical_and(lane_i >= N_EXPERTS, lane_i < N_EXPERTS + N_GROUPS), logits, neg)
    gmax = jnp.max(gl, axis=-1, keepdims=True)
    gsum = jnp.sum(jnp.exp(gl - gmax), axis=-1, keepdims=True)
    pg = 1.0 / gsum
    gidx = jnp.min(jnp.where(gl == gmax, lane - N_EXPERTS, big), axis=-1, keepdims=True)
    in_grp = jnp.logical_and(lane_i < N_EXPERTS, lane_grp == gidx)
    el = jnp.where(in_grp, logits, neg)
    emax = jnp.max(el, axis=-1, keepdims=True)
    esum = jnp.sum(jnp.exp(el - emax), axis=-1, keepdims=True)
    i1 = jnp.min(jnp.where(el == emax, lane, big), axis=-1, keepdims=True)
    el2 = jnp.where(lane == i1, neg, el)
    emax2 = jnp.max(el2, axis=-1, keepdims=True)
    i2 = jnp.min(jnp.where(el2 == emax2, lane, big), axis=-1, keepdims=True)
    pe1 = 1.0 / esum
    pe2 = jnp.exp(emax2 - emax) / esum
    den = pe1 + pe2
    w1 = pg * pe1 / den
    w2 = pg * pe2 / den
    route_ref[...] = jnp.where(lane_i == 0, i1,
                               jnp.where(lane_i == 1, i2,
                                         jnp.where(lane_i == 2, w1, jnp.where(lane_i == 3, w2, 0.0))))


def _route(h, g, mod, sel, wr, br):
    n, d = h.shape
    return pl.pallas_call(
        _route_kernel,
        grid=(n // TM,),
        in_specs=[pl.BlockSpec((TM, d), lambda t: (t, 0)),
                  pl.BlockSpec((1, d), lambda t: (0, 0)),
                  _mod_spec(d, sel),
                  pl.BlockSpec(wr.shape, lambda t: (0, 0)),
                  pl.BlockSpec(br.shape, lambda t: (0, 0))],
        out_specs=[pl.BlockSpec((TM, d), lambda t: (t, 0)),
                   pl.BlockSpec((TM, 128), lambda t: (t, 0))],
        out_shape=[jax.ShapeDtypeStruct((n, d), F32), jax.ShapeDtypeStruct((n, 128), F32)],
        compiler_params=_cparams(("arbitrary",)),
        name="moe_route",
    )(h, g.reshape(1, d), mod, wr, br)


def _expert_kernel(te_ref, nt_ref, x_ref, wg_ref, wu_ref, wd_ref, o_ref, wg_s, wu_s, wd_s):
    t = pl.program_id(0)

    @pl.when(t < nt_ref[0])
    def _():
        changed = jnp.logical_or(t == 0, te_ref[t] != te_ref[jnp.maximum(t - 1, 0)])

        @pl.when(changed)
        def _():
            wg_s[...] = wg_ref[...].astype(BF16)
            wu_s[...] = wu_ref[...].astype(BF16)
            wd_s[...] = wd_ref[...].astype(BF16)

        x = x_ref[...].astype(BF16)
        h1 = jnp.dot(x, wg_s[...], preferred_element_type=F32)
        h2 = jnp.dot(x, wu_s[...], preferred_element_type=F32)
        hid = ((h1 * jax.nn.sigmoid(h1)) * h2).astype(BF16)
        o_ref[...] = jnp.dot(hid, wd_s[...], preferred_element_type=F32)

    @pl.when(t >= nt_ref[0])
    def _():
        o_ref[...] = jnp.zeros_like(o_ref)


def _experts(tile_e, ntiles, xs, w_gate, w_up, w_down, layer):
    rows, d = xs.shape
    hid = w_gate.shape[-1]
    grid_spec = pltpu.PrefetchScalarGridSpec(
        num_scalar_prefetch=2,
        grid=(rows // TMOE,),
        in_specs=[pl.BlockSpec((TMOE, d), lambda t, te, nt: (jnp.minimum(t, nt[0] - 1), 0)),
                  pl.BlockSpec((None, None, d, hid), lambda t, te, nt: (layer, te[t], 0, 0)),
                  pl.BlockSpec((None, None, d, hid), lambda t, te, nt: (layer, te[t], 0, 0)),
                  pl.BlockSpec((None, None, hid, d), lambda t, te, nt: (layer, te[t], 0, 0))],
        out_specs=pl.BlockSpec((TMOE, d), lambda t, te, nt: (t, 0)),
        scratch_shapes=[pltpu.VMEM((d, hid), BF16), pltpu.VMEM((d, hid), BF16), pltpu.VMEM((hid, d), BF16)],
    )
    return pl.pallas_call(
        _expert_kernel,
        grid_spec=grid_spec,
        out_shape=jax.ShapeDtypeStruct((rows, d), F32),
        compiler_params=_cparams(("arbitrary",)),
        name="moe_experts",
    )(tile_e, ntiles, xs, w_gate, w_up, w_down)


def _combine_kernel(y1_ref, y2_ref, route_ref, h_ref, mod_ref, fg_ref, p_ref, out_ref, *, final):
    route = route_ref[...]
    y = route[:, 2:3] * y1_ref[...] + route[:, 3:4] * y2_ref[...]
    hn = _gated_residual(h_ref[...], mod_ref[5], y)
    if final:
        hn = hn * lax.rsqrt(jnp.mean(hn * hn, axis=-1, keepdims=True) + EPS) * fg_ref[...]
        out_ref[...] = _seg_sum_left3(p_ref[...], hn).reshape(out_ref.shape)
    else:
        out_ref[...] = hn


def _combine(y1, y2, route, h, mod, sel, fg, perm, final):
    n, d = h.shape
    tok = pl.BlockSpec((TM, d), lambda t: (t, 0))
    if final:
        out_spec = pl.BlockSpec((NB, TM // NB, d), lambda t: (0, t, 0))
        out_shape = jax.ShapeDtypeStruct((NB, n // NB, d), F32)
    else:
        out_spec, out_shape = tok, jax.ShapeDtypeStruct((n, d), F32)
    return pl.pallas_call(
        functools.partial(_combine_kernel, final=final),
        grid=(n // TM,),
        in_specs=[tok, tok, pl.BlockSpec((TM, 128), lambda t: (t, 0)), tok,
                  _mod_spec(d, sel),
                  pl.BlockSpec((1, d), lambda t: (0, 0)),
                  pl.BlockSpec(perm.shape, lambda t: (0, 0))],
        out_specs=out_spec,
        out_shape=out_shape,
        compiler_params=_cparams(("arbitrary",)),
        name="moe_combine",
    )(y1, y2, route, h, mod, fg.reshape(1, d), perm)


def _moe(h, g, mod, sel, wr, br, w_gate, w_up, w_down, layer, fg, perm, final):
    n, d = h.shape
    xl, route = _route(h, g, mod, sel, wr, br)
    e = route[:, :2].astype(jnp.int32).reshape(-1)
    onehot = (e[:, None] == jnp.arange(N_EXPERTS, dtype=jnp.int32)[None, :]).astype(jnp.int32)
    csum = jnp.cumsum(onehot, axis=0)
    counts = csum[-1]
    rank = jnp.sum(csum * onehot, axis=1) - 1
    padded = ((counts + TMOE - 1) // TMOE) * TMOE
    pend = jnp.cumsum(padded)
    pos = (pend - padded)[e] + rank
    rows = 2 * n + N_EXPERTS * TMOE
    hit = jnp.zeros((rows,), jnp.int32).at[pos].add(jnp.arange(2 * n, dtype=jnp.int32) // 2 + 1,
                                                    mode="promise_in_bounds", unique_indices=True)
    src = jnp.where(hit > 0, hit - 1, jnp.arange(rows, dtype=jnp.int32) % n)
    ntile = rows // TMOE
    nvalid = (pend[-1] // TMOE).astype(jnp.int32)
    tstart = jnp.arange(ntile, dtype=jnp.int32) * TMOE
    tile_e = jnp.sum((tstart[:, None] >= pend[None, :]).astype(jnp.int32), axis=1)
    last_e = jnp.sum((((nvalid - 1) * TMOE) >= pend).astype(jnp.int32))
    tile_e = jnp.where(tstart < pend[-1], tile_e, last_e).astype(jnp.int32)
    xs = xl.at[src].get(mode="promise_in_bounds")
    ys = _experts(tile_e, nvalid.reshape(1), xs, w_gate, w_up, w_down, layer)
    pos2 = pos.reshape(n, 2)
    y1 = ys.at[pos2[:, 0]].get(mode="promise_in_bounds", unique_indices=True)
    y2 = ys.at[pos2[:, 1]].get(mode="promise_in_bounds", unique_indices=True)
    return _combine(y1, y2, route, h, mod, sel, fg, perm, final)


def _block_ones(width, seg):
    idx = np.arange(width) // seg
    return jnp.asarray((idx[:, None] == idx[None, :]).astype(np.float32))


def _rope_tables(lc, l):
    rows = l // GRID_W
    row = np.repeat(np.arange(rows, dtype=np.float32), GRID_W)
    col = np.tile(np.arange(GRID_W, dtype=np.float32), rows)
    n_freq = QK_ROPE // 4
    inv_freq = jnp.asarray(ROPE_BASE, F32) ** (-jnp.arange(n_freq, dtype=F32) / n_freq)
    ang_r = jnp.asarray(row)[:, None] * inv_freq
    ang_c = jnp.asarray(col)[:, None] * inv_freq
    cos = jnp.concatenate([jnp.cos(ang_r), jnp.cos(ang_r), jnp.cos(ang_c), jnp.cos(ang_c)], axis=-1)
    sin = jnp.concatenate([jnp.sin(ang_r), jnp.sin(ang_r), jnp.sin(ang_c), jnp.sin(ang_c)], axis=-1)
    cos = jnp.concatenate([jnp.ones((lc, QK_ROPE), F32), cos], axis=0)
    sin = jnp.concatenate([jnp.zeros((lc, QK_ROPE), F32), sin], axis=0)
    return cos, sin


def _batch_major_perm(tm):
    steps = tm // NB
    r_out = np.arange(tm)
    r_in = (r_out % steps) * NB + r_out // steps
    return jnp.asarray((r_in[:, None] == np.arange(tm)[None, :]).astype(np.float32)).astype(BF16)


def _rot_cols(pe):
    q = QK_ROPE // 4
    return jnp.concatenate([-pe[..., q:2 * q], pe[..., 0:q], -pe[..., 3 * q:4 * q], pe[..., 2 * q:3 * q]], axis=-1)


def _pad_cols(w, n):
    return jnp.pad(w, ((0, 0), (0, n - w.shape[1])))


def kernel(x, c, ctx, c_ctx, ada_w, ada_b, norm1_g, norm2_g, final_g, ev_w_in, ev_shift_mu, rwkv_w0, rwkv_w2, rwkv_a0, rwkv_a2, rwkv_g2, rwkv_k_k, rwkv_k_a, rwkv_r_k, rwkv_lnx_w, rwkv_lnx_b, ev_w_out, mla_w_in, mla_q_norm_g, mla_w_uq, mla_kv_norm_g, mla_w_ukv, mla_w_o, moe_w_grp, moe_b_grp, moe_w_exp, moe_b_exp, moe_w_gate, moe_w_up, moe_w_down):
    b, l, d = x.shape
    lc = ctx.shape[1]
    s = lc + l
    n = b * s
    steps = TM // NB
    nct = lc // steps
    assert b == NB and lc % steps == 0 and l % steps == 0 and ada_w.shape[0] == 2

    def sel_all(t):
        return (t >= nct).astype(jnp.int32)

    def sel_lat(t):
        return 1

    c_all = jnp.concatenate([c, c_ctx[None], jnp.zeros((MOD_ROWS - b - 1, d), F32)], axis=0)
    ada = _ada_table(c_all, ada_w, ada_b)
    mod = jnp.stack([jnp.broadcast_to(ada[:, b][:, :, None, :], (2, N_MOD, NB, d)),
                     ada[:, :b].transpose(0, 2, 1, 3)], axis=1)
    h = jnp.concatenate([ctx, x], axis=1).transpose(1, 0, 2).reshape(n, d)

    def router_weights(layer):
        wr = jnp.concatenate([moe_w_exp[layer], moe_w_grp[layer]], axis=1)
        br = jnp.concatenate([moe_b_exp[layer], moe_b_grp[layer]])[None]
        return _pad_cols(wr, 128), _pad_cols(br, 128)

    def two_term(w):
        hi = w.astype(BF16)
        return jnp.stack([hi, (w - hi.astype(F32)).astype(BF16)])

    w_ = RWKV_W
    ret_w = 2 * RET_HEADS * RET_DK + RET_HEADS * RET_DV
    w_in = _pad_cols(ev_w_in[0], 3584).astype(BF16)
    perm = _batch_major_perm(TM)
    perm_t = perm.T
    qkv, gret, rw = _norm_proj(h, norm1_g[0], mod[0], sel_all, w_in, perm,
                               (ret_w, RET_HEADS * RET_DV, 2048), (BF16, F32, F32), 1)
    lg = jnp.log1p(-jnp.exp2(-5.0 - jnp.arange(RET_HEADS, dtype=F32)))
    o_ret = _retention(qkv, jnp.broadcast_to(lg[:, None, None], (RET_HEADS, 1, 128)), lc, l)

    ones8 = _block_ones(w_, RWKV_N).astype(BF16)
    zero = jnp.zeros((64, w_), F32)
    w2bd = jnp.concatenate([jnp.concatenate([rwkv_w2[0, 0], zero], axis=1),
                            jnp.concatenate([zero, rwkv_w2[0, 1]], axis=1)], axis=0)
    a2bd = jnp.concatenate([jnp.concatenate([rwkv_a2[0, 0], zero], axis=1),
                            jnp.concatenate([zero, rwkv_a2[0, 1]], axis=1)], axis=0)
    g2p = jnp.pad(rwkv_g2[0], ((0, 256 - rwkv_g2.shape[1]), (0, 0)))
    mu = _pad_cols(ev_shift_mu[0], 2048)
    feat_params = (mu, rwkv_w0[0].reshape(1, 2 * w_), two_term(w2bd), rwkv_a0[0].reshape(1, 2 * w_),
                   two_term(a2bd), two_term(g2p),
                   rwkv_k_k[0][None], rwkv_k_a[0][None], rwkv_r_k[0].reshape(1, w_), ones8)
    p1, p2f, p2b, p3f, p3b, gate, bonus = _rwkv_features(rw, feat_params, nct)
    y2 = _rwkv_scan(p1, p2f, p2b, p3f, p3b, nct)

    h = _even_out(o_ret, gret, y2, bonus, gate, rwkv_lnx_w[0][None], rwkv_lnx_b[0][None], ones8, perm_t,
                  ev_w_out[0].astype(BF16), h, mod[0], sel_all)
    wr, br = router_weights(0)
    h = _moe(h, norm2_g[0], mod[0], sel_all, wr, br, moe_w_gate, moe_w_up, moe_w_down, 0, final_g, perm, False)

    w_in1 = jnp.concatenate([mla_w_in[0], _rot_cols(mla_w_in[0][:, Q_RANK + KV_RANK:])], axis=1).astype(BF16)
    (a1,) = _norm_proj(h, norm1_g[1], mod[1], sel_all, w_in1, perm, (w_in1.shape[1],), (F32,), 1)
    wq = mla_w_uq[0].reshape(Q_RANK, MLA_HEADS, QK_NOPE + QK_ROPE)
    wq = jnp.concatenate([wq, _rot_cols(wq[..., QK_NOPE:])], axis=-1).reshape(Q_RANK, -1).astype(BF16)
    cos, sin = _rope_tables(lc, l)
    qn, qp, kn, vv, kp = _mla_proj(a1, mla_q_norm_g[0][None], mla_kv_norm_g[0][None], wq,
                                   mla_w_ukv[0].astype(BF16), cos, sin, nct)
    o = _attention(qn, qp, kn, kp, vv)
    h = _oproj(o, perm_t, mla_w_o[0].astype(BF16), h, mod[1], lambda t: t + nct, sel_lat)
    wr, br = router_weights(1)
    return _moe(h, norm2_g[1], mod[1], sel_lat, wr, br, moe_w_gate, moe_w_up, moe_w_down, 1, final_g, perm, True)
```

```python
import functools

import jax
import jax.numpy as jnp
import numpy as np
from jax import lax
from jax.experimental import pallas as pl
from jax.experimental.pallas import tpu as pltpu

F32 = jnp.float32
BF16 = jnp.bfloat16
HIGHEST = lax.Precision.HIGHEST

NB = 8
TM = 256
RET_TQ = 256
TQ_ATTN = 256
TMOE = 256
EPS = 1e-6
GN_EPS = 64e-5
GRID_W = 64
ROPE_BASE = 10000.0

RET_HEADS, RET_DK, RET_DV = 4, 64, 128
RWKV_HEADS, RWKV_N = 8, 64
RWKV_W = RWKV_HEADS * RWKV_N
MLA_HEADS, Q_RANK, KV_RANK, QK_NOPE, QK_ROPE, V_HEAD = 8, 384, 256, 128, 64, 128
MLA_SCALE = (QK_NOPE + QK_ROPE) ** -0.5
N_GROUPS, EXPERTS_PER_GROUP = 4, 8
N_EXPERTS = N_GROUPS * EXPERTS_PER_GROUP
N_MOD = 6
MOD_ROWS = 16


def _cparams(sem):
    return pltpu.CompilerParams(dimension_semantics=sem)


def _rows8(x, fn):
    tm, d = x.shape
    return fn(x.reshape(tm // NB, NB, d)).reshape(tm, d)


def _norm_mod(x, g, shift8, scale8):
    var = jnp.mean(x * x, axis=-1, keepdims=True)
    y = x * lax.rsqrt(var + EPS) * g
    return _rows8(y, lambda y3: y3 * (1.0 + scale8[None]) + shift8[None])


def _gated_residual(h, gate8, y):
    return h + _rows8(y, lambda y3: y3 * gate8[None])


def _mod_spec(d, sel):
    return pl.BlockSpec((None, N_MOD, NB, d), lambda t: (sel(t), 0, 0, 0))


def _split_bf16(x):
    hi = x.astype(BF16)
    return hi, (x - hi.astype(F32)).astype(BF16)


def _seg_sum(x, ones):
    hi, lo = _split_bf16(x)
    return jnp.dot(hi, ones, preferred_element_type=F32) + jnp.dot(lo, ones, preferred_element_type=F32)


def _seg_sum_left(sel, x):
    hi, lo = _split_bf16(x)
    return jnp.dot(sel, hi, preferred_element_type=F32) + jnp.dot(sel, lo, preferred_element_type=F32)


def _seg_sum_left3(sel, x):
    hi, lo = _split_bf16(x)
    rest = (x - hi.astype(F32)) - lo.astype(F32)
    return (jnp.dot(sel, hi, preferred_element_type=F32) + jnp.dot(sel, lo, preferred_element_type=F32)
            + jnp.dot(sel, rest.astype(BF16), preferred_element_type=F32))


def _dot3(x, w_ref):
    hi, lo = _split_bf16(x)
    w_hi = w_ref[0]
    return (jnp.dot(hi, w_hi, preferred_element_type=F32) + jnp.dot(lo, w_hi, preferred_element_type=F32)
            + jnp.dot(hi, w_ref[1], preferred_element_type=F32))


def _ada_kernel(c_ref, w_ref, b_ref, o_ref):
    s = c_ref[...]
    s = s * jax.nn.sigmoid(s)
    o_ref[...] = jnp.dot(s, w_ref[...], precision=HIGHEST, preferred_element_type=F32) + b_ref[...]


def _ada_table(c_all, ada_w, ada_b):
    depth, d, nd = ada_w.shape
    out = pl.pallas_call(
        _ada_kernel,
        grid=(depth, nd // d),
        in_specs=[pl.BlockSpec((MOD_ROWS, d), lambda l, j: (0, 0)),
                  pl.BlockSpec((None, d, d), lambda l, j: (l, 0, j)),
                  pl.BlockSpec((None, 1, d), lambda l, j: (l, 0, j))],
        out_specs=pl.BlockSpec((None, MOD_ROWS, d), lambda l, j: (l, 0, j)),
        out_shape=jax.ShapeDtypeStruct((depth, MOD_ROWS, nd), F32),
        compiler_params=_cparams(("arbitrary", "arbitrary")),
        name="ada_table",
    )(c_all, ada_w, ada_b.reshape(depth, 1, nd))
    return out.reshape(depth, MOD_ROWS, N_MOD, d)


def _proj_kernel(x_ref, g_ref, mod_ref, w_ref, p_ref, *o_refs, splits, nperm):
    xm = _norm_mod(x_ref[...], g_ref[...], mod_ref[0], mod_ref[1]).astype(BF16)
    tm = xm.shape[0]
    xp = jnp.dot(p_ref[...], xm, preferred_element_type=F32).astype(BF16) if nperm else None
    off = 0
    for idx, (o_ref, n) in enumerate(zip(o_refs, splits)):
        for j in range(0, n, 512):
            c = min(512, n - j)
            w = w_ref[:, off + j:off + j + c]
            if idx < nperm:
                res = jnp.dot(xp, w, preferred_element_type=F32).astype(o_ref.dtype)
                o_ref[:, :, j:j + c] = res.reshape(NB, tm // NB, c)
            else:
                o_ref[:, j:j + c] = jnp.dot(xm, w, preferred_element_type=F32).astype(o_ref.dtype)
        off += n


def _norm_proj(h, g, mod, sel, w_bf16, perm, splits, dtypes, nperm):
    n, d = h.shape
    nout = w_bf16.shape[1]
    steps = TM // NB
    out_specs, out_shape = [], []
    for idx, (s, dt) in enumerate(zip(splits, dtypes)):
        if idx < nperm:
            out_specs.append(pl.BlockSpec((NB, steps, s), lambda t: (0, t, 0)))
            out_shape.append(jax.ShapeDtypeStruct((NB, n // NB, s), dt))
        else:
            out_specs.append(pl.BlockSpec((TM, s), lambda t: (t, 0)))
            out_shape.append(jax.ShapeDtypeStruct((n, s), dt))
    return pl.pallas_call(
        functools.partial(_proj_kernel, splits=splits, nperm=nperm),
        grid=(n // TM,),
        in_specs=[pl.BlockSpec((TM, d), lambda t: (t, 0)),
                  pl.BlockSpec((1, d), lambda t: (0, 0)),
                  _mod_spec(d, sel),
                  pl.BlockSpec((d, nout), lambda t: (0, 0)),
                  pl.BlockSpec(perm.shape, lambda t: (0, 0))],
        out_specs=out_specs,
        out_shape=out_shape,
        compiler_params=_cparams(("arbitrary",)),
        name="norm_proj",
    )(h, g.reshape(1, d), mod, w_bf16, perm)


def _ret_kernel(lg_ref, q_ref, k_ref, v_ref, o_ref, g_ref, *, lc, l, tq):
    nct = lc // tq
    nk = (lc + l) // tq
    dn = (((1,), (1,)), ((), ()))
    scale = RET_DK ** -0.5

    rel = (lax.broadcasted_iota(jnp.int32, (tq, tq), 0)
           - lax.broadcasted_iota(jnp.int32, (tq, tq), 1)).astype(F32)
    for hh in range(2):
        lg = lg_ref[hh][0:1, 0:1]
        g_ref[hh, 0] = jnp.exp(lg * rel)
        g_ref[hh, 1] = jnp.exp(-(lg * rel))
        g_ref[hh, 2] = jnp.exp(lg * jnp.abs(rel))

    k2 = k_ref[...]
    v2 = v_ref[...]

    def tile(qi, carry):
        rows = pl.ds(pl.multiple_of(qi * tq, tq), tq)
        q_lat = qi >= nct
        q2 = q_ref[rows, :]
        for hh in range(2):
            lg = lg_ref[hh][0:1, 0:1]
            s = lax.dot_general(q2[:, hh * RET_DK:(hh + 1) * RET_DK], k2[:, hh * RET_DK:(hh + 1) * RET_DK], dn,
                                preferred_element_type=F32)
            pieces = []
            for kj in range(nk):
                d = qi - kj
                idx = jnp.where(d > 0, 0, jnp.where(d < 0, 1, 2))
                dabs = jnp.full((1, 1), jnp.abs(d) * tq, jnp.int32).astype(F32)
                sig = jnp.exp(lg * dabs) * scale
                if kj < nct:
                    dback = jnp.full((1, 1), l + lc - d * tq, jnp.int32).astype(F32)
                    sig2 = jnp.where(q_lat, jnp.exp(lg * dback) * scale, 0.0)
                    m = g_ref[hh, idx] * sig + g_ref[hh, 1] * sig2
                else:
                    m = g_ref[hh, idx] * jnp.where(q_lat, sig, 0.0)
                pieces.append((s[:, kj * tq:(kj + 1) * tq] * m).astype(BF16))
            p = jnp.concatenate(pieces, axis=1)
            o_ref[rows, hh * RET_DV:(hh + 1) * RET_DV] = jnp.dot(p, v2[:, hh * RET_DV:(hh + 1) * RET_DV],
                                                                preferred_element_type=F32)
        return carry

    lax.fori_loop(0, nk, tile, 0, unroll=3)


def _retention(qkv, lg, lc, l):
    b, s, _ = qkv.shape
    tq = RET_TQ if lc % RET_TQ == 0 else 128
    qk_w = 2 * RET_DK
    v_w = 2 * RET_DV
    k_blk0 = RET_HEADS * RET_DK // qk_w
    v_blk0 = 2 * RET_HEADS * RET_DK // v_w
    return pl.pallas_call(
        functools.partial(_ret_kernel, lc=lc, l=l, tq=tq),
        grid=(b, RET_HEADS // 2),
        in_specs=[pl.BlockSpec((2, 1, 128), lambda bi, hp: (hp, 0, 0)),
                  pl.BlockSpec((None, s, qk_w), lambda bi, hp: (bi, 0, hp)),
                  pl.BlockSpec((None, s, qk_w), lambda bi, hp: (bi, 0, k_blk0 + hp)),
                  pl.BlockSpec((None, s, v_w), lambda bi, hp: (bi, 0, v_blk0 + hp))],
        out_specs=pl.BlockSpec((None, s, v_w), lambda bi, hp: (bi, 0, hp)),
        out_shape=jax.ShapeDtypeStruct((b, s, RET_HEADS * RET_DV), F32),
        scratch_shapes=[pltpu.VMEM((2, 3, tq, tq), F32)],
        compiler_params=_cparams(("arbitrary", "arbitrary")),
        name="retention",
    )(lg, qkv, qkv, qkv)


def _head_pairs(qa, qb):
    tm = qa.shape[0]
    lo = lax.broadcasted_iota(jnp.int32, (tm, 128), 1) < RWKV_N
    out = []
    for c in range(RWKV_W // 128):
        a = qa[:, c * 128:(c + 1) * 128]
        b = qb[:, c * 128:(c + 1) * 128]
        out.append(jnp.where(lo, a, pltpu.roll(b, RWKV_N, 1)))
        out.append(jnp.where(lo, pltpu.roll(a, RWKV_N, 1), b))
    return out


def _feat_kernel(rw_ref, prev_ref, next_ref, mu_ref, w0_ref, w2_ref, a0_ref, a2_ref, g2_ref,
                 kkw_ref, ka_ref, rk_ref, ones_ref,
                 p1_o, p2f_o, p2b_o, p3f_o, p3b_o, gate_o, bonus_o, *, nct, ntile):
    t = pl.program_id(0)
    first = jnp.logical_or(t == 0, t == nct)
    last = jnp.logical_or(t == nct - 1, t == ntile - 1)
    y = rw_ref[...]
    tm, wid = y.shape
    grp = lax.broadcasted_iota(jnp.int32, (tm // NB, 1, 1), 0)
    prow = jnp.where(first, 0.0, prev_ref[...])
    nrow = jnp.where(last, 0.0, next_ref[...])
    prev = jnp.where(grp == 0, prow[None], pltpu.roll(y, NB, 0).reshape(tm // NB, NB, wid)).reshape(tm, wid)
    nxt = jnp.where(grp == tm // NB - 1, nrow[None],
                    pltpu.roll(y, tm - NB, 0).reshape(tm // NB, NB, wid)).reshape(tm, wid)
    ys = y + mu_ref[0:1, :] * (prev - y) + mu_ref[1:2, :] * (nxt - y)

    w_ = RWKV_W
    r = ys[:, 0:w_]
    kr = ys[:, w_:2 * w_]
    vr = ys[:, 2 * w_:3 * w_]
    wd = ys[:, 3 * w_:3 * w_ + 128]
    ad = ys[:, 3 * w_ + 128:3 * w_ + 256]
    gd = ys[:, 3 * w_ + 256:3 * w_ + 512]
    ones = ones_ref[...]

    kk = kr * kkw_ref[...]
    ss = _seg_sum(kk * kk, ones)
    kk = kk / jnp.maximum(jnp.sqrt(ss), 1e-12)
    zw = w0_ref[...] + _dot3(jnp.tanh(wd), w2_ref)
    decay = jnp.exp(-(float(np.exp(-0.5)) * jax.nn.sigmoid(zw)))
    a = jax.nn.sigmoid(a0_ref[...] + _dot3(ad, a2_ref))
    gate_o[...] = _dot3(jax.nn.sigmoid(gd), g2_ref)
    ka = ka_ref[...]

    def emit(o_ref, qa, qb):
        for hh, slab in enumerate(_head_pairs(qa, qb)):
            o_ref[hh] = slab

    emit(p1_o, kk, r)
    ktsum = None
    for d, (p2_o, p3_o) in enumerate(((p2f_o, p3f_o), (p2b_o, p3b_o))):
        a_d = a[:, d * w_:(d + 1) * w_]
        kt = kr * (1.0 + (a_d - 1.0) * ka)
        emit(p2_o, decay[:, d * w_:(d + 1) * w_], kk * a_d)
        emit(p3_o, kt, vr)
        ktsum = kt if ktsum is None else ktsum + kt
    bonus_o[...] = _seg_sum(r * ktsum * rk_ref[...], ones) * vr


def _rwkv_features(rw, params, nct):
    n, wid = rw.shape
    ntile = n // TM
    rb = TM // 8
    nrb = n // 8
    w_ = RWKV_W
    tok = pl.BlockSpec((TM, w_), lambda t: (t, 0))
    pair = pl.BlockSpec((RWKV_HEADS, TM, 128), lambda t: (0, t, 0))

    def full(a):
        return pl.BlockSpec(a.shape, lambda t: (0,) * a.ndim)

    one = jax.ShapeDtypeStruct((n, w_), F32)
    pshape = jax.ShapeDtypeStruct((RWKV_HEADS, n, 128), F32)
    return pl.pallas_call(
        functools.partial(_feat_kernel, nct=nct, ntile=ntile),
        grid=(ntile,),
        in_specs=[pl.BlockSpec((TM, wid), lambda t: (t, 0)),
                  pl.BlockSpec((8, wid), lambda t: (jnp.maximum(t * rb - 1, 0), 0)),
                  pl.BlockSpec((8, wid), lambda t: (jnp.minimum((t + 1) * rb, nrb - 1), 0))]
                 + [full(a) for a in params],
        out_specs=[pair] * 5 + [tok, tok],
        out_shape=[pshape] * 5 + [one, one],
        compiler_params=_cparams(("arbitrary",)),
        name="rwkv_features",
    )(rw, rw, rw, *params)


def _scan_kernel(p1f, p1b, p2f, p2b, p3f, p3b, yf_ref, yb_ref, s_ref, t_ref, y_buf):
    @pl.when(pl.program_id(0) == 0)
    def _():
        s_ref[...] = jnp.zeros_like(s_ref)
        y_buf[...] = jnp.zeros_like(y_buf)

    nkey = s_ref.shape[0]
    nh = p1f.shape[0]
    tc = p1f.shape[1] // NB
    half = NB * nh
    lanes = 2 * half
    kblk = 16
    pairs = ((p1f, p1b), (p2f, p2b), (p3f, p3b))
    kk_q, r_q, w_q, kka_q, kt_q, v_q = (0, 0), (0, nkey), (1, 0), (1, nkey), (2, 0), (2, nkey)

    def rows_of(i):
        return pl.ds(pl.multiple_of(i * NB, NB), NB)

    def relayout(i, dst):
        for p, (f_ref, b_ref) in enumerate(pairs):
            rows = ([f_ref[hh, rows_of(i), :] for hh in range(nh)]
                    + [b_ref[hh, rows_of(tc - 1 - i), :] for hh in range(nh)])
            dst[p] = jnp.concatenate(rows, axis=0).T

    def emit(i):
        y = y_buf[...].reshape(nkey, lanes)
        yt = jnp.concatenate([y, y], axis=0).T
        for hh in range(nh):
            yf_ref[hh, rows_of(i), :] = yt[hh * NB:(hh + 1) * NB]
            yb_ref[hh, rows_of(tc - 1 - i), :] = yt[half + hh * NB:half + (hh + 1) * NB]

    def step(i, cur, nxt):
        def row(q, k):
            return cur[q[0], pl.ds(q[1] + k, 1), :][None]

        sa = s_ref[0] * row(kk_q, 0)
        for k in range(1, nkey):
            sa = sa + s_ref[k] * row(kk_q, k)
        relayout(jnp.minimum(i + 1, tc - 1), nxt)
        emit(jnp.maximum(i - 1, 0))
        v = cur[v_q[0], v_q[1]:v_q[1] + nkey, :].reshape(nkey // 8, 8, lanes)

        def upd_body(kb, y):
            base = pl.multiple_of(kb * kblk, kblk)
            for j in range(kblk):
                k = base + j
                s_new = s_ref[k] * row(w_q, k) + (v * row(kt_q, k) - sa * row(kka_q, k))
                s_ref[k] = s_new
                y = y + s_new * row(r_q, k)
            return y

        y_buf[...] = lax.fori_loop(0, nkey // kblk, upd_body, jnp.zeros((nkey // 8, 8, lanes), F32))

    relayout(0, t_ref.at[0])

    def two_steps(j, carry):
        step(2 * j, t_ref.at[0], t_ref.at[1])
        step(2 * j + 1, t_ref.at[1], t_ref.at[0])
        return carry

    lax.fori_loop(0, tc // 2, two_steps, 0)
    emit(tc - 1)


def _rwkv_scan(p1, p2f, p2b, p3f, p3b, nct):
    nh, n, _ = p1.shape
    nkey = RWKV_N
    ntb = n // TM

    def mirror(g):
        return jnp.where(g < nct, nct - 1 - g, nct + ntb - 1 - g)

    sf = pl.BlockSpec((nh, TM, 128), lambda g: (0, g, 0))
    sb = pl.BlockSpec((nh, TM, 128), lambda g: (0, mirror(g), 0))
    out = jax.ShapeDtypeStruct((nh, n, 128), F32)
    lanes = 2 * NB * nh
    return pl.pallas_call(
        _scan_kernel,
        grid=(ntb,),
        in_specs=[sf, sb, sf, sb, sf, sb],
        out_specs=[sf, sb],
        out_shape=[out, out],
        scratch_shapes=[pltpu.VMEM((nkey, nkey // 8, 8, lanes), F32),
                        pltpu.VMEM((2, 3, 2 * nkey, lanes), F32),
                        pltpu.VMEM((nkey // 8, 8, lanes), F32)],
        compiler_params=_cparams(("arbitrary",)),
        name="rwkv_scan",
    )(p1, p1, p2f, p2b, p3f, p3b)


def _even_out_kernel(o_ref, g_ref, yf_ref, yb_ref, bonus_ref, gate_ref, lnw_ref, lnb_ref, ones_ref, pt_ref,
                     w_ref, h_ref, mod_ref, out_ref):
    tm = h_ref.shape[0]
    o_all = _seg_sum_left(pt_ref[...], o_ref[...].reshape(tm, RET_HEADS * RET_DV))
    parts = []
    for hh in range(RET_HEADS):
        o = o_all[:, hh * RET_DV:(hh + 1) * RET_DV]
        o = o * lax.rsqrt(jnp.mean(o * o, axis=-1, keepdims=True) + EPS)
        gg = g_ref[:, hh * RET_DV:(hh + 1) * RET_DV]
        parts.append(o * (gg * jax.nn.sigmoid(gg)))
    ones = ones_ref[...]
    lo = lax.broadcasted_iota(jnp.int32, (tm, 128), 1) < RWKV_N
    chunks = []
    for c in range(RWKV_HEADS // 2):
        even = yf_ref[2 * c] + yb_ref[2 * c]
        odd = yf_ref[2 * c + 1] + yb_ref[2 * c + 1]
        chunks.append(jnp.where(lo, even, pltpu.roll(odd, RWKV_N, 1)))
    y = jnp.concatenate(chunks, axis=-1)
    mu = _seg_sum(y, ones) * (1.0 / RWKV_N)
    yc = y - mu
    var = _seg_sum(yc * yc, ones) * (1.0 / RWKV_N)
    yn = yc * lax.rsqrt(var + GN_EPS) * lnw_ref[...] + lnb_ref[...]
    parts.append((yn + bonus_ref[...]) * gate_ref[...])
    cat = jnp.concatenate(parts, axis=-1).astype(BF16)
    mix = jnp.dot(cat, w_ref[...], preferred_element_type=F32)
    out_ref[...] = _gated_residual(h_ref[...], mod_ref[2], mix)


def _even_out(o_ret, g, y2, bonus, gate, lnw, lnb, ones, perm_t, w_out, h, mod, sel):
    n, d = h.shape
    w_ = RWKV_W
    tok = pl.BlockSpec((TM, w_), lambda t: (t, 0))
    ysp = pl.BlockSpec((RWKV_HEADS, TM, 128), lambda t: (0, t, 0))
    osp = pl.BlockSpec((NB, TM // NB, o_ret.shape[2]), lambda t: (0, t, 0))

    def full(a):
        return pl.BlockSpec(a.shape, lambda t: (0,) * a.ndim)

    return pl.pallas_call(
        _even_out_kernel,
        grid=(n // TM,),
        in_specs=[osp, tok, ysp, ysp, tok, tok, full(lnw), full(lnb), full(ones), full(perm_t), full(w_out),
                  pl.BlockSpec((TM, d), lambda t: (t, 0)),
                  _mod_spec(d, sel)],
        out_specs=pl.BlockSpec((TM, d), lambda t: (t, 0)),
        out_shape=jax.ShapeDtypeStruct((n, d), F32),
        compiler_params=_cparams(("arbitrary",)),
        name="even_out",
    )(o_ret, g, y2[0], y2[1], bonus, gate, lnw, lnb, ones, perm_t, w_out, h, mod)


def _mla_proj_kernel(a_ref, qg_ref, kvg_ref, wuq_ref, wukv_ref, cos_ref, sin_ref,
                     qn_o, qp_o, kn_o, v_o, kp_o):
    nb, steps, wid = a_ref.shape
    tm = nb * steps
    a = a_ref[...].reshape(tm, wid)
    cos = jnp.broadcast_to(cos_ref[...][None], (nb, steps, QK_ROPE)).reshape(tm, QK_ROPE)
    sin = jnp.broadcast_to(sin_ref[...][None], (nb, steps, QK_ROPE)).reshape(tm, QK_ROPE)
    zpad = jnp.zeros((tm, 128 - QK_ROPE), F32)

    def rms(t, g):
        return (t * lax.rsqrt(jnp.mean(t * t, axis=-1, keepdims=True) + EPS) * g).astype(BF16)

    def pad128(pe):
        return jnp.concatenate([pe, zpad], axis=1).astype(BF16).reshape(nb, steps, 128)

    cq = rms(a[:, :Q_RANK], qg_ref[...])
    ckv = rms(a[:, Q_RANK:Q_RANK + KV_RANK], kvg_ref[...])
    pe0 = Q_RANK + KV_RANK
    kp_o[...] = pad128(a[:, pe0:pe0 + QK_ROPE] * cos + a[:, pe0 + QK_ROPE:pe0 + 2 * QK_ROPE] * sin)
    hw = QK_NOPE + 2 * QK_ROPE
    for hh in range(MLA_HEADS):
        hs = slice(hh * 128, (hh + 1) * 128)
        qh = jnp.dot(cq, wuq_ref[:, hh * hw:(hh + 1) * hw], preferred_element_type=F32)
        qn_o[:, :, hs] = (qh[:, :QK_NOPE] * MLA_SCALE).astype(BF16).reshape(nb, steps, 128)
        qp_o[:, :, hs] = pad128(
            (qh[:, QK_NOPE:QK_NOPE + QK_ROPE] * cos + qh[:, QK_NOPE + QK_ROPE:] * sin) * MLA_SCALE)
        kvh = jnp.dot(ckv, wukv_ref[:, hh * hw:(hh + 1) * hw], preferred_element_type=F32)
        kn_o[:, :, hs] = kvh[:, :QK_NOPE].astype(BF16).reshape(nb, steps, 128)
        v_o[:, :, hs] = kvh[:, QK_NOPE:].astype(BF16).reshape(nb, steps, 128)


def _mla_proj(a, qg, kvg, wuq, wukv, cos, sin, nct):
    nb, s, wid = a.shape
    hw = MLA_HEADS * 128
    steps = TM // NB
    ntile = s // steps
    l = (ntile - nct) * steps

    def full(x):
        return pl.BlockSpec(x.shape, lambda t: (0,) * x.ndim)

    def lat_blk(t):
        return jnp.maximum(t - nct, 0)

    allrows = pl.BlockSpec((nb, steps, hw), lambda t: (0, t, 0))
    latrows = pl.BlockSpec((nb, steps, hw), lambda t: (0, lat_blk(t), 0))
    rope = pl.BlockSpec((steps, QK_ROPE), lambda t: (t, 0))
    return pl.pallas_call(
        _mla_proj_kernel,
        grid=(ntile,),
        in_specs=[pl.BlockSpec((nb, steps, wid), lambda t: (0, t, 0)), full(qg), full(kvg), full(wuq),
                  full(wukv), rope, rope],
        out_specs=[latrows, latrows, allrows, allrows, pl.BlockSpec((nb, steps, 128), lambda t: (0, t, 0))],
        out_shape=[jax.ShapeDtypeStruct((nb, l, hw), BF16),
                   jax.ShapeDtypeStruct((nb, l, hw), BF16),
                   jax.ShapeDtypeStruct((nb, s, hw), BF16),
                   jax.ShapeDtypeStruct((nb, s, hw), BF16),
                   jax.ShapeDtypeStruct((nb, s, 128), BF16)],
        compiler_params=_cparams(("arbitrary",)),
        name="mla_proj",
    )(a, qg, kvg, wuq, wukv, cos, sin)


def _attn_kernel(qn_ref, qp_ref, kn_ref, kp_ref, v_ref, o_ref, *, tq):
    hi = pl.program_id(1)
    dn = (((1,), (1,)), ((), ()))
    hsl = pl.ds(pl.multiple_of(hi * 128, 128), 128)
    k = jnp.concatenate([kn_ref[:, hsl], kp_ref[...]], axis=1)
    v = v_ref[:, hsl]

    def tile(i, carry):
        rows = pl.ds(pl.multiple_of(i * tq, tq), tq)
        q = jnp.concatenate([qn_ref[rows, :], qp_ref[rows, :]], axis=1)
        s = lax.dot_general(q, k, dn, preferred_element_type=F32)
        m = jnp.max(s, axis=-1, keepdims=True)
        p = jnp.exp(s - m)
        l = jnp.sum(p, axis=-1, keepdims=True)
        o = jnp.dot(p.astype(BF16), v, preferred_element_type=F32)
        o_ref[rows, :] = (o / l).astype(o_ref.dtype)
        return carry

    lax.fori_loop(0, qn_ref.shape[0] // tq, tile, 0, unroll=8)


def _attention(qn, qp, kn, kp, v):
    b, l, hw = qn.shape
    s = kn.shape[1]
    hd = MLA_HEADS
    tq = TQ_ATTN if l % TQ_ATTN == 0 else 128
    qspec = pl.BlockSpec((None, l, 128), lambda bi, hi: (bi, 0, hi))
    kvspec = pl.BlockSpec((None, s, hw), lambda bi, hi: (bi, 0, 0))
    return pl.pallas_call(
        functools.partial(_attn_kernel, tq=tq),
        grid=(b, hd),
        in_specs=[qspec, qspec, kvspec,
                  pl.BlockSpec((None, s, 128), lambda bi, hi: (bi, 0, 0)),
                  kvspec],
        out_specs=qspec,
        out_shape=jax.ShapeDtypeStruct((b, l, hw), BF16),
        compiler_params=_cparams(("arbitrary", "arbitrary")),
        name="mla_attention",
    )(qn, qp, kn, kp, v)


def _oproj_kernel(o_ref, pt_ref, w_ref, h_ref, mod_ref, out_ref):
    nb, steps, wid = o_ref.shape
    o = jnp.dot(pt_ref[...], o_ref[...].reshape(nb * steps, wid), preferred_element_type=F32).astype(BF16)
    mix = jnp.dot(o, w_ref[...], preferred_element_type=F32)
    out_ref[...] = _gated_residual(h_ref[...], mod_ref[2], mix)


def _oproj(o, perm_t, w_o, h, mod, hrow, sel):
    nb, l, wid = o.shape
    d = h.shape[1]
    steps = TM // NB
    return pl.pallas_call(
        _oproj_kernel,
        grid=(l // steps,),
        in_specs=[pl.BlockSpec((nb, steps, wid), lambda t: (0, t, 0)),
                  pl.BlockSpec(perm_t.shape, lambda t: (0, 0)),
                  pl.BlockSpec(w_o.shape, lambda t: (0, 0)),
                  pl.BlockSpec((TM, d), lambda t: (hrow(t), 0)),
                  _mod_spec(d, sel)],
        out_specs=pl.BlockSpec((TM, d), lambda t: (t, 0)),
        out_shape=jax.ShapeDtypeStruct((nb * l, d), F32),
        compiler_params=_cparams(("arbitrary",)),
        name="mla_oproj",
    )(o, perm_t, w_o, h, mod)


def _route_kernel(h_ref, g_ref, mod_ref, wr_ref, br_ref, xl_ref, route_ref):
    xl = _norm_mod(h_ref[...], g_ref[...], mod_ref[3], mod_ref[4])
    xl_ref[...] = xl
    logits = jnp.dot(xl, wr_ref[...], precision=HIGHEST, preferred_element_type=F32) + br_ref[...]
    lane_i = lax.broadcasted_iota(jnp.int32, logits.shape, 1)
    lane = lane_i.astype(F32)
    lane_grp = (lane_i >> 3).astype(F32)
    neg = -jnp.inf
    big = 1e6
    gl = jnp.where(jnp.logical_and(lane_i >= N_EXPERTS, lane_i < N_EXPERTS + N_GROUPS), logits, neg)
    gmax = jnp.max(gl, axis=-1, keepdims=True)
    gsum = jnp.sum(jnp.exp(gl - gmax), axis=-1, keepdims=True)
    pg = 1.0 / gsum
    gidx = jnp.min(jnp.where(gl == gmax, lane - N_EXPERTS, big), axis=-1, keepdims=True)
    in_grp = jnp.logical_and(lane_i < N_EXPERTS, lane_grp == gidx)
    el = jnp.where(in_grp, logits, neg)
    emax = jnp.max(el, axis=-1, keepdims=True)
    esum = jnp.sum(jnp.exp(el - emax), axis=-1, keepdims=True)
    i1 = jnp.min(jnp.where(el == emax, lane, big), axis=-1, keepdims=True)
    el2 = jnp.where(lane == i1, neg, el)
    emax2 = jnp.max(el2, axis=-1, keepdims=True)
    i2 = jnp.min(jnp.where(el2 == emax2, lane, big), axis=-1, keepdims=True)
    pe1 = 1.0 / esum
    pe2 = jnp.exp(emax2 - emax) / esum
    den = pe1 + pe2
    w1 = pg * pe1 / den
    w2 = pg * pe2 / den
    route_ref[...] = jnp.where(lane_i == 0, i1,
                               jnp.where(lane_i == 1, i2,
                                         jnp.where(lane_i == 2, w1, jnp.where(lane_i == 3, w2, 0.0))))


def _route(h, g, mod, sel, wr, br):
    n, d = h.shape
    return pl.pallas_call(
        _route_kernel,
        grid=(n // TM,),
        in_specs=[pl.BlockSpec((TM, d), lambda t: (t, 0)),
                  pl.BlockSpec((1, d), lambda t: (0, 0)),
                  _mod_spec(d, sel),
                  pl.BlockSpec(wr.shape, lambda t: (0, 0)),
                  pl.BlockSpec(br.shape, lambda t: (0, 0))],
        out_specs=[pl.BlockSpec((TM, d), lambda t: (t, 0)),
                   pl.BlockSpec((TM, 128), lambda t: (t, 0))],
        out_shape=[jax.ShapeDtypeStruct((n, d), F32), jax.ShapeDtypeStruct((n, 128), F32)],
        compiler_params=_cparams(("arbitrary",)),
        name="moe_route",
    )(h, g.reshape(1, d), mod, wr, br)


def _expert_kernel(te_ref, nt_ref, x_ref, wg_ref, wu_ref, wd_ref, o_ref, wg_s, wu_s, wd_s):
    t = pl.program_id(0)

    @pl.when(t < nt_ref[0])
    def _():
        changed = jnp.logical_or(t == 0, te_ref[t] != te_ref[jnp.maximum(t - 1, 0)])

        @pl.when(changed)
        def _():
            wg_s[...] = wg_ref[...].astype(BF16)
            wu_s[...] = wu_ref[...].astype(BF16)
            wd_s[...] = wd_ref[...].astype(BF16)

        x = x_ref[...].astype(BF16)
        h1 = jnp.dot(x, wg_s[...], preferred_element_type=F32)
        h2 = jnp.dot(x, wu_s[...], preferred_element_type=F32)
        hid = ((h1 * jax.nn.sigmoid(h1)) * h2).astype(BF16)
        o_ref[...] = jnp.dot(hid, wd_s[...], preferred_element_type=F32)

    @pl.when(t >= nt_ref[0])
    def _():
        o_ref[...] = jnp.zeros_like(o_ref)


def _experts(tile_e, ntiles, xs, w_gate, w_up, w_down, layer):
    rows, d = xs.shape
    hid = w_gate.shape[-1]
    grid_spec = pltpu.PrefetchScalarGridSpec(
        num_scalar_prefetch=2,
        grid=(rows // TMOE,),
        in_specs=[pl.BlockSpec((TMOE, d), lambda t, te, nt: (jnp.minimum(t, nt[0] - 1), 0)),
                  pl.BlockSpec((None, None, d, hid), lambda t, te, nt: (layer, te[t], 0, 0)),
                  pl.BlockSpec((None, None, d, hid), lambda t, te, nt: (layer, te[t], 0, 0)),
                  pl.BlockSpec((None, None, hid, d), lambda t, te, nt: (layer, te[t], 0, 0))],
        out_specs=pl.BlockSpec((TMOE, d), lambda t, te, nt: (t, 0)),
        scratch_shapes=[pltpu.VMEM((d, hid), BF16), pltpu.VMEM((d, hid), BF16), pltpu.VMEM((hid, d), BF16)],
    )
    return pl.pallas_call(
        _expert_kernel,
        grid_spec=grid_spec,
        out_shape=jax.ShapeDtypeStruct((rows, d), F32),
        compiler_params=_cparams(("arbitrary",)),
        name="moe_experts",
    )(tile_e, ntiles, xs, w_gate, w_up, w_down)


def _combine_kernel(y1_ref, y2_ref, route_ref, h_ref, mod_ref, fg_ref, p_ref, out_ref, *, final):
    route = route_ref[...]
    y = route[:, 2:3] * y1_ref[...] + route[:, 3:4] * y2_ref[...]
    hn = _gated_residual(h_ref[...], mod_ref[5], y)
    if final:
        hn = hn * lax.rsqrt(jnp.mean(hn * hn, axis=-1, keepdims=True) + EPS) * fg_ref[...]
        out_ref[...] = _seg_sum_left3(p_ref[...], hn).reshape(out_ref.shape)
    else:
        out_ref[...] = hn


def _combine(y1, y2, route, h, mod, sel, fg, perm, final):
    n, d = h.shape
    tok = pl.BlockSpec((TM, d), lambda t: (t, 0))
    if final:
        out_spec = pl.BlockSpec((NB, TM // NB, d), lambda t: (0, t, 0))
        out_shape = jax.ShapeDtypeStruct((NB, n // NB, d), F32)
    else:
        out_spec, out_shape = tok, jax.ShapeDtypeStruct((n, d), F32)
    return pl.pallas_call(
        functools.partial(_combine_kernel, final=final),
        grid=(n // TM,),
        in_specs=[tok, tok, pl.BlockSpec((TM, 128), lambda t: (t, 0)), tok,
                  _mod_spec(d, sel),
                  pl.BlockSpec((1, d), lambda t: (0, 0)),
                  pl.BlockSpec(perm.shape, lambda t: (0, 0))],
        out_specs=out_spec,
        out_shape=out_shape,
        compiler_params=_cparams(("arbitrary",)),
        name="moe_combine",
    )(y1, y2, route, h, mod, fg.reshape(1, d), perm)


def _moe(h, g, mod, sel, wr, br, w_gate, w_up, w_down, layer, fg, perm, final):
    n, d = h.shape
    xl, route = _route(h, g, mod, sel, wr, br)
    e = route[:, :2].astype(jnp.int32).reshape(-1)
    onehot = (e[:, None] == jnp.arange(N_EXPERTS, dtype=jnp.int32)[None, :]).astype(jnp.int32)
    csum = jnp.cumsum(onehot, axis=0)
    counts = csum[-1]
    rank = jnp.sum(csum * onehot, axis=1) - 1
    padded = ((counts + TMOE - 1) // TMOE) * TMOE
    pend = jnp.cumsum(padded)
    pos = (pend - padded)[e] + rank
    rows = 2 * n + N_EXPERTS * TMOE
    hit = jnp.zeros((rows,), jnp.int32).at[pos].add(jnp.arange(2 * n, dtype=jnp.int32) // 2 + 1,
                                                    mode="promise_in_bounds", unique_indices=True)
    src = jnp.where(hit > 0, hit - 1, jnp.arange(rows, dtype=jnp.int32) % n)
    ntile = rows // TMOE
    nvalid = (pend[-1] // TMOE).astype(jnp.int32)
    tstart = jnp.arange(ntile, dtype=jnp.int32) * TMOE
    tile_e = jnp.sum((tstart[:, None] >= pend[None, :]).astype(jnp.int32), axis=1)
    last_e = jnp.sum((((nvalid - 1) * TMOE) >= pend).astype(jnp.int32))
    tile_e = jnp.where(tstart < pend[-1], tile_e, last_e).astype(jnp.int32)
    xs = xl.at[src].get(mode="promise_in_bounds")
    ys = _experts(tile_e, nvalid.reshape(1), xs, w_gate, w_up, w_down, layer)
    pos2 = pos.reshape(n, 2)
    y1 = ys.at[pos2[:, 0]].get(mode="promise_in_bounds", unique_indices=True)
    y2 = ys.at[pos2[:, 1]].get(mode="promise_in_bounds", unique_indices=True)
    return _combine(y1, y2, route, h, mod, sel, fg, perm, final)


def _block_ones(width, seg):
    idx = np.arange(width) // seg
    return jnp.asarray((idx[:, None] == idx[None, :]).astype(np.float32))


def _rope_tables(lc, l):
    rows = l // GRID_W
    row = np.repeat(np.arange(rows, dtype=np.float32), GRID_W)
    col = np.tile(np.arange(GRID_W, dtype=np.float32), rows)
    n_freq = QK_ROPE // 4
    inv_freq = jnp.asarray(ROPE_BASE, F32) ** (-jnp.arange(n_freq, dtype=F32) / n_freq)
    ang_r = jnp.asarray(row)[:, None] * inv_freq
    ang_c = jnp.asarray(col)[:, None] * inv_freq
    cos = jnp.concatenate([jnp.cos(ang_r), jnp.cos(ang_r), jnp.cos(ang_c), jnp.cos(ang_c)], axis=-1)
    sin = jnp.concatenate([jnp.sin(ang_r), jnp.sin(ang_r), jnp.sin(ang_c), jnp.sin(ang_c)], axis=-1)
    cos = jnp.concatenate([jnp.ones((lc, QK_ROPE), F32), cos], axis=0)
    sin = jnp.concatenate([jnp.zeros((lc, QK_ROPE), F32), sin], axis=0)
    return cos, sin


def _batch_major_perm(tm):
    steps = tm // NB
    r_out = np.arange(tm)
    r_in = (r_out % steps) * NB + r_out // steps
    return jnp.asarray((r_in[:, None] == np.arange(tm)[None, :]).astype(np.float32)).astype(BF16)


def _rot_cols(pe):
    q = QK_ROPE // 4
    return jnp.concatenate([-pe[..., q:2 * q], pe[..., 0:q], -pe[..., 3 * q:4 * q], pe[..., 2 * q:3 * q]], axis=-1)


def _pad_cols(w, n):
    return jnp.pad(w, ((0, 0), (0, n - w.shape[1])))


def kernel(x, c, ctx, c_ctx, ada_w, ada_b, norm1_g, norm2_g, final_g, ev_w_in, ev_shift_mu, rwkv_w0, rwkv_w2, rwkv_a0, rwkv_a2, rwkv_g2, rwkv_k_k, rwkv_k_a, rwkv_r_k, rwkv_lnx_w, rwkv_lnx_b, ev_w_out, mla_w_in, mla_q_norm_g, mla_w_uq, mla_kv_norm_g, mla_w_ukv, mla_w_o, moe_w_grp, moe_b_grp, moe_w_exp, moe_b_exp, moe_w_gate, moe_w_up, moe_w_down):
    b, l, d = x.shape
    lc = ctx.shape[1]
    s = lc + l
    n = b * s
    steps = TM // NB
    nct = lc // steps
    assert b == NB and lc % steps == 0 and l % steps == 0 and ada_w.shape[0] == 2

    def sel_all(t):
        return (t >= nct).astype(jnp.int32)

    def sel_lat(t):
        return 1

    c_all = jnp.concatenate([c, c_ctx[None], jnp.zeros((MOD_ROWS - b - 1, d), F32)], axis=0)
    ada = _ada_table(c_all, ada_w, ada_b)
    mod = jnp.stack([jnp.broadcast_to(ada[:, b][:, :, None, :], (2, N_MOD, NB, d)),
                     ada[:, :b].transpose(0, 2, 1, 3)], axis=1)
    h = jnp.concatenate([ctx, x], axis=1).transpose(1, 0, 2).reshape(n, d)

    def router_weights(layer):
        wr = jnp.concatenate([moe_w_exp[layer], moe_w_grp[layer]], axis=1)
        br = jnp.concatenate([moe_b_exp[layer], moe_b_grp[layer]])[None]
        return _pad_cols(wr, 128), _pad_cols(br, 128)

    def two_term(w):
        hi = w.astype(BF16)
        return jnp.stack([hi, (w - hi.astype(F32)).astype(BF16)])

    w_ = RWKV_W
    ret_w = 2 * RET_HEADS * RET_DK + RET_HEADS * RET_DV
    w_in = _pad_cols(ev_w_in[0], 3584).astype(BF16)
    perm = _batch_major_perm(TM)
    perm_t = perm.T
    qkv, gret, rw = _norm_proj(h, norm1_g[0], mod[0], sel_all, w_in, perm,
                               (ret_w, RET_HEADS * RET_DV, 2048), (BF16, F32, F32), 1)
    lg = jnp.log1p(-jnp.exp2(-5.0 - jnp.arange(RET_HEADS, dtype=F32)))
    o_ret = _retention(qkv, jnp.broadcast_to(lg[:, None, None], (RET_HEADS, 1, 128)), lc, l)

    ones8 = _block_ones(w_, RWKV_N).astype(BF16)
    zero = jnp.zeros((64, w_), F32)
    w2bd = jnp.concatenate([jnp.concatenate([rwkv_w2[0, 0], zero], axis=1),
                            jnp.concatenate([zero, rwkv_w2[0, 1]], axis=1)], axis=0)
    a2bd = jnp.concatenate([jnp.concatenate([rwkv_a2[0, 0], zero], axis=1),
                            jnp.concatenate([zero, rwkv_a2[0, 1]], axis=1)], axis=0)
    g2p = jnp.pad(rwkv_g2[0], ((0, 256 - rwkv_g2.shape[1]), (0, 0)))
    mu = _pad_cols(ev_shift_mu[0], 2048)
    feat_params = (mu, rwkv_w0[0].reshape(1, 2 * w_), two_term(w2bd), rwkv_a0[0].reshape(1, 2 * w_),
                   two_term(a2bd), two_term(g2p),
                   rwkv_k_k[0][None], rwkv_k_a[0][None], rwkv_r_k[0].reshape(1, w_), ones8)
    p1, p2f, p2b, p3f, p3b, gate, bonus = _rwkv_features(rw, feat_params, nct)
    y2 = _rwkv_scan(p1, p2f, p2b, p3f, p3b, nct)

    h = _even_out(o_ret, gret, y2, bonus, gate, rwkv_lnx_w[0][None], rwkv_lnx_b[0][None], ones8, perm_t,
                  ev_w_out[0].astype(BF16), h, mod[0], sel_all)
    wr, br = router_weights(0)
    h = _moe(h, norm2_g[0], mod[0], sel_all, wr, br, moe_w_gate, moe_w_up, moe_w_down, 0, final_g, perm, False)

    w_in1 = jnp.concatenate([mla_w_in[0], _rot_cols(mla_w_in[0][:, Q_RANK + KV_RANK:])], axis=1).astype(BF16)
    (a1,) = _norm_proj(h, norm1_g[1], mod[1], sel_all, w_in1, perm, (w_in1.shape[1],), (F32,), 1)
    wq = mla_w_uq[0].reshape(Q_RANK, MLA_HEADS, QK_NOPE + QK_ROPE)
    wq = jnp.concatenate([wq, _rot_cols(wq[..., QK_NOPE:])], axis=-1).reshape(Q_RANK, -1).astype(BF16)
    cos, sin = _rope_tables(lc, l)
    qn, qp, kn, vv, kp = _mla_proj(a1, mla_q_norm_g[0][None], mla_kv_norm_g[0][None], wq,
                                   mla_w_ukv[0].astype(BF16), cos, sin, nct)
    o = _attention(qn, qp, kn, kp, vv)
    h = _oproj(o, perm_t, mla_w_o[0].astype(BF16), h, mod[1], lambda t: t + nct, sel_lat)
    wr, br = router_weights(1)
    return _moe(h, norm2_g[1], mod[1], sel_lat, wr, br, moe_w_gate, moe_w_up, moe_w_down, 1, final_g, perm, True)
```

```python
import functools

import jax
import jax.numpy as jnp
import numpy as np
from jax import lax
from jax.experimental import pallas as pl
from jax.experimental.pallas import tpu as pltpu

F32 = jnp.float32
BF16 = jnp.bfloat16
HIGHEST = lax.Precision.HIGHEST

NB = 8
TM = 256
RET_TQ = 256
TQ_ATTN = 256
TMOE = 512
EPS = 1e-6
GN_EPS = 64e-5
GRID_W = 64
ROPE_BASE = 10000.0

RET_HEADS, RET_DK, RET_DV = 4, 64, 128
RWKV_HEADS, RWKV_N = 8, 64
RWKV_W = RWKV_HEADS * RWKV_N
MLA_HEADS, Q_RANK, KV_RANK, QK_NOPE, QK_ROPE, V_HEAD = 8, 384, 256, 128, 64, 128
MLA_SCALE = (QK_NOPE + QK_ROPE) ** -0.5
N_GROUPS, EXPERTS_PER_GROUP = 4, 8
N_EXPERTS = N_GROUPS * EXPERTS_PER_GROUP
N_MOD = 6
MOD_ROWS = 16


def _cparams(sem):
    return pltpu.CompilerParams(dimension_semantics=sem)


def _rows8(x, fn):
    tm, d = x.shape
    return fn(x.reshape(tm // NB, NB, d)).reshape(tm, d)


def _norm_mod(x, g, shift8, scale8):
    var = jnp.mean(x * x, axis=-1, keepdims=True)
    y = x * lax.rsqrt(var + EPS) * g
    return _rows8(y, lambda y3: y3 * (1.0 + scale8[None]) + shift8[None])


def _gated_residual(h, gate8, y):
    return h + _rows8(y, lambda y3: y3 * gate8[None])


def _mod_spec(d, sel):
    return pl.BlockSpec((None, N_MOD, NB, d), lambda t: (sel(t), 0, 0, 0))


def _split_bf16(x):
    hi = x.astype(BF16)
    return hi, (x - hi.astype(F32)).astype(BF16)


def _seg_sum(x, ones):
    hi, lo = _split_bf16(x)
    return jnp.dot(hi, ones, preferred_element_type=F32) + jnp.dot(lo, ones, preferred_element_type=F32)


def _seg_sum_left(sel, x):
    hi, lo = _split_bf16(x)
    return jnp.dot(sel, hi, preferred_element_type=F32) + jnp.dot(sel, lo, preferred_element_type=F32)


def _seg_sum_left3(sel, x):
    hi, lo = _split_bf16(x)
    rest = (x - hi.astype(F32)) - lo.astype(F32)
    return (jnp.dot(sel, hi, preferred_element_type=F32) + jnp.dot(sel, lo, preferred_element_type=F32)
            + jnp.dot(sel, rest.astype(BF16), preferred_element_type=F32))


def _dot3(x, w_ref):
    hi, lo = _split_bf16(x)
    w_hi = w_ref[0]
    return (jnp.dot(hi, w_hi, preferred_element_type=F32) + jnp.dot(lo, w_hi, preferred_element_type=F32)
            + jnp.dot(hi, w_ref[1], preferred_element_type=F32))


def _ada_kernel(c_ref, w_ref, b_ref, o_ref):
    s = c_ref[...]
    s = s * jax.nn.sigmoid(s)
    o_ref[...] = jnp.dot(s, w_ref[...], precision=HIGHEST, preferred_element_type=F32) + b_ref[...]


def _ada_table(c_all, ada_w, ada_b):
    depth, d, nd = ada_w.shape
    out = pl.pallas_call(
        _ada_kernel,
        grid=(depth, nd // d),
        in_specs=[pl.BlockSpec((MOD_ROWS, d), lambda l, j: (0, 0)),
                  pl.BlockSpec((None, d, d), lambda l, j: (l, 0, j)),
                  pl.BlockSpec((None, 1, d), lambda l, j: (l, 0, j))],
        out_specs=pl.BlockSpec((None, MOD_ROWS, d), lambda l, j: (l, 0, j)),
        out_shape=jax.ShapeDtypeStruct((depth, MOD_ROWS, nd), F32),
        compiler_params=_cparams(("arbitrary", "arbitrary")),
        name="ada_table",
    )(c_all, ada_w, ada_b.reshape(depth, 1, nd))
    return out.reshape(depth, MOD_ROWS, N_MOD, d)


def _proj_kernel(x_ref, g_ref, mod_ref, w_ref, p_ref, *o_refs, splits, nperm):
    xm = _norm_mod(x_ref[...], g_ref[...], mod_ref[0], mod_ref[1]).astype(BF16)
    tm = xm.shape[0]
    xp = jnp.dot(p_ref[...], xm, preferred_element_type=F32).astype(BF16) if nperm else None
    off = 0
    for idx, (o_ref, n) in enumerate(zip(o_refs, splits)):
        for j in range(0, n, 512):
            c = min(512, n - j)
            w = w_ref[:, off + j:off + j + c]
            if idx < nperm:
                res = jnp.dot(xp, w, preferred_element_type=F32).astype(o_ref.dtype)
                o_ref[:, :, j:j + c] = res.reshape(NB, tm // NB, c)
            else:
                o_ref[:, j:j + c] = jnp.dot(xm, w, preferred_element_type=F32).astype(o_ref.dtype)
        off += n


def _norm_proj(h, g, mod, sel, w_bf16, perm, splits, dtypes, nperm):
    n, d = h.shape
    nout = w_bf16.shape[1]
    steps = TM // NB
    out_specs, out_shape = [], []
    for idx, (s, dt) in enumerate(zip(splits, dtypes)):
        if idx < nperm:
            out_specs.append(pl.BlockSpec((NB, steps, s), lambda t: (0, t, 0)))
            out_shape.append(jax.ShapeDtypeStruct((NB, n // NB, s), dt))
        else:
            out_specs.append(pl.BlockSpec((TM, s), lambda t: (t, 0)))
            out_shape.append(jax.ShapeDtypeStruct((n, s), dt))
    return pl.pallas_call(
        functools.partial(_proj_kernel, splits=splits, nperm=nperm),
        grid=(n // TM,),
        in_specs=[pl.BlockSpec((TM, d), lambda t: (t, 0)),
                  pl.BlockSpec((1, d), lambda t: (0, 0)),
                  _mod_spec(d, sel),
                  pl.BlockSpec((d, nout), lambda t: (0, 0)),
                  pl.BlockSpec(perm.shape, lambda t: (0, 0))],
        out_specs=out_specs,
        out_shape=out_shape,
        compiler_params=_cparams(("arbitrary",)),
        name="norm_proj",
    )(h, g.reshape(1, d), mod, w_bf16, perm)


def _ret_kernel(lg_ref, q_ref, k_ref, v_ref, o_ref, g_ref, *, lc, l, tq):
    nct = lc // tq
    nk = (lc + l) // tq
    dn = (((1,), (1,)), ((), ()))
    scale = RET_DK ** -0.5

    rel = (lax.broadcasted_iota(jnp.int32, (tq, tq), 0)
           - lax.broadcasted_iota(jnp.int32, (tq, tq), 1)).astype(F32)
    for hh in range(2):
        lg = lg_ref[hh][0:1, 0:1]
        g_ref[hh, 0] = jnp.exp(lg * rel)
        g_ref[hh, 1] = jnp.exp(-(lg * rel))
        g_ref[hh, 2] = jnp.exp(lg * jnp.abs(rel))

    k2 = k_ref[...]
    v2 = v_ref[...]

    def tile(qi, carry):
        rows = pl.ds(pl.multiple_of(qi * tq, tq), tq)
        q_lat = qi >= nct
        q2 = q_ref[rows, :]
        for hh in range(2):
            lg = lg_ref[hh][0:1, 0:1]
            s = lax.dot_general(q2[:, hh * RET_DK:(hh + 1) * RET_DK], k2[:, hh * RET_DK:(hh + 1) * RET_DK], dn,
                                preferred_element_type=F32)
            pieces = []
            for kj in range(nk):
                d = qi - kj
                idx = jnp.where(d > 0, 0, jnp.where(d < 0, 1, 2))
                dabs = jnp.full((1, 1), jnp.abs(d) * tq, jnp.int32).astype(F32)
                sig = jnp.exp(lg * dabs) * scale
                if kj < nct:
                    dback = jnp.full((1, 1), l + lc - d * tq, jnp.int32).astype(F32)
                    sig2 = jnp.where(q_lat, jnp.exp(lg * dback) * scale, 0.0)
                    m = g_ref[hh, idx] * sig + g_ref[hh, 1] * sig2
                else:
                    m = g_ref[hh, idx] * jnp.where(q_lat, sig, 0.0)
                pieces.append((s[:, kj * tq:(kj + 1) * tq] * m).astype(BF16))
            p = jnp.concatenate(pieces, axis=1)
            o_ref[rows, hh * RET_DV:(hh + 1) * RET_DV] = jnp.dot(p, v2[:, hh * RET_DV:(hh + 1) * RET_DV],
                                                                preferred_element_type=F32)
        return carry

    lax.fori_loop(0, nk, tile, 0, unroll=3)


def _retention(qkv, lg, lc, l):
    b, s, _ = qkv.shape
    tq = RET_TQ if lc % RET_TQ == 0 else 128
    qk_w = 2 * RET_DK
    v_w = 2 * RET_DV
    k_blk0 = RET_HEADS * RET_DK // qk_w
    v_blk0 = 2 * RET_HEADS * RET_DK // v_w
    return pl.pallas_call(
        functools.partial(_ret_kernel, lc=lc, l=l, tq=tq),
        grid=(b, RET_HEADS // 2),
        in_specs=[pl.BlockSpec((2, 1, 128), lambda bi, hp: (hp, 0, 0)),
                  pl.BlockSpec((None, s, qk_w), lambda bi, hp: (bi, 0, hp)),
                  pl.BlockSpec((None, s, qk_w), lambda bi, hp: (bi, 0, k_blk0 + hp)),
                  pl.BlockSpec((None, s, v_w), lambda bi, hp: (bi, 0, v_blk0 + hp))],
        out_specs=pl.BlockSpec((None, s, v_w), lambda bi, hp: (bi, 0, hp)),
        out_shape=jax.ShapeDtypeStruct((b, s, RET_HEADS * RET_DV), F32),
        scratch_shapes=[pltpu.VMEM((2, 3, tq, tq), F32)],
        compiler_params=_cparams(("arbitrary", "arbitrary")),
        name="retention",
    )(lg, qkv, qkv, qkv)


def _head_pairs(qa, qb):
    tm = qa.shape[0]
    lo = lax.broadcasted_iota(jnp.int32, (tm, 128), 1) < RWKV_N
    out = []
    for c in range(RWKV_W // 128):
        a = qa[:, c * 128:(c + 1) * 128]
        b = qb[:, c * 128:(c + 1) * 128]
        out.append(jnp.where(lo, a, pltpu.roll(b, RWKV_N, 1)))
        out.append(jnp.where(lo, pltpu.roll(a, RWKV_N, 1), b))
    return out


def _feat_kernel(rw_ref, prev_ref, next_ref, mu_ref, w0_ref, w2_ref, a0_ref, a2_ref, g2_ref,
                 kkw_ref, ka_ref, rk_ref, ones_ref,
                 p1_o, p2f_o, p2b_o, p3f_o, p3b_o, gate_o, bonus_o, *, nct, ntile):
    t = pl.program_id(0)
    first = jnp.logical_or(t == 0, t == nct)
    last = jnp.logical_or(t == nct - 1, t == ntile - 1)
    y = rw_ref[...]
    tm, wid = y.shape
    grp = lax.broadcasted_iota(jnp.int32, (tm // NB, 1, 1), 0)
    prow = jnp.where(first, 0.0, prev_ref[...])
    nrow = jnp.where(last, 0.0, next_ref[...])
    prev = jnp.where(grp == 0, prow[None], pltpu.roll(y, NB, 0).reshape(tm // NB, NB, wid)).reshape(tm, wid)
    nxt = jnp.where(grp == tm // NB - 1, nrow[None],
                    pltpu.roll(y, tm - NB, 0).reshape(tm // NB, NB, wid)).reshape(tm, wid)
    ys = y + mu_ref[0:1, :] * (prev - y) + mu_ref[1:2, :] * (nxt - y)

    w_ = RWKV_W
    r = ys[:, 0:w_]
    kr = ys[:, w_:2 * w_]
    vr = ys[:, 2 * w_:3 * w_]
    wd = ys[:, 3 * w_:3 * w_ + 128]
    ad = ys[:, 3 * w_ + 128:3 * w_ + 256]
    gd = ys[:, 3 * w_ + 256:3 * w_ + 512]
    ones = ones_ref[...]

    kk = kr * kkw_ref[...]
    ss = _seg_sum(kk * kk, ones)
    kk = kk / jnp.maximum(jnp.sqrt(ss), 1e-12)
    zw = w0_ref[...] + _dot3(jnp.tanh(wd), w2_ref)
    decay = jnp.exp(-(float(np.exp(-0.5)) * jax.nn.sigmoid(zw)))
    a = jax.nn.sigmoid(a0_ref[...] + _dot3(ad, a2_ref))
    gate_o[...] = _dot3(jax.nn.sigmoid(gd), g2_ref)
    ka = ka_ref[...]

    def emit(o_ref, qa, qb):
        for hh, slab in enumerate(_head_pairs(qa, qb)):
            o_ref[hh] = slab

    emit(p1_o, kk, r)
    ktsum = None
    for d, (p2_o, p3_o) in enumerate(((p2f_o, p3f_o), (p2b_o, p3b_o))):
        a_d = a[:, d * w_:(d + 1) * w_]
        kt = kr * (1.0 + (a_d - 1.0) * ka)
        emit(p2_o, decay[:, d * w_:(d + 1) * w_], kk * a_d)
        emit(p3_o, kt, vr)
        ktsum = kt if ktsum is None else ktsum + kt
    bonus_o[...] = _seg_sum(r * ktsum * rk_ref[...], ones) * vr


def _rwkv_features(rw, params, nct):
    n, wid = rw.shape
    ntile = n // TM
    rb = TM // 8
    nrb = n // 8
    w_ = RWKV_W
    tok = pl.BlockSpec((TM, w_), lambda t: (t, 0))
    pair = pl.BlockSpec((RWKV_HEADS, TM, 128), lambda t: (0, t, 0))

    def full(a):
        return pl.BlockSpec(a.shape, lambda t: (0,) * a.ndim)

    one = jax.ShapeDtypeStruct((n, w_), F32)
    pshape = jax.ShapeDtypeStruct((RWKV_HEADS, n, 128), F32)
    return pl.pallas_call(
        functools.partial(_feat_kernel, nct=nct, ntile=ntile),
        grid=(ntile,),
        in_specs=[pl.BlockSpec((TM, wid), lambda t: (t, 0)),
                  pl.BlockSpec((8, wid), lambda t: (jnp.maximum(t * rb - 1, 0), 0)),
                  pl.BlockSpec((8, wid), lambda t: (jnp.minimum((t + 1) * rb, nrb - 1), 0))]
                 + [full(a) for a in params],
        out_specs=[pair] * 5 + [tok, tok],
        out_shape=[pshape] * 5 + [one, one],
        compiler_params=_cparams(("arbitrary",)),
        name="rwkv_features",
    )(rw, rw, rw, *params)


def _scan_kernel(p1f, p1b, p2f, p2b, p3f, p3b, yf_ref, yb_ref, s_ref, t_ref, y_buf):
    @pl.when(pl.program_id(0) == 0)
    def _():
        s_ref[...] = jnp.zeros_like(s_ref)
        y_buf[...] = jnp.zeros_like(y_buf)

    nkey = s_ref.shape[0]
    nh = p1f.shape[0]
    tc = p1f.shape[1] // NB
    half = NB * nh
    lanes = 2 * half
    kblk = 32
    pairs = ((p1f, p1b), (p2f, p2b), (p3f, p3b))
    kk_q, r_q, w_q, kka_q, kt_q, v_q = (0, 0), (0, nkey), (1, 0), (1, nkey), (2, 0), (2, nkey)

    def rows_of(i):
        return pl.ds(pl.multiple_of(i * NB, NB), NB)

    def relayout(i, dst):
        for p, (f_ref, b_ref) in enumerate(pairs):
            rows = ([f_ref[hh, rows_of(i), :] for hh in range(nh)]
                    + [b_ref[hh, rows_of(tc - 1 - i), :] for hh in range(nh)])
            dst[p] = jnp.concatenate(rows, axis=0).T

    def emit(i):
        y = y_buf[...].reshape(nkey, lanes)
        yt = jnp.concatenate([y, y], axis=0).T
        for hh in range(nh):
            yf_ref[hh, rows_of(i), :] = yt[hh * NB:(hh + 1) * NB]
            yb_ref[hh, rows_of(tc - 1 - i), :] = yt[half + hh * NB:half + (hh + 1) * NB]

    def step(i, cur, nxt):
        def row(q, k):
            return cur[q[0], pl.ds(q[1] + k, 1), :][None]

        sa = s_ref[0] * row(kk_q, 0)
        for k in range(1, nkey):
            sa = sa + s_ref[k] * row(kk_q, k)
        relayout(jnp.minimum(i + 1, tc - 1), nxt)
        emit(jnp.maximum(i - 1, 0))
        v = cur[v_q[0], v_q[1]:v_q[1] + nkey, :].reshape(nkey // 8, 8, lanes)

        def upd_body(kb, y):
            base = pl.multiple_of(kb * kblk, kblk)
            for j in range(kblk):
                k = base + j
                s_new = s_ref[k] * row(w_q, k) + (v * row(kt_q, k) - sa * row(kka_q, k))
                s_ref[k] = s_new
                y = y + s_new * row(r_q, k)
            return y

        y_buf[...] = lax.fori_loop(0, nkey // kblk, upd_body, jnp.zeros((nkey // 8, 8, lanes), F32))

    relayout(0, t_ref.at[0])

    def two_steps(j, carry):
        step(2 * j, t_ref.at[0], t_ref.at[1])
        step(2 * j + 1, t_ref.at[1], t_ref.at[0])
        return carry

    lax.fori_loop(0, tc // 2, two_steps, 0)
    emit(tc - 1)


def _rwkv_scan(p1, p2f, p2b, p3f, p3b, nct):
    nh, n, _ = p1.shape
    nkey = RWKV_N
    ntb = n // TM

    def mirror(g):
        return jnp.where(g < nct, nct - 1 - g, nct + ntb - 1 - g)

    sf = pl.BlockSpec((nh, TM, 128), lambda g: (0, g, 0))
    sb = pl.BlockSpec((nh, TM, 128), lambda g: (0, mirror(g), 0))
    out = jax.ShapeDtypeStruct((nh, n, 128), F32)
    lanes = 2 * NB * nh
    return pl.pallas_call(
        _scan_kernel,
        grid=(ntb,),
        in_specs=[sf, sb, sf, sb, sf, sb],
        out_specs=[sf, sb],
        out_shape=[out, out],
        scratch_shapes=[pltpu.VMEM((nkey, nkey // 8, 8, lanes), F32),
                        pltpu.VMEM((2, 3, 2 * nkey, lanes), F32),
                        pltpu.VMEM((nkey // 8, 8, lanes), F32)],
        compiler_params=_cparams(("arbitrary",)),
        name="rwkv_scan",
    )(p1, p1, p2f, p2b, p3f, p3b)


def _even_out_kernel(o_ref, g_ref, yf_ref, yb_ref, bonus_ref, gate_ref, lnw_ref, lnb_ref, ones_ref, pt_ref,
                     w_ref, h_ref, mod_ref, out_ref):
    tm = h_ref.shape[0]
    o_all = _seg_sum_left(pt_ref[...], o_ref[...].reshape(tm, RET_HEADS * RET_DV))
    parts = []
    for hh in range(RET_HEADS):
        o = o_all[:, hh * RET_DV:(hh + 1) * RET_DV]
        o = o * lax.rsqrt(jnp.mean(o * o, axis=-1, keepdims=True) + EPS)
        gg = g_ref[:, hh * RET_DV:(hh + 1) * RET_DV]
        parts.append(o * (gg * jax.nn.sigmoid(gg)))
    ones = ones_ref[...]
    lo = lax.broadcasted_iota(jnp.int32, (tm, 128), 1) < RWKV_N
    chunks = []
    for c in range(RWKV_HEADS // 2):
        even = yf_ref[2 * c] + yb_ref[2 * c]
        odd = yf_ref[2 * c + 1] + yb_ref[2 * c + 1]
        chunks.append(jnp.where(lo, even, pltpu.roll(odd, RWKV_N, 1)))
    y = jnp.concatenate(chunks, axis=-1)
    mu = _seg_sum(y, ones) * (1.0 / RWKV_N)
    yc = y - mu
    var = _seg_sum(yc * yc, ones) * (1.0 / RWKV_N)
    yn = yc * lax.rsqrt(var + GN_EPS) * lnw_ref[...] + lnb_ref[...]
    parts.append((yn + bonus_ref[...]) * gate_ref[...])
    cat = jnp.concatenate(parts, axis=-1).astype(BF16)
    mix = jnp.dot(cat, w_ref[...], preferred_element_type=F32)
    out_ref[...] = _gated_residual(h_ref[...], mod_ref[2], mix)


def _even_out(o_ret, g, y2, bonus, gate, lnw, lnb, ones, perm_t, w_out, h, mod, sel):
    n, d = h.shape
    w_ = RWKV_W
    tok = pl.BlockSpec((TM, w_), lambda t: (t, 0))
    ysp = pl.BlockSpec((RWKV_HEADS, TM, 128), lambda t: (0, t, 0))
    osp = pl.BlockSpec((NB, TM // NB, o_ret.shape[2]), lambda t: (0, t, 0))

    def full(a):
        return pl.BlockSpec(a.shape, lambda t: (0,) * a.ndim)

    return pl.pallas_call(
        _even_out_kernel,
        grid=(n // TM,),
        in_specs=[osp, tok, ysp, ysp, tok, tok, full(lnw), full(lnb), full(ones), full(perm_t), full(w_out),
                  pl.BlockSpec((TM, d), lambda t: (t, 0)),
                  _mod_spec(d, sel)],
        out_specs=pl.BlockSpec((TM, d), lambda t: (t, 0)),
        out_shape=jax.ShapeDtypeStruct((n, d), F32),
        compiler_params=_cparams(("arbitrary",)),
        name="even_out",
    )(o_ret, g, y2[0], y2[1], bonus, gate, lnw, lnb, ones, perm_t, w_out, h, mod)


def _mla_proj_kernel(a_ref, qg_ref, kvg_ref, wuq_ref, wukv_ref, cos_ref, sin_ref,
                     qn_o, qp_o, kn_o, v_o, kp_o):
    nb, steps, wid = a_ref.shape
    tm = nb * steps
    a = a_ref[...].reshape(tm, wid)
    cos = jnp.broadcast_to(cos_ref[...][None], (nb, steps, QK_ROPE)).reshape(tm, QK_ROPE)
    sin = jnp.broadcast_to(sin_ref[...][None], (nb, steps, QK_ROPE)).reshape(tm, QK_ROPE)
    zpad = jnp.zeros((tm, 128 - QK_ROPE), F32)

    def rms(t, g):
        return (t * lax.rsqrt(jnp.mean(t * t, axis=-1, keepdims=True) + EPS) * g).astype(BF16)

    def pad128(pe):
        return jnp.concatenate([pe, zpad], axis=1).astype(BF16).reshape(nb, steps, 128)

    cq = rms(a[:, :Q_RANK], qg_ref[...])
    ckv = rms(a[:, Q_RANK:Q_RANK + KV_RANK], kvg_ref[...])
    pe0 = Q_RANK + KV_RANK
    kp_o[...] = pad128(a[:, pe0:pe0 + QK_ROPE] * cos + a[:, pe0 + QK_ROPE:pe0 + 2 * QK_ROPE] * sin)
    hw = QK_NOPE + 2 * QK_ROPE
    for hh in range(MLA_HEADS):
        hs = slice(hh * 128, (hh + 1) * 128)
        qh = jnp.dot(cq, wuq_ref[:, hh * hw:(hh + 1) * hw], preferred_element_type=F32)
        qn_o[:, :, hs] = (qh[:, :QK_NOPE] * MLA_SCALE).astype(BF16).reshape(nb, steps, 128)
        qp_o[:, :, hs] = pad128(
            (qh[:, QK_NOPE:QK_NOPE + QK_ROPE] * cos + qh[:, QK_NOPE + QK_ROPE:] * sin) * MLA_SCALE)
        kvh = jnp.dot(ckv, wukv_ref[:, hh * hw:(hh + 1) * hw], preferred_element_type=F32)
        kn_o[:, :, hs] = kvh[:, :QK_NOPE].astype(BF16).reshape(nb, steps, 128)
        v_o[:, :, hs] = kvh[:, QK_NOPE:].astype(BF16).reshape(nb, steps, 128)


def _mla_proj(a, qg, kvg, wuq, wukv, cos, sin, nct):
    nb, s, wid = a.shape
    hw = MLA_HEADS * 128
    steps = TM // NB
    ntile = s // steps
    l = (ntile - nct) * steps

    def full(x):
        return pl.BlockSpec(x.shape, lambda t: (0,) * x.ndim)

    def lat_blk(t):
        return jnp.maximum(t - nct, 0)

    allrows = pl.BlockSpec((nb, steps, hw), lambda t: (0, t, 0))
    latrows = pl.BlockSpec((nb, steps, hw), lambda t: (0, lat_blk(t), 0))
    rope = pl.BlockSpec((steps, QK_ROPE), lambda t: (t, 0))
    return pl.pallas_call(
        _mla_proj_kernel,
        grid=(ntile,),
        in_specs=[pl.BlockSpec((nb, steps, wid), lambda t: (0, t, 0)), full(qg), full(kvg), full(wuq),
                  full(wukv), rope, rope],
        out_specs=[latrows, latrows, allrows, allrows, pl.BlockSpec((nb, steps, 128), lambda t: (0, t, 0))],
        out_shape=[jax.ShapeDtypeStruct((nb, l, hw), BF16),
                   jax.ShapeDtypeStruct((nb, l, hw), BF16),
                   jax.ShapeDtypeStruct((nb, s, hw), BF16),
                   jax.ShapeDtypeStruct((nb, s, hw), BF16),
                   jax.ShapeDtypeStruct((nb, s, 128), BF16)],
        compiler_params=_cparams(("arbitrary",)),
        name="mla_proj",
    )(a, qg, kvg, wuq, wukv, cos, sin)


def _attn_kernel(qn_ref, qp_ref, kn_ref, kp_ref, v_ref, o_ref, *, tq):
    hi = pl.program_id(1)
    dn = (((1,), (1,)), ((), ()))
    hsl = pl.ds(pl.multiple_of(hi * 128, 128), 128)
    k = jnp.concatenate([kn_ref[:, hsl], kp_ref[...]], axis=1)
    v = v_ref[:, hsl]

    def tile(i, carry):
        rows = pl.ds(pl.multiple_of(i * tq, tq), tq)
        q = jnp.concatenate([qn_ref[rows, :], qp_ref[rows, :]], axis=1)
        s = lax.dot_general(q, k, dn, preferred_element_type=F32)
        m = jnp.max(s, axis=-1, keepdims=True)
        p = jnp.exp(s - m)
        l = jnp.sum(p, axis=-1, keepdims=True)
        o = jnp.dot(p.astype(BF16), v, preferred_element_type=F32)
        o_ref[rows, :] = (o / l).astype(o_ref.dtype)
        return carry

    lax.fori_loop(0, qn_ref.shape[0] // tq, tile, 0, unroll=8)


def _attention(qn, qp, kn, kp, v):
    b, l, hw = qn.shape
    s = kn.shape[1]
    hd = MLA_HEADS
    tq = TQ_ATTN if l % TQ_ATTN == 0 else 128
    qspec = pl.BlockSpec((None, l, 128), lambda bi, hi: (bi, 0, hi))
    kvspec = pl.BlockSpec((None, s, hw), lambda bi, hi: (bi, 0, 0))
    return pl.pallas_call(
        functools.partial(_attn_kernel, tq=tq),
        grid=(b, hd),
        in_specs=[qspec, qspec, kvspec,
                  pl.BlockSpec((None, s, 128), lambda bi, hi: (bi, 0, 0)),
                  kvspec],
        out_specs=qspec,
        out_shape=jax.ShapeDtypeStruct((b, l, hw), BF16),
        compiler_params=_cparams(("arbitrary", "arbitrary")),
        name="mla_attention",
    )(qn, qp, kn, kp, v)


def _oproj_kernel(o_ref, pt_ref, w_ref, h_ref, mod_ref, out_ref):
    nb, steps, wid = o_ref.shape
    o = jnp.dot(pt_ref[...], o_ref[...].reshape(nb * steps, wid), preferred_element_type=F32).astype(BF16)
    mix = jnp.dot(o, w_ref[...], preferred_element_type=F32)
    out_ref[...] = _gated_residual(h_ref[...], mod_ref[2], mix)


def _oproj(o, perm_t, w_o, h, mod, hrow, sel):
    nb, l, wid = o.shape
    d = h.shape[1]
    steps = TM // NB
    return pl.pallas_call(
        _oproj_kernel,
        grid=(l // steps,),
        in_specs=[pl.BlockSpec((nb, steps, wid), lambda t: (0, t, 0)),
                  pl.BlockSpec(perm_t.shape, lambda t: (0, 0)),
                  pl.BlockSpec(w_o.shape, lambda t: (0, 0)),
                  pl.BlockSpec((TM, d), lambda t: (hrow(t), 0)),
                  _mod_spec(d, sel)],
        out_specs=pl.BlockSpec((TM, d), lambda t: (t, 0)),
        out_shape=jax.ShapeDtypeStruct((nb * l, d), F32),
        compiler_params=_cparams(("arbitrary",)),
        name="mla_oproj",
    )(o, perm_t, w_o, h, mod)


def _route_kernel(h_ref, g_ref, mod_ref, wr_ref, br_ref, xl_ref, route_ref):
    xl = _norm_mod(h_ref[...], g_ref[...], mod_ref[3], mod_ref[4])
    xl_ref[...] = xl
    logits = jnp.dot(xl, wr_ref[...], precision=HIGHEST, preferred_element_type=F32) + br_ref[...]
    lane_i = lax.broadcasted_iota(jnp.int32, logits.shape, 1)
    lane = lane_i.astype(F32)
    lane_grp = (lane_i >> 3).astype(F32)
    neg = -jnp.inf
    big = 1e6
    gl = jnp.where(jnp.logical_and(lane_i >= N_EXPERTS, lane_i < N_EXPERTS + N_GROUPS), logits, neg)
    gmax = jnp.max(gl, axis=-1, keepdims=True)
    gsum = jnp.sum(jnp.exp(gl - gmax), axis=-1, keepdims=True)
    pg = 1.0 / gsum
    gidx = jnp.min(jnp.where(gl == gmax, lane - N_EXPERTS, big), axis=-1, keepdims=True)
    in_grp = jnp.logical_and(lane_i < N_EXPERTS, lane_grp == gidx)
    el = jnp.where(in_grp, logits, neg)
    emax = jnp.max(el, axis=-1, keepdims=True)
    esum = jnp.sum(jnp.exp(el - emax), axis=-1, keepdims=True)
    i1 = jnp.min(jnp.where(el == emax, lane, big), axis=-1, keepdims=True)
    el2 = jnp.where(lane == i1, neg, el)
    emax2 = jnp.max(el2, axis=-1, keepdims=True)
    i2 = jnp.min(jnp.where(el2 == emax2, lane, big), axis=-1, keepdims=True)
    pe1 = 1.0 / esum
    pe2 = jnp.exp(emax2 - emax) / esum
    den = pe1 + pe2
    w1 = pg * pe1 / den
    w2 = pg * pe2 / den
    route_ref[...] = jnp.where(lane_i == 0, i1,
                               jnp.where(lane_i == 1, i2,
                                         jnp.where(lane_i == 2, w1, jnp.where(lane_i == 3, w2, 0.0))))


def _route(h, g, mod, sel, wr, br):
    n, d = h.shape
    return pl.pallas_call(
        _route_kernel,
        grid=(n // TM,),
        in_specs=[pl.BlockSpec((TM, d), lambda t: (t, 0)),
                  pl.BlockSpec((1, d), lambda t: (0, 0)),
                  _mod_spec(d, sel),
                  pl.BlockSpec(wr.shape, lambda t: (0, 0)),
                  pl.BlockSpec(br.shape, lambda t: (0, 0))],
        out_specs=[pl.BlockSpec((TM, d), lambda t: (t, 0)),
                   pl.BlockSpec((TM, 128), lambda t: (t, 0))],
        out_shape=[jax.ShapeDtypeStruct((n, d), F32), jax.ShapeDtypeStruct((n, 128), F32)],
        compiler_params=_cparams(("arbitrary",)),
        name="moe_route",
    )(h, g.reshape(1, d), mod, wr, br)


def _expert_kernel(te_ref, nt_ref, x_ref, wg_ref, wu_ref, wd_ref, o_ref, wg_s, wu_s, wd_s):
    t = pl.program_id(0)

    @pl.when(t < nt_ref[0])
    def _():
        changed = jnp.logical_or(t == 0, te_ref[t] != te_ref[jnp.maximum(t - 1, 0)])

        @pl.when(changed)
        def _():
            wg_s[...] = wg_ref[...].astype(BF16)
            wu_s[...] = wu_ref[...].astype(BF16)
            wd_s[...] = wd_ref[...].astype(BF16)

        x = x_ref[...].astype(BF16)
        h1 = jnp.dot(x, wg_s[...], preferred_element_type=F32)
        h2 = jnp.dot(x, wu_s[...], preferred_element_type=F32)
        hid = ((h1 * jax.nn.sigmoid(h1)) * h2).astype(BF16)
        o_ref[...] = jnp.dot(hid, wd_s[...], preferred_element_type=F32)

    @pl.when(t >= nt_ref[0])
    def _():
        o_ref[...] = jnp.zeros_like(o_ref)


def _experts(tile_e, ntiles, xs, w_gate, w_up, w_down, layer):
    rows, d = xs.shape
    hid = w_gate.shape[-1]
    grid_spec = pltpu.PrefetchScalarGridSpec(
        num_scalar_prefetch=2,
        grid=(rows // TMOE,),
        in_specs=[pl.BlockSpec((TMOE, d), lambda t, te, nt: (jnp.minimum(t, nt[0] - 1), 0)),
                  pl.BlockSpec((None, None, d, hid), lambda t, te, nt: (layer, te[t], 0, 0)),
                  pl.BlockSpec((None, None, d, hid), lambda t, te, nt: (layer, te[t], 0, 0)),
                  pl.BlockSpec((None, None, hid, d), lambda t, te, nt: (layer, te[t], 0, 0))],
        out_specs=pl.BlockSpec((TMOE, d), lambda t, te, nt: (t, 0)),
        scratch_shapes=[pltpu.VMEM((d, hid), BF16), pltpu.VMEM((d, hid), BF16), pltpu.VMEM((hid, d), BF16)],
    )
    return pl.pallas_call(
        _expert_kernel,
        grid_spec=grid_spec,
        out_shape=jax.ShapeDtypeStruct((rows, d), F32),
        compiler_params=_cparams(("arbitrary",)),
        name="moe_experts",
    )(tile_e, ntiles, xs, w_gate, w_up, w_down)


def _combine_kernel(y1_ref, y2_ref, route_ref, h_ref, mod_ref, fg_ref, p_ref, out_ref, *, final):
    route = route_ref[...]
    y = route[:, 2:3] * y1_ref[...] + route[:, 3:4] * y2_ref[...]
    hn = _gated_residual(h_ref[...], mod_ref[5], y)
    if final:
        hn = hn * lax.rsqrt(jnp.mean(hn * hn, axis=-1, keepdims=True) + EPS) * fg_ref[...]
        out_ref[...] = _seg_sum_left3(p_ref[...], hn).reshape(out_ref.shape)
    else:
        out_ref[...] = hn


def _combine(y1, y2, route, h, mod, sel, fg, perm, final):
    n, d = h.shape
    tok = pl.BlockSpec((TM, d), lambda t: (t, 0))
    if final:
        out_spec = pl.BlockSpec((NB, TM // NB, d), lambda t: (0, t, 0))
        out_shape = jax.ShapeDtypeStruct((NB, n // NB, d), F32)
    else:
        out_spec, out_shape = tok, jax.ShapeDtypeStruct((n, d), F32)
    return pl.pallas_call(
        functools.partial(_combine_kernel, final=final),
        grid=(n // TM,),
        in_specs=[tok, tok, pl.BlockSpec((TM, 128), lambda t: (t, 0)), tok,
                  _mod_spec(d, sel),
                  pl.BlockSpec((1, d), lambda t: (0, 0)),
                  pl.BlockSpec(perm.shape, lambda t: (0, 0))],
        out_specs=out_spec,
        out_shape=out_shape,
        compiler_params=_cparams(("arbitrary",)),
        name="moe_combine",
    )(y1, y2, route, h, mod, fg.reshape(1, d), perm)


def _moe(h, g, mod, sel, wr, br, w_gate, w_up, w_down, layer, fg, perm, final):
    n, d = h.shape
    xl, route = _route(h, g, mod, sel, wr, br)
    e = route[:, :2].astype(jnp.int32).reshape(-1)
    onehot = (e[:, None] == jnp.arange(N_EXPERTS, dtype=jnp.int32)[None, :]).astype(jnp.int32)
    csum = jnp.cumsum(onehot, axis=0)
    counts = csum[-1]
    rank = jnp.sum(csum * onehot, axis=1) - 1
    padded = ((counts + TMOE - 1) // TMOE) * TMOE
    pend = jnp.cumsum(padded)
    pos = (pend - padded)[e] + rank
    rows = 2 * n + N_EXPERTS * TMOE
    hit = jnp.zeros((rows,), jnp.int32).at[pos].add(jnp.arange(2 * n, dtype=jnp.int32) // 2 + 1,
                                                    mode="promise_in_bounds", unique_indices=True)
    src = jnp.where(hit > 0, hit - 1, jnp.arange(rows, dtype=jnp.int32) % n)
    ntile = rows // TMOE
    nvalid = (pend[-1] // TMOE).astype(jnp.int32)
    tstart = jnp.arange(ntile, dtype=jnp.int32) * TMOE
    tile_e = jnp.sum((tstart[:, None] >= pend[None, :]).astype(jnp.int32), axis=1)
    last_e = jnp.sum((((nvalid - 1) * TMOE) >= pend).astype(jnp.int32))
    tile_e = jnp.where(tstart < pend[-1], tile_e, last_e).astype(jnp.int32)
    xs = xl.at[src].get(mode="promise_in_bounds")
    ys = _experts(tile_e, nvalid.reshape(1), xs, w_gate, w_up, w_down, layer)
    pos2 = pos.reshape(n, 2)
    y1 = ys.at[pos2[:, 0]].get(mode="promise_in_bounds", unique_indices=True)
    y2 = ys.at[pos2[:, 1]].get(mode="promise_in_bounds", unique_indices=True)
    return _combine(y1, y2, route, h, mod, sel, fg, perm, final)


def _block_ones(width, seg):
    idx = np.arange(width) // seg
    return jnp.asarray((idx[:, None] == idx[None, :]).astype(np.float32))


def _rope_tables(lc, l):
    rows = l // GRID_W
    row = np.repeat(np.arange(rows, dtype=np.float32), GRID_W)
    col = np.tile(np.arange(GRID_W, dtype=np.float32), rows)
    n_freq = QK_ROPE // 4
    inv_freq = jnp.asarray(ROPE_BASE, F32) ** (-jnp.arange(n_freq, dtype=F32) / n_freq)
    ang_r = jnp.asarray(row)[:, None] * inv_freq
    ang_c = jnp.asarray(col)[:, None] * inv_freq
    cos = jnp.concatenate([jnp.cos(ang_r), jnp.cos(ang_r), jnp.cos(ang_c), jnp.cos(ang_c)], axis=-1)
    sin = jnp.concatenate([jnp.sin(ang_r), jnp.sin(ang_r), jnp.sin(ang_c), jnp.sin(ang_c)], axis=-1)
    cos = jnp.concatenate([jnp.ones((lc, QK_ROPE), F32), cos], axis=0)
    sin = jnp.concatenate([jnp.zeros((lc, QK_ROPE), F32), sin], axis=0)
    return cos, sin


def _batch_major_perm(tm):
    steps = tm // NB
    r_out = np.arange(tm)
    r_in = (r_out % steps) * NB + r_out // steps
    return jnp.asarray((r_in[:, None] == np.arange(tm)[None, :]).astype(np.float32)).astype(BF16)


def _rot_cols(pe):
    q = QK_ROPE // 4
    return jnp.concatenate([-pe[..., q:2 * q], pe[..., 0:q], -pe[..., 3 * q:4 * q], pe[..., 2 * q:3 * q]], axis=-1)


def _pad_cols(w, n):
    return jnp.pad(w, ((0, 0), (0, n - w.shape[1])))


def kernel(x, c, ctx, c_ctx, ada_w, ada_b, norm1_g, norm2_g, final_g, ev_w_in, ev_shift_mu, rwkv_w0, rwkv_w2, rwkv_a0, rwkv_a2, rwkv_g2, rwkv_k_k, rwkv_k_a, rwkv_r_k, rwkv_lnx_w, rwkv_lnx_b, ev_w_out, mla_w_in, mla_q_norm_g, mla_w_uq, mla_kv_norm_g, mla_w_ukv, mla_w_o, moe_w_grp, moe_b_grp, moe_w_exp, moe_b_exp, moe_w_gate, moe_w_up, moe_w_down):
    b, l, d = x.shape
    lc = ctx.shape[1]
    s = lc + l
    n = b * s
    steps = TM // NB
    nct = lc // steps
    assert b == NB and lc % steps == 0 and l % steps == 0 and ada_w.shape[0] == 2

    def sel_all(t):
        return (t >= nct).astype(jnp.int32)

    def sel_lat(t):
        return 1

    c_all = jnp.concatenate([c, c_ctx[None], jnp.zeros((MOD_ROWS - b - 1, d), F32)], axis=0)
    ada = _ada_table(c_all, ada_w, ada_b)
    mod = jnp.stack([jnp.broadcast_to(ada[:, b][:, :, None, :], (2, N_MOD, NB, d)),
                     ada[:, :b].transpose(0, 2, 1, 3)], axis=1)
    h = jnp.concatenate([ctx, x], axis=1).transpose(1, 0, 2).reshape(n, d)

    def router_weights(layer):
        wr = jnp.concatenate([moe_w_exp[layer], moe_w_grp[layer]], axis=1)
        br = jnp.concatenate([moe_b_exp[layer], moe_b_grp[layer]])[None]
        return _pad_cols(wr, 128), _pad_cols(br, 128)

    def two_term(w):
        hi = w.astype(BF16)
        return jnp.stack([hi, (w - hi.astype(F32)).astype(BF16)])

    w_ = RWKV_W
    ret_w = 2 * RET_HEADS * RET_DK + RET_HEADS * RET_DV
    w_in = _pad_cols(ev_w_in[0], 3584).astype(BF16)
    perm = _batch_major_perm(TM)
    perm_t = perm.T
    qkv, gret, rw = _norm_proj(h, norm1_g[0], mod[0], sel_all, w_in, perm,
                               (ret_w, RET_HEADS * RET_DV, 2048), (BF16, F32, F32), 1)
    lg = jnp.log1p(-jnp.exp2(-5.0 - jnp.arange(RET_HEADS, dtype=F32)))
    o_ret = _retention(qkv, jnp.broadcast_to(lg[:, None, None], (RET_HEADS, 1, 128)), lc, l)

    ones8 = _block_ones(w_, RWKV_N).astype(BF16)
    zero = jnp.zeros((64, w_), F32)
    w2bd = jnp.concatenate([jnp.concatenate([rwkv_w2[0, 0], zero], axis=1),
                            jnp.concatenate([zero, rwkv_w2[0, 1]], axis=1)], axis=0)
    a2bd = jnp.concatenate([jnp.concatenate([rwkv_a2[0, 0], zero], axis=1),
                            jnp.concatenate([zero, rwkv_a2[0, 1]], axis=1)], axis=0)
    g2p = jnp.pad(rwkv_g2[0], ((0, 256 - rwkv_g2.shape[1]), (0, 0)))
    mu = _pad_cols(ev_shift_mu[0], 2048)
    feat_params = (mu, rwkv_w0[0].reshape(1, 2 * w_), two_term(w2bd), rwkv_a0[0].reshape(1, 2 * w_),
                   two_term(a2bd), two_term(g2p),
                   rwkv_k_k[0][None], rwkv_k_a[0][None], rwkv_r_k[0].reshape(1, w_), ones8)
    p1, p2f, p2b, p3f, p3b, gate, bonus = _rwkv_features(rw, feat_params, nct)
    y2 = _rwkv_scan(p1, p2f, p2b, p3f, p3b, nct)

    h = _even_out(o_ret, gret, y2, bonus, gate, rwkv_lnx_w[0][None], rwkv_lnx_b[0][None], ones8, perm_t,
                  ev_w_out[0].astype(BF16), h, mod[0], sel_all)
    wr, br = router_weights(0)
    h = _moe(h, norm2_g[0], mod[0], sel_all, wr, br, moe_w_gate, moe_w_up, moe_w_down, 0, final_g, perm, False)

    w_in1 = jnp.concatenate([mla_w_in[0], _rot_cols(mla_w_in[0][:, Q_RANK + KV_RANK:])], axis=1).astype(BF16)
    (a1,) = _norm_proj(h, norm1_g[1], mod[1], sel_all, w_in1, perm, (w_in1.shape[1],), (F32,), 1)
    wq = mla_w_uq[0].reshape(Q_RANK, MLA_HEADS, QK_NOPE + QK_ROPE)
    wq = jnp.concatenate([wq, _rot_cols(wq[..., QK_NOPE:])], axis=-1).reshape(Q_RANK, -1).astype(BF16)
    cos, sin = _rope_tables(lc, l)
    qn, qp, kn, vv, kp = _mla_proj(a1, mla_q_norm_g[0][None], mla_kv_norm_g[0][None], wq,
                                   mla_w_ukv[0].astype(BF16), cos, sin, nct)
    o = _attention(qn, qp, kn, kp, vv)
    h = _oproj(o, perm_t, mla_w_o[0].astype(BF16), h, mod[1], lambda t: t + nct, sel_lat)
    wr, br = router_weights(1)
    return _moe(h, norm2_g[1], mod[1], sel_lat, wr, br, moe_w_gate, moe_w_up, moe_w_down, 1, final_g, perm, True)
```

```python
import functools

import jax
import jax.numpy as jnp
import numpy as np
from jax import lax
from jax.experimental import pallas as pl
from jax.experimental.pallas import tpu as pltpu

F32 = jnp.float32
BF16 = jnp.bfloat16
HIGHEST = lax.Precision.HIGHEST

NB = 8
TM = 256
RET_TQ = 256
TQ_ATTN = 256
TMOE = 512
EPS = 1e-6
GN_EPS = 64e-5
GRID_W = 64
ROPE_BASE = 10000.0

RET_HEADS, RET_DK, RET_DV = 4, 64, 128
RWKV_HEADS, RWKV_N = 8, 64
RWKV_W = RWKV_HEADS * RWKV_N
MLA_HEADS, Q_RANK, KV_RANK, QK_NOPE, QK_ROPE, V_HEAD = 8, 384, 256, 128, 64, 128
MLA_SCALE = (QK_NOPE + QK_ROPE) ** -0.5
N_GROUPS, EXPERTS_PER_GROUP = 4, 8
N_EXPERTS = N_GROUPS * EXPERTS_PER_GROUP
N_MOD = 6
MOD_ROWS = 16


def _cparams(sem):
    return pltpu.CompilerParams(dimension_semantics=sem)


def _rows8(x, fn):
    tm, d = x.shape
    return fn(x.reshape(tm // NB, NB, d)).reshape(tm, d)


def _norm_mod(x, g, shift8, scale8):
    var = jnp.mean(x * x, axis=-1, keepdims=True)
    y = x * lax.rsqrt(var + EPS) * g
    return _rows8(y, lambda y3: y3 * (1.0 + scale8[None]) + shift8[None])


def _gated_residual(h, gate8, y):
    return h + _rows8(y, lambda y3: y3 * gate8[None])


def _mod_spec(d, sel):
    return pl.BlockSpec((None, N_MOD, NB, d), lambda t: (sel(t), 0, 0, 0))


def _split_bf16(x):
    hi = x.astype(BF16)
    return hi, (x - hi.astype(F32)).astype(BF16)


def _seg_sum(x, ones):
    hi, lo = _split_bf16(x)
    return jnp.dot(hi, ones, preferred_element_type=F32) + jnp.dot(lo, ones, preferred_element_type=F32)


def _seg_sum_left(sel, x):
    hi, lo = _split_bf16(x)
    return jnp.dot(sel, hi, preferred_element_type=F32) + jnp.dot(sel, lo, preferred_element_type=F32)


def _seg_sum_left3(sel, x):
    hi, lo = _split_bf16(x)
    rest = (x - hi.astype(F32)) - lo.astype(F32)
    return (jnp.dot(sel, hi, preferred_element_type=F32) + jnp.dot(sel, lo, preferred_element_type=F32)
            + jnp.dot(sel, rest.astype(BF16), preferred_element_type=F32))


def _dot3(x, w_ref):
    hi, lo = _split_bf16(x)
    w_hi = w_ref[0]
    return (jnp.dot(hi, w_hi, preferred_element_type=F32) + jnp.dot(lo, w_hi, preferred_element_type=F32)
            + jnp.dot(hi, w_ref[1], preferred_element_type=F32))


def _ada_kernel(c_ref, w_ref, b_ref, o_ref):
    s = c_ref[...]
    s = s * jax.nn.sigmoid(s)
    o_ref[...] = jnp.dot(s, w_ref[...], precision=HIGHEST, preferred_element_type=F32) + b_ref[...]


def _ada_table(c_all, ada_w, ada_b):
    depth, d, nd = ada_w.shape
    out = pl.pallas_call(
        _ada_kernel,
        grid=(depth, nd // d),
        in_specs=[pl.BlockSpec((MOD_ROWS, d), lambda l, j: (0, 0)),
                  pl.BlockSpec((None, d, d), lambda l, j: (l, 0, j)),
                  pl.BlockSpec((None, 1, d), lambda l, j: (l, 0, j))],
        out_specs=pl.BlockSpec((None, MOD_ROWS, d), lambda l, j: (l, 0, j)),
        out_shape=jax.ShapeDtypeStruct((depth, MOD_ROWS, nd), F32),
        compiler_params=_cparams(("arbitrary", "arbitrary")),
        name="ada_table",
    )(c_all, ada_w, ada_b.reshape(depth, 1, nd))
    return out.reshape(depth, MOD_ROWS, N_MOD, d)


def _proj_kernel(x_ref, g_ref, mod_ref, w_ref, p_ref, *refs, splits, nperm, cast_w):
    o_refs = refs[:len(splits)]
    if cast_w:
        w_f32, w_ref = w_ref, refs[-1]

        @pl.when(pl.program_id(0) == 0)
        def _():
            raw = w_f32.shape[1]
            w_ref[:, :raw] = w_f32[...].astype(BF16)
            if raw < w_ref.shape[1]:
                w_ref[:, raw:] = jnp.zeros((w_ref.shape[0], w_ref.shape[1] - raw), BF16)

    xm = _norm_mod(x_ref[...], g_ref[...], mod_ref[0], mod_ref[1]).astype(BF16)
    tm = xm.shape[0]
    xp = jnp.dot(p_ref[...], xm, preferred_element_type=F32).astype(BF16) if nperm else None
    off = 0
    for idx, (o_ref, n) in enumerate(zip(o_refs, splits)):
        for j in range(0, n, 512):
            c = min(512, n - j)
            w = w_ref[:, off + j:off + j + c]
            if idx < nperm:
                res = jnp.dot(xp, w, preferred_element_type=F32).astype(o_ref.dtype)
                o_ref[:, :, j:j + c] = res.reshape(NB, tm // NB, c)
            else:
                o_ref[:, j:j + c] = jnp.dot(xm, w, preferred_element_type=F32).astype(o_ref.dtype)
        off += n


def _norm_proj(h, g, mod, sel, w, perm, splits, dtypes, nperm):
    n, d = h.shape
    nout = w.shape[1]
    cast_w = w.dtype != BF16
    steps = TM // NB
    out_specs, out_shape = [], []
    for idx, (s, dt) in enumerate(zip(splits, dtypes)):
        if idx < nperm:
            out_specs.append(pl.BlockSpec((NB, steps, s), lambda t: (0, t, 0)))
            out_shape.append(jax.ShapeDtypeStruct((NB, n // NB, s), dt))
        else:
            out_specs.append(pl.BlockSpec((TM, s), lambda t: (t, 0)))
            out_shape.append(jax.ShapeDtypeStruct((n, s), dt))
    return pl.pallas_call(
        functools.partial(_proj_kernel, splits=splits, nperm=nperm, cast_w=cast_w),
        grid=(n // TM,),
        in_specs=[pl.BlockSpec((TM, d), lambda t: (t, 0)),
                  pl.BlockSpec((1, d), lambda t: (0, 0)),
                  _mod_spec(d, sel),
                  pl.BlockSpec((d, nout), lambda t: (0, 0)),
                  pl.BlockSpec(perm.shape, lambda t: (0, 0))],
        out_specs=out_specs,
        out_shape=out_shape,
        scratch_shapes=[pltpu.VMEM((d, sum(splits)), BF16)] if cast_w else [],
        compiler_params=_cparams(("arbitrary",)),
        name="norm_proj",
    )(h, g.reshape(1, d), mod, w, perm)


def _ret_kernel(lg_ref, q_ref, k_ref, v_ref, o_ref, g_ref, *, lc, l, tq):
    nct = lc // tq
    nk = (lc + l) // tq
    dn = (((1,), (1,)), ((), ()))
    scale = RET_DK ** -0.5

    rel = (lax.broadcasted_iota(jnp.int32, (tq, tq), 0)
           - lax.broadcasted_iota(jnp.int32, (tq, tq), 1)).astype(F32)
    for hh in range(2):
        lg = lg_ref[hh][0:1, 0:1]
        g_ref[hh, 0] = jnp.exp(lg * rel)
        g_ref[hh, 1] = jnp.exp(-(lg * rel))
        g_ref[hh, 2] = jnp.exp(lg * jnp.abs(rel))

    k2 = k_ref[...]
    v2 = v_ref[...]

    def tile(qi, carry):
        rows = pl.ds(pl.multiple_of(qi * tq, tq), tq)
        q_lat = qi >= nct
        q2 = q_ref[rows, :]
        for hh in range(2):
            lg = lg_ref[hh][0:1, 0:1]
            s = lax.dot_general(q2[:, hh * RET_DK:(hh + 1) * RET_DK], k2[:, hh * RET_DK:(hh + 1) * RET_DK], dn,
                                preferred_element_type=F32)
            pieces = []
            for kj in range(nk):
                d = qi - kj
                idx = jnp.where(d > 0, 0, jnp.where(d < 0, 1, 2))
                dabs = jnp.full((1, 1), jnp.abs(d) * tq, jnp.int32).astype(F32)
                sig = jnp.exp(lg * dabs) * scale
                if kj < nct:
                    dback = jnp.full((1, 1), l + lc - d * tq, jnp.int32).astype(F32)
                    sig2 = jnp.where(q_lat, jnp.exp(lg * dback) * scale, 0.0)
                    m = g_ref[hh, idx] * sig + g_ref[hh, 1] * sig2
                else:
                    m = g_ref[hh, idx] * jnp.where(q_lat, sig, 0.0)
                pieces.append((s[:, kj * tq:(kj + 1) * tq] * m).astype(BF16))
            p = jnp.concatenate(pieces, axis=1)
            o_ref[rows, hh * RET_DV:(hh + 1) * RET_DV] = jnp.dot(p, v2[:, hh * RET_DV:(hh + 1) * RET_DV],
                                                                preferred_element_type=F32)
        return carry

    lax.fori_loop(0, nk, tile, 0, unroll=3)


def _retention(qkv, lg, lc, l):
    b, s, _ = qkv.shape
    tq = RET_TQ if lc % RET_TQ == 0 else 128
    qk_w = 2 * RET_DK
    v_w = 2 * RET_DV
    k_blk0 = RET_HEADS * RET_DK // qk_w
    v_blk0 = 2 * RET_HEADS * RET_DK // v_w
    return pl.pallas_call(
        functools.partial(_ret_kernel, lc=lc, l=l, tq=tq),
        grid=(b, RET_HEADS // 2),
        in_specs=[pl.BlockSpec((2, 1, 128), lambda bi, hp: (hp, 0, 0)),
                  pl.BlockSpec((None, s, qk_w), lambda bi, hp: (bi, 0, hp)),
                  pl.BlockSpec((None, s, qk_w), lambda bi, hp: (bi, 0, k_blk0 + hp)),
                  pl.BlockSpec((None, s, v_w), lambda bi, hp: (bi, 0, v_blk0 + hp))],
        out_specs=pl.BlockSpec((None, s, v_w), lambda bi, hp: (bi, 0, hp)),
        out_shape=jax.ShapeDtypeStruct((b, s, RET_HEADS * RET_DV), F32),
        scratch_shapes=[pltpu.VMEM((2, 3, tq, tq), F32)],
        compiler_params=_cparams(("arbitrary", "arbitrary")),
        name="retention",
    )(lg, qkv, qkv, qkv)


def _head_pairs(qa, qb):
    tm = qa.shape[0]
    lo = lax.broadcasted_iota(jnp.int32, (tm, 128), 1) < RWKV_N
    out = []
    for c in range(RWKV_W // 128):
        a = qa[:, c * 128:(c + 1) * 128]
        b = qb[:, c * 128:(c + 1) * 128]
        out.append(jnp.where(lo, a, pltpu.roll(b, RWKV_N, 1)))
        out.append(jnp.where(lo, pltpu.roll(a, RWKV_N, 1), b))
    return out


def _feat_kernel(rw_ref, prev_ref, next_ref, mu_ref, w0_ref, w2_ref, a0_ref, a2_ref, g2_ref,
                 kkw_ref, ka_ref, rk_ref, ones_ref,
                 p1_o, p2f_o, p2b_o, p3f_o, p3b_o, gate_o, bonus_o, *, nct, ntile):
    t = pl.program_id(0)
    first = jnp.logical_or(t == 0, t == nct)
    last = jnp.logical_or(t == nct - 1, t == ntile - 1)
    y = rw_ref[...]
    tm, wid = y.shape
    grp = lax.broadcasted_iota(jnp.int32, (tm // NB, 1, 1), 0)
    prow = jnp.where(first, 0.0, prev_ref[...])
    nrow = jnp.where(last, 0.0, next_ref[...])
    prev = jnp.where(grp == 0, prow[None], pltpu.roll(y, NB, 0).reshape(tm // NB, NB, wid)).reshape(tm, wid)
    nxt = jnp.where(grp == tm // NB - 1, nrow[None],
                    pltpu.roll(y, tm - NB, 0).reshape(tm // NB, NB, wid)).reshape(tm, wid)
    ys = y + mu_ref[0:1, :] * (prev - y) + mu_ref[1:2, :] * (nxt - y)

    w_ = RWKV_W
    r = ys[:, 0:w_]
    kr = ys[:, w_:2 * w_]
    vr = ys[:, 2 * w_:3 * w_]
    wd = ys[:, 3 * w_:3 * w_ + 128]
    ad = ys[:, 3 * w_ + 128:3 * w_ + 256]
    gd = ys[:, 3 * w_ + 256:3 * w_ + 512]
    ones = ones_ref[...]

    kk = kr * kkw_ref[...]
    ss = _seg_sum(kk * kk, ones)
    kk = kk / jnp.maximum(jnp.sqrt(ss), 1e-12)
    zw = w0_ref[...] + _dot3(jnp.tanh(wd), w2_ref)
    decay = jnp.exp(-(float(np.exp(-0.5)) * jax.nn.sigmoid(zw)))
    a = jax.nn.sigmoid(a0_ref[...] + _dot3(ad, a2_ref))
    gate_o[...] = _dot3(jax.nn.sigmoid(gd), g2_ref)
    ka = ka_ref[...]

    def emit(o_ref, qa, qb):
        for hh, slab in enumerate(_head_pairs(qa, qb)):
            o_ref[hh] = slab

    emit(p1_o, kk, r)
    ktsum = None
    for d, (p2_o, p3_o) in enumerate(((p2f_o, p3f_o), (p2b_o, p3b_o))):
        a_d = a[:, d * w_:(d + 1) * w_]
        kt = kr * (1.0 + (a_d - 1.0) * ka)
        emit(p2_o, decay[:, d * w_:(d + 1) * w_], kk * a_d)
        emit(p3_o, kt, vr)
        ktsum = kt if ktsum is None else ktsum + kt
    bonus_o[...] = _seg_sum(r * ktsum * rk_ref[...], ones) * vr


def _rwkv_features(rw, params, nct):
    n, wid = rw.shape
    ntile = n // TM
    rb = TM // 8
    nrb = n // 8
    w_ = RWKV_W
    tok = pl.BlockSpec((TM, w_), lambda t: (t, 0))
    pair = pl.BlockSpec((RWKV_HEADS, TM, 128), lambda t: (0, t, 0))

    def full(a):
        return pl.BlockSpec(a.shape, lambda t: (0,) * a.ndim)

    one = jax.ShapeDtypeStruct((n, w_), F32)
    pshape = jax.ShapeDtypeStruct((RWKV_HEADS, n, 128), F32)
    return pl.pallas_call(
        functools.partial(_feat_kernel, nct=nct, ntile=ntile),
        grid=(ntile,),
        in_specs=[pl.BlockSpec((TM, wid), lambda t: (t, 0)),
                  pl.BlockSpec((8, wid), lambda t: (jnp.maximum(t * rb - 1, 0), 0)),
                  pl.BlockSpec((8, wid), lambda t: (jnp.minimum((t + 1) * rb, nrb - 1), 0))]
                 + [full(a) for a in params],
        out_specs=[pair] * 5 + [tok, tok],
        out_shape=[pshape] * 5 + [one, one],
        compiler_params=_cparams(("arbitrary",)),
        name="rwkv_features",
    )(rw, rw, rw, *params)


def _scan_kernel(p1f, p1b, p2f, p2b, p3f, p3b, yf_ref, yb_ref, s_ref, t_ref, y_buf):
    @pl.when(pl.program_id(0) == 0)
    def _():
        s_ref[...] = jnp.zeros_like(s_ref)
        y_buf[...] = jnp.zeros_like(y_buf)

    nkey = s_ref.shape[0]
    nh = p1f.shape[0]
    tc = p1f.shape[1] // NB
    half = NB * nh
    lanes = 2 * half
    kblk = 32
    pairs = ((p1f, p1b), (p2f, p2b), (p3f, p3b))
    kk_q, r_q, w_q, kka_q, kt_q, v_q = (0, 0), (0, nkey), (1, 0), (1, nkey), (2, 0), (2, nkey)

    def rows_of(i):
        return pl.ds(pl.multiple_of(i * NB, NB), NB)

    def relayout(i, dst):
        for p, (f_ref, b_ref) in enumerate(pairs):
            rows = ([f_ref[hh, rows_of(i), :] for hh in range(nh)]
                    + [b_ref[hh, rows_of(tc - 1 - i), :] for hh in range(nh)])
            dst[p] = jnp.concatenate(rows, axis=0).T

    def emit(i):
        y = y_buf[...].reshape(nkey, lanes)
        yt = jnp.concatenate([y, y], axis=0).T
        for hh in range(nh):
            yf_ref[hh, rows_of(i), :] = yt[hh * NB:(hh + 1) * NB]
            yb_ref[hh, rows_of(tc - 1 - i), :] = yt[half + hh * NB:half + (hh + 1) * NB]

    def step(i, cur, nxt):
        def row(q, k):
            return cur[q[0], pl.ds(q[1] + k, 1), :][None]

        sa = s_ref[0] * row(kk_q, 0)
        for k in range(1, nkey):
            sa = sa + s_ref[k] * row(kk_q, k)
        relayout(jnp.minimum(i + 1, tc - 1), nxt)
        emit(jnp.maximum(i - 1, 0))
        v = cur[v_q[0], v_q[1]:v_q[1] + nkey, :].reshape(nkey // 8, 8, lanes)

        def upd_body(kb, y):
            base = pl.multiple_of(kb * kblk, kblk)
            for j in range(kblk):
                k = base + j
                s_new = s_ref[k] * row(w_q, k) + (v * row(kt_q, k) - sa * row(kka_q, k))
                s_ref[k] = s_new
                y = y + s_new * row(r_q, k)
            return y

        y_buf[...] = lax.fori_loop(0, nkey // kblk, upd_body, jnp.zeros((nkey // 8, 8, lanes), F32))

    relayout(0, t_ref.at[0])

    def two_steps(j, carry):
        step(2 * j, t_ref.at[0], t_ref.at[1])
        step(2 * j + 1, t_ref.at[1], t_ref.at[0])
        return carry

    lax.fori_loop(0, tc // 2, two_steps, 0)
    emit(tc - 1)


def _rwkv_scan(p1, p2f, p2b, p3f, p3b, nct):
    nh, n, _ = p1.shape
    nkey = RWKV_N
    ntb = n // TM

    def mirror(g):
        return jnp.where(g < nct, nct - 1 - g, nct + ntb - 1 - g)

    sf = pl.BlockSpec((nh, TM, 128), lambda g: (0, g, 0))
    sb = pl.BlockSpec((nh, TM, 128), lambda g: (0, mirror(g), 0))
    out = jax.ShapeDtypeStruct((nh, n, 128), F32)
    lanes = 2 * NB * nh
    return pl.pallas_call(
        _scan_kernel,
        grid=(ntb,),
        in_specs=[sf, sb, sf, sb, sf, sb],
        out_specs=[sf, sb],
        out_shape=[out, out],
        scratch_shapes=[pltpu.VMEM((nkey, nkey // 8, 8, lanes), F32),
                        pltpu.VMEM((2, 3, 2 * nkey, lanes), F32),
                        pltpu.VMEM((nkey // 8, 8, lanes), F32)],
        compiler_params=_cparams(("arbitrary",)),
        name="rwkv_scan",
    )(p1, p1, p2f, p2b, p3f, p3b)


def _even_out_kernel(o_ref, g_ref, yf_ref, yb_ref, bonus_ref, gate_ref, lnw_ref, lnb_ref, ones_ref, pt_ref,
                     w_ref, h_ref, mod_ref, out_ref):
    tm = h_ref.shape[0]
    o_all = _seg_sum_left(pt_ref[...], o_ref[...].reshape(tm, RET_HEADS * RET_DV))
    parts = []
    for hh in range(RET_HEADS):
        o = o_all[:, hh * RET_DV:(hh + 1) * RET_DV]
        o = o * lax.rsqrt(jnp.mean(o * o, axis=-1, keepdims=True) + EPS)
        gg = g_ref[:, hh * RET_DV:(hh + 1) * RET_DV]
        parts.append(o * (gg * jax.nn.sigmoid(gg)))
    ones = ones_ref[...]
    lo = lax.broadcasted_iota(jnp.int32, (tm, 128), 1) < RWKV_N
    chunks = []
    for c in range(RWKV_HEADS // 2):
        even = yf_ref[2 * c] + yb_ref[2 * c]
        odd = yf_ref[2 * c + 1] + yb_ref[2 * c + 1]
        chunks.append(jnp.where(lo, even, pltpu.roll(odd, RWKV_N, 1)))
    y = jnp.concatenate(chunks, axis=-1)
    mu = _seg_sum(y, ones) * (1.0 / RWKV_N)
    yc = y - mu
    var = _seg_sum(yc * yc, ones) * (1.0 / RWKV_N)
    yn = yc * lax.rsqrt(var + GN_EPS) * lnw_ref[...] + lnb_ref[...]
    parts.append((yn + bonus_ref[...]) * gate_ref[...])
    cat = jnp.concatenate(parts, axis=-1).astype(BF16)
    mix = jnp.dot(cat, w_ref[...], preferred_element_type=F32)
    out_ref[...] = _gated_residual(h_ref[...], mod_ref[2], mix)


def _even_out(o_ret, g, y2, bonus, gate, lnw, lnb, ones, perm_t, w_out, h, mod, sel):
    n, d = h.shape
    w_ = RWKV_W
    tok = pl.BlockSpec((TM, w_), lambda t: (t, 0))
    ysp = pl.BlockSpec((RWKV_HEADS, TM, 128), lambda t: (0, t, 0))
    osp = pl.BlockSpec((NB, TM // NB, o_ret.shape[2]), lambda t: (0, t, 0))

    def full(a):
        return pl.BlockSpec(a.shape, lambda t: (0,) * a.ndim)

    return pl.pallas_call(
        _even_out_kernel,
        grid=(n // TM,),
        in_specs=[osp, tok, ysp, ysp, tok, tok, full(lnw), full(lnb), full(ones), full(perm_t), full(w_out),
                  pl.BlockSpec((TM, d), lambda t: (t, 0)),
                  _mod_spec(d, sel)],
        out_specs=pl.BlockSpec((TM, d), lambda t: (t, 0)),
        out_shape=jax.ShapeDtypeStruct((n, d), F32),
        compiler_params=_cparams(("arbitrary",)),
        name="even_out",
    )(o_ret, g, y2[0], y2[1], bonus, gate, lnw, lnb, ones, perm_t, w_out, h, mod)


def _mla_proj_kernel(a_ref, qg_ref, kvg_ref, wuq_ref, wukv_ref, cos_ref, sin_ref,
                     qn_o, qp_o, kn_o, v_o, kp_o):
    nb, steps, wid = a_ref.shape
    tm = nb * steps
    a = a_ref[...].reshape(tm, wid)
    cos = jnp.broadcast_to(cos_ref[...][None], (nb, steps, QK_ROPE)).reshape(tm, QK_ROPE)
    sin = jnp.broadcast_to(sin_ref[...][None], (nb, steps, QK_ROPE)).reshape(tm, QK_ROPE)
    zpad = jnp.zeros((tm, 128 - QK_ROPE), F32)

    def rms(t, g):
        return (t * lax.rsqrt(jnp.mean(t * t, axis=-1, keepdims=True) + EPS) * g).astype(BF16)

    def pad128(pe):
        return jnp.concatenate([pe, zpad], axis=1).astype(BF16).reshape(nb, steps, 128)

    cq = rms(a[:, :Q_RANK], qg_ref[...])
    ckv = rms(a[:, Q_RANK:Q_RANK + KV_RANK], kvg_ref[...])
    pe0 = Q_RANK + KV_RANK
    kp_o[...] = pad128(a[:, pe0:pe0 + QK_ROPE] * cos + a[:, pe0 + QK_ROPE:pe0 + 2 * QK_ROPE] * sin)
    hw = QK_NOPE + 2 * QK_ROPE
    for hh in range(MLA_HEADS):
        hs = slice(hh * 128, (hh + 1) * 128)
        qh = jnp.dot(cq, wuq_ref[:, hh * hw:(hh + 1) * hw], preferred_element_type=F32)
        qn_o[:, :, hs] = (qh[:, :QK_NOPE] * MLA_SCALE).astype(BF16).reshape(nb, steps, 128)
        qp_o[:, :, hs] = pad128(
            (qh[:, QK_NOPE:QK_NOPE + QK_ROPE] * cos + qh[:, QK_NOPE + QK_ROPE:] * sin) * MLA_SCALE)
        kvh = jnp.dot(ckv, wukv_ref[:, hh * hw:(hh + 1) * hw], preferred_element_type=F32)
        kn_o[:, :, hs] = kvh[:, :QK_NOPE].astype(BF16).reshape(nb, steps, 128)
        v_o[:, :, hs] = kvh[:, QK_NOPE:].astype(BF16).reshape(nb, steps, 128)


def _mla_proj(a, qg, kvg, wuq, wukv, cos, sin, nct):
    nb, s, wid = a.shape
    hw = MLA_HEADS * 128
    steps = TM // NB
    ntile = s // steps
    l = (ntile - nct) * steps

    def full(x):
        return pl.BlockSpec(x.shape, lambda t: (0,) * x.ndim)

    def lat_blk(t):
        return jnp.maximum(t - nct, 0)

    allrows = pl.BlockSpec((nb, steps, hw), lambda t: (0, t, 0))
    latrows = pl.BlockSpec((nb, steps, hw), lambda t: (0, lat_blk(t), 0))
    rope = pl.BlockSpec((steps, QK_ROPE), lambda t: (t, 0))
    return pl.pallas_call(
        _mla_proj_kernel,
        grid=(ntile,),
        in_specs=[pl.BlockSpec((nb, steps, wid), lambda t: (0, t, 0)), full(qg), full(kvg), full(wuq),
                  full(wukv), rope, rope],
        out_specs=[latrows, latrows, allrows, allrows, pl.BlockSpec((nb, steps, 128), lambda t: (0, t, 0))],
        out_shape=[jax.ShapeDtypeStruct((nb, l, hw), BF16),
                   jax.ShapeDtypeStruct((nb, l, hw), BF16),
                   jax.ShapeDtypeStruct((nb, s, hw), BF16),
                   jax.ShapeDtypeStruct((nb, s, hw), BF16),
                   jax.ShapeDtypeStruct((nb, s, 128), BF16)],
        compiler_params=_cparams(("arbitrary",)),
        name="mla_proj",
    )(a, qg, kvg, wuq, wukv, cos, sin)


def _attn_kernel(qn_ref, qp_ref, kn_ref, kp_ref, v_ref, o_ref, *, tq):
    hi = pl.program_id(1)
    dn = (((1,), (1,)), ((), ()))
    hsl = pl.ds(pl.multiple_of(hi * 128, 128), 128)
    k = jnp.concatenate([kn_ref[:, hsl], kp_ref[...]], axis=1)
    v = v_ref[:, hsl]

    def tile(i, carry):
        rows = pl.ds(pl.multiple_of(i * tq, tq), tq)
        q = jnp.concatenate([qn_ref[rows, :], qp_ref[rows, :]], axis=1)
        s = lax.dot_general(q, k, dn, preferred_element_type=F32)
        m = jnp.max(s, axis=-1, keepdims=True)
        p = jnp.exp(s - m)
        l = jnp.sum(p, axis=-1, keepdims=True)
        o = jnp.dot(p.astype(BF16), v, preferred_element_type=F32)
        o_ref[rows, :] = (o / l).astype(o_ref.dtype)
        return carry

    lax.fori_loop(0, qn_ref.shape[0] // tq, tile, 0, unroll=8)


def _attention(qn, qp, kn, kp, v):
    b, l, hw = qn.shape
    s = kn.shape[1]
    hd = MLA_HEADS
    tq = TQ_ATTN if l % TQ_ATTN == 0 else 128
    qspec = pl.BlockSpec((None, l, 128), lambda bi, hi: (bi, 0, hi))
    kvspec = pl.BlockSpec((None, s, hw), lambda bi, hi: (bi, 0, 0))
    return pl.pallas_call(
        functools.partial(_attn_kernel, tq=tq),
        grid=(b, hd),
        in_specs=[qspec, qspec, kvspec,
                  pl.BlockSpec((None, s, 128), lambda bi, hi: (bi, 0, 0)),
                  kvspec],
        out_specs=qspec,
        out_shape=jax.ShapeDtypeStruct((b, l, hw), BF16),
        compiler_params=_cparams(("arbitrary", "arbitrary")),
        name="mla_attention",
    )(qn, qp, kn, kp, v)


def _oproj_kernel(o_ref, pt_ref, w_ref, h_ref, mod_ref, out_ref):
    nb, steps, wid = o_ref.shape
    o = jnp.dot(pt_ref[...], o_ref[...].reshape(nb * steps, wid), preferred_element_type=F32).astype(BF16)
    mix = jnp.dot(o, w_ref[...], preferred_element_type=F32)
    out_ref[...] = _gated_residual(h_ref[...], mod_ref[2], mix)


def _oproj(o, perm_t, w_o, h, mod, hrow, sel):
    nb, l, wid = o.shape
    d = h.shape[1]
    steps = TM // NB
    return pl.pallas_call(
        _oproj_kernel,
        grid=(l // steps,),
        in_specs=[pl.BlockSpec((nb, steps, wid), lambda t: (0, t, 0)),
                  pl.BlockSpec(perm_t.shape, lambda t: (0, 0)),
                  pl.BlockSpec(w_o.shape, lambda t: (0, 0)),
                  pl.BlockSpec((TM, d), lambda t: (hrow(t), 0)),
                  _mod_spec(d, sel)],
        out_specs=pl.BlockSpec((TM, d), lambda t: (t, 0)),
        out_shape=jax.ShapeDtypeStruct((nb * l, d), F32),
        compiler_params=_cparams(("arbitrary",)),
        name="mla_oproj",
    )(o, perm_t, w_o, h, mod)


def _route_kernel(h_ref, g_ref, mod_ref, wr_ref, br_ref, xl_ref, route_ref):
    xl = _norm_mod(h_ref[...], g_ref[...], mod_ref[3], mod_ref[4])
    xl_ref[...] = xl
    logits = jnp.dot(xl, wr_ref[...], precision=HIGHEST, preferred_element_type=F32) + br_ref[...]
    lane_i = lax.broadcasted_iota(jnp.int32, logits.shape, 1)
    lane = lane_i.astype(F32)
    lane_grp = (lane_i >> 3).astype(F32)
    neg = -jnp.inf
    big = 1e6
    gl = jnp.where(jnp.logical_and(lane_i >= N_EXPERTS, lane_i < N_EXPERTS + N_GROUPS), logits, neg)
    gmax = jnp.max(gl, axis=-1, keepdims=True)
    gsum = jnp.sum(jnp.exp(gl - gmax), axis=-1, keepdims=True)
    pg = 1.0 / gsum
    gidx = jnp.min(jnp.where(gl == gmax, lane - N_EXPERTS, big), axis=-1, keepdims=True)
    in_grp = jnp.logical_and(lane_i < N_EXPERTS, lane_grp == gidx)
    el = jnp.where(in_grp, logits, neg)
    emax = jnp.max(el, axis=-1, keepdims=True)
    esum = jnp.sum(jnp.exp(el - emax), axis=-1, keepdims=True)
    i1 = jnp.min(jnp.where(el == emax, lane, big), axis=-1, keepdims=True)
    el2 = jnp.where(lane == i1, neg, el)
    emax2 = jnp.max(el2, axis=-1, keepdims=True)
    i2 = jnp.min(jnp.where(el2 == emax2, lane, big), axis=-1, keepdims=True)
    pe1 = 1.0 / esum
    pe2 = jnp.exp(emax2 - emax) / esum
    den = pe1 + pe2
    w1 = pg * pe1 / den
    w2 = pg * pe2 / den
    route_ref[...] = jnp.where(lane_i == 0, i1,
                               jnp.where(lane_i == 1, i2,
                                         jnp.where(lane_i == 2, w1, jnp.where(lane_i == 3, w2, 0.0))))


def _route(h, g, mod, sel, wr, br):
    n, d = h.shape
    return pl.pallas_call(
        _route_kernel,
        grid=(n // TM,),
        in_specs=[pl.BlockSpec((TM, d), lambda t: (t, 0)),
                  pl.BlockSpec((1, d), lambda t: (0, 0)),
                  _mod_spec(d, sel),
                  pl.BlockSpec(wr.shape, lambda t: (0, 0)),
                  pl.BlockSpec(br.shape, lambda t: (0, 0))],
        out_specs=[pl.BlockSpec((TM, d), lambda t: (t, 0)),
                   pl.BlockSpec((TM, 128), lambda t: (t, 0))],
        out_shape=[jax.ShapeDtypeStruct((n, d), F32), jax.ShapeDtypeStruct((n, 128), F32)],
        compiler_params=_cparams(("arbitrary",)),
        name="moe_route",
    )(h, g.reshape(1, d), mod, wr, br)


def _expert_kernel(te_ref, nt_ref, x_ref, wg_ref, wu_ref, wd_ref, o_ref, wg_s, wu_s, wd_s):
    t = pl.program_id(0)

    @pl.when(t < nt_ref[0])
    def _():
        changed = jnp.logical_or(t == 0, te_ref[t] != te_ref[jnp.maximum(t - 1, 0)])

        @pl.when(changed)
        def _():
            wg_s[...] = wg_ref[...].astype(BF16)
            wu_s[...] = wu_ref[...].astype(BF16)
            wd_s[...] = wd_ref[...].astype(BF16)

        x = x_ref[...].astype(BF16)
        h1 = jnp.dot(x, wg_s[...], preferred_element_type=F32)
        h2 = jnp.dot(x, wu_s[...], preferred_element_type=F32)
        hid = ((h1 * jax.nn.sigmoid(h1)) * h2).astype(BF16)
        o_ref[...] = jnp.dot(hid, wd_s[...], preferred_element_type=F32)

    @pl.when(t >= nt_ref[0])
    def _():
        o_ref[...] = jnp.zeros_like(o_ref)


def _experts(tile_e, ntiles, xs, w_gate, w_up, w_down, layer):
    rows, d = xs.shape
    hid = w_gate.shape[-1]
    grid_spec = pltpu.PrefetchScalarGridSpec(
        num_scalar_prefetch=2,
        grid=(rows // TMOE,),
        in_specs=[pl.BlockSpec((TMOE, d), lambda t, te, nt: (jnp.minimum(t, nt[0] - 1), 0)),
                  pl.BlockSpec((None, None, d, hid), lambda t, te, nt: (layer, te[t], 0, 0)),
                  pl.BlockSpec((None, None, d, hid), lambda t, te, nt: (layer, te[t], 0, 0)),
                  pl.BlockSpec((None, None, hid, d), lambda t, te, nt: (layer, te[t], 0, 0))],
        out_specs=pl.BlockSpec((TMOE, d), lambda t, te, nt: (t, 0)),
        scratch_shapes=[pltpu.VMEM((d, hid), BF16), pltpu.VMEM((d, hid), BF16), pltpu.VMEM((hid, d), BF16)],
    )
    return pl.pallas_call(
        _expert_kernel,
        grid_spec=grid_spec,
        out_shape=jax.ShapeDtypeStruct((rows, d), F32),
        compiler_params=_cparams(("arbitrary",)),
        name="moe_experts",
    )(tile_e, ntiles, xs, w_gate, w_up, w_down)


def _combine_kernel(y1_ref, y2_ref, route_ref, h_ref, mod_ref, fg_ref, p_ref, out_ref, *, final):
    route = route_ref[...]
    y = route[:, 2:3] * y1_ref[...] + route[:, 3:4] * y2_ref[...]
    hn = _gated_residual(h_ref[...], mod_ref[5], y)
    if final:
        hn = hn * lax.rsqrt(jnp.mean(hn * hn, axis=-1, keepdims=True) + EPS) * fg_ref[...]
        out_ref[...] = _seg_sum_left3(p_ref[...], hn).reshape(out_ref.shape)
    else:
        out_ref[...] = hn


def _combine(y1, y2, route, h, mod, sel, fg, perm, final):
    n, d = h.shape
    tok = pl.BlockSpec((TM, d), lambda t: (t, 0))
    if final:
        out_spec = pl.BlockSpec((NB, TM // NB, d), lambda t: (0, t, 0))
        out_shape = jax.ShapeDtypeStruct((NB, n // NB, d), F32)
    else:
        out_spec, out_shape = tok, jax.ShapeDtypeStruct((n, d), F32)
    return pl.pallas_call(
        functools.partial(_combine_kernel, final=final),
        grid=(n // TM,),
        in_specs=[tok, tok, pl.BlockSpec((TM, 128), lambda t: (t, 0)), tok,
                  _mod_spec(d, sel),
                  pl.BlockSpec((1, d), lambda t: (0, 0)),
                  pl.BlockSpec(perm.shape, lambda t: (0, 0))],
        out_specs=out_spec,
        out_shape=out_shape,
        compiler_params=_cparams(("arbitrary",)),
        name="moe_combine",
    )(y1, y2, route, h, mod, fg.reshape(1, d), perm)


def _moe(h, g, mod, sel, wr, br, w_gate, w_up, w_down, layer, fg, perm, final):
    n, d = h.shape
    xl, route = _route(h, g, mod, sel, wr, br)
    e = route[:, :2].astype(jnp.int32).reshape(-1)
    onehot = (e[:, None] == jnp.arange(N_EXPERTS, dtype=jnp.int32)[None, :]).astype(jnp.int32)
    csum = jnp.cumsum(onehot, axis=0)
    counts = csum[-1]
    rank = jnp.sum(csum * onehot, axis=1) - 1
    padded = ((counts + TMOE - 1) // TMOE) * TMOE
    pend = jnp.cumsum(padded)
    pos = (pend - padded)[e] + rank
    rows = 2 * n + N_EXPERTS * TMOE
    hit = jnp.zeros((rows,), jnp.int32).at[pos].add(jnp.arange(2 * n, dtype=jnp.int32) // 2 + 1,
                                                    mode="promise_in_bounds", unique_indices=True)
    src = jnp.where(hit > 0, hit - 1, jnp.arange(rows, dtype=jnp.int32) % n)
    ntile = rows // TMOE
    nvalid = (pend[-1] // TMOE).astype(jnp.int32)
    tstart = jnp.arange(ntile, dtype=jnp.int32) * TMOE
    tile_e = jnp.sum((tstart[:, None] >= pend[None, :]).astype(jnp.int32), axis=1)
    last_e = jnp.sum((((nvalid - 1) * TMOE) >= pend).astype(jnp.int32))
    tile_e = jnp.where(tstart < pend[-1], tile_e, last_e).astype(jnp.int32)
    xs = xl.at[src].get(mode="promise_in_bounds")
    ys = _experts(tile_e, nvalid.reshape(1), xs, w_gate, w_up, w_down, layer)
    pos2 = pos.reshape(n, 2)
    y1 = ys.at[pos2[:, 0]].get(mode="promise_in_bounds", unique_indices=True)
    y2 = ys.at[pos2[:, 1]].get(mode="promise_in_bounds", unique_indices=True)
    return _combine(y1, y2, route, h, mod, sel, fg, perm, final)


def _block_ones(width, seg):
    idx = np.arange(width) // seg
    return jnp.asarray((idx[:, None] == idx[None, :]).astype(np.float32))


def _rope_tables(lc, l):
    rows = l // GRID_W
    row = np.repeat(np.arange(rows, dtype=np.float32), GRID_W)
    col = np.tile(np.arange(GRID_W, dtype=np.float32), rows)
    n_freq = QK_ROPE // 4
    inv_freq = jnp.asarray(ROPE_BASE, F32) ** (-jnp.arange(n_freq, dtype=F32) / n_freq)
    ang_r = jnp.asarray(row)[:, None] * inv_freq
    ang_c = jnp.asarray(col)[:, None] * inv_freq
    cos = jnp.concatenate([jnp.cos(ang_r), jnp.cos(ang_r), jnp.cos(ang_c), jnp.cos(ang_c)], axis=-1)
    sin = jnp.concatenate([jnp.sin(ang_r), jnp.sin(ang_r), jnp.sin(ang_c), jnp.sin(ang_c)], axis=-1)
    cos = jnp.concatenate([jnp.ones((lc, QK_ROPE), F32), cos], axis=0)
    sin = jnp.concatenate([jnp.zeros((lc, QK_ROPE), F32), sin], axis=0)
    return cos, sin


def _batch_major_perm(tm):
    steps = tm // NB
    r_out = np.arange(tm)
    r_in = (r_out % steps) * NB + r_out // steps
    return jnp.asarray((r_in[:, None] == np.arange(tm)[None, :]).astype(np.float32)).astype(BF16)


def _rot_cols(pe):
    q = QK_ROPE // 4
    return jnp.concatenate([-pe[..., q:2 * q], pe[..., 0:q], -pe[..., 3 * q:4 * q], pe[..., 2 * q:3 * q]], axis=-1)


def _pad_cols(w, n):
    return jnp.pad(w, ((0, 0), (0, n - w.shape[1])))


def kernel(x, c, ctx, c_ctx, ada_w, ada_b, norm1_g, norm2_g, final_g, ev_w_in, ev_shift_mu, rwkv_w0, rwkv_w2, rwkv_a0, rwkv_a2, rwkv_g2, rwkv_k_k, rwkv_k_a, rwkv_r_k, rwkv_lnx_w, rwkv_lnx_b, ev_w_out, mla_w_in, mla_q_norm_g, mla_w_uq, mla_kv_norm_g, mla_w_ukv, mla_w_o, moe_w_grp, moe_b_grp, moe_w_exp, moe_b_exp, moe_w_gate, moe_w_up, moe_w_down):
    b, l, d = x.shape
    lc = ctx.shape[1]
    s = lc + l
    n = b * s
    steps = TM // NB
    nct = lc // steps
    assert b == NB and lc % steps == 0 and l % steps == 0 and ada_w.shape[0] == 2

    def sel_all(t):
        return (t >= nct).astype(jnp.int32)

    def sel_lat(t):
        return 1

    c_all = jnp.concatenate([c, c_ctx[None], jnp.zeros((MOD_ROWS - b - 1, d), F32)], axis=0)
    ada = _ada_table(c_all, ada_w, ada_b)
    mod = jnp.stack([jnp.broadcast_to(ada[:, b][:, :, None, :], (2, N_MOD, NB, d)),
                     ada[:, :b].transpose(0, 2, 1, 3)], axis=1)
    h = jnp.concatenate([ctx, x], axis=1).transpose(1, 0, 2).reshape(n, d)

    def router_weights(layer):
        wr = jnp.concatenate([moe_w_exp[layer], moe_w_grp[layer]], axis=1)
        br = jnp.concatenate([moe_b_exp[layer], moe_b_grp[layer]])[None]
        return _pad_cols(wr, 128), _pad_cols(br, 128)

    def two_term(w):
        hi = w.astype(BF16)
        return jnp.stack([hi, (w - hi.astype(F32)).astype(BF16)])

    w_ = RWKV_W
    ret_w = 2 * RET_HEADS * RET_DK + RET_HEADS * RET_DV
    perm = _batch_major_perm(TM)
    perm_t = perm.T
    qkv, gret, rw = _norm_proj(h, norm1_g[0], mod[0], sel_all, ev_w_in[0], perm,
                               (ret_w, RET_HEADS * RET_DV, 2048), (BF16, F32, F32), 1)
    lg = jnp.log1p(-jnp.exp2(-5.0 - jnp.arange(RET_HEADS, dtype=F32)))
    o_ret = _retention(qkv, jnp.broadcast_to(lg[:, None, None], (RET_HEADS, 1, 128)), lc, l)

    ones8 = _block_ones(w_, RWKV_N).astype(BF16)
    zero = jnp.zeros((64, w_), F32)
    w2bd = jnp.concatenate([jnp.concatenate([rwkv_w2[0, 0], zero], axis=1),
                            jnp.concatenate([zero, rwkv_w2[0, 1]], axis=1)], axis=0)
    a2bd = jnp.concatenate([jnp.concatenate([rwkv_a2[0, 0], zero], axis=1),
                            jnp.concatenate([zero, rwkv_a2[0, 1]], axis=1)], axis=0)
    g2p = jnp.pad(rwkv_g2[0], ((0, 256 - rwkv_g2.shape[1]), (0, 0)))
    mu = _pad_cols(ev_shift_mu[0], 2048)
    feat_params = (mu, rwkv_w0[0].reshape(1, 2 * w_), two_term(w2bd), rwkv_a0[0].reshape(1, 2 * w_),
                   two_term(a2bd), two_term(g2p),
                   rwkv_k_k[0][None], rwkv_k_a[0][None], rwkv_r_k[0].reshape(1, w_), ones8)
    p1, p2f, p2b, p3f, p3b, gate, bonus = _rwkv_features(rw, feat_params, nct)
    y2 = _rwkv_scan(p1, p2f, p2b, p3f, p3b, nct)

    h = _even_out(o_ret, gret, y2, bonus, gate, rwkv_lnx_w[0][None], rwkv_lnx_b[0][None], ones8, perm_t,
                  ev_w_out[0].astype(BF16), h, mod[0], sel_all)
    wr, br = router_weights(0)
    h = _moe(h, norm2_g[0], mod[0], sel_all, wr, br, moe_w_gate, moe_w_up, moe_w_down, 0, final_g, perm, False)

    w_in1 = jnp.concatenate([mla_w_in[0], _rot_cols(mla_w_in[0][:, Q_RANK + KV_RANK:])], axis=1).astype(BF16)
    (a1,) = _norm_proj(h, norm1_g[1], mod[1], sel_all, w_in1, perm, (w_in1.shape[1],), (F32,), 1)
    wq = mla_w_uq[0].reshape(Q_RANK, MLA_HEADS, QK_NOPE + QK_ROPE)
    wq = jnp.concatenate([wq, _rot_cols(wq[..., QK_NOPE:])], axis=-1).reshape(Q_RANK, -1).astype(BF16)
    cos, sin = _rope_tables(lc, l)
    qn, qp, kn, vv, kp = _mla_proj(a1, mla_q_norm_g[0][None], mla_kv_norm_g[0][None], wq,
                                   mla_w_ukv[0].astype(BF16), cos, sin, nct)
    o = _attention(qn, qp, kn, kp, vv)
    h = _oproj(o, perm_t, mla_w_o[0].astype(BF16), h, mod[1], lambda t: t + nct, sel_lat)
    wr, br = router_weights(1)
    return _moe(h, norm2_g[1], mod[1], sel_lat, wr, br, moe_w_gate, moe_w_up, moe_w_down, 1, final_g, perm, True)
```

```python
import functools

import jax
import jax.numpy as jnp
import numpy as np
from jax import lax
from jax.experimental import pallas as pl
from jax.experimental.pallas import tpu as pltpu

F32 = jnp.float32
BF16 = jnp.bfloat16
HIGHEST = lax.Precision.HIGHEST

NB = 8
TM = 256
RET_TQ = 256
TQ_ATTN = 256
TMOE = 512
EPS = 1e-6
GN_EPS = 64e-5
GRID_W = 64
ROPE_BASE = 10000.0

RET_HEADS, RET_DK, RET_DV = 4, 64, 128
RWKV_HEADS, RWKV_N = 8, 64
RWKV_W = RWKV_HEADS * RWKV_N
MLA_HEADS, Q_RANK, KV_RANK, QK_NOPE, QK_ROPE, V_HEAD = 8, 384, 256, 128, 64, 128
MLA_SCALE = (QK_NOPE + QK_ROPE) ** -0.5
N_GROUPS, EXPERTS_PER_GROUP = 4, 8
N_EXPERTS = N_GROUPS * EXPERTS_PER_GROUP
N_MOD = 6
MOD_ROWS = 16


def _cparams(sem):
    return pltpu.CompilerParams(dimension_semantics=sem)


def _rows8(x, fn):
    tm, d = x.shape
    return fn(x.reshape(tm // NB, NB, d)).reshape(tm, d)


def _norm_mod(x, g, shift8, scale8):
    var = jnp.mean(x * x, axis=-1, keepdims=True)
    y = x * lax.rsqrt(var + EPS) * g
    return _rows8(y, lambda y3: y3 * (1.0 + scale8[None]) + shift8[None])


def _gated_residual(h, gate8, y):
    return h + _rows8(y, lambda y3: y3 * gate8[None])


def _mod_spec(d, sel):
    return pl.BlockSpec((None, N_MOD, NB, d), lambda t: (sel(t), 0, 0, 0))


def _split_bf16(x):
    hi = x.astype(BF16)
    return hi, (x - hi.astype(F32)).astype(BF16)


def _seg_sum(x, ones):
    hi, lo = _split_bf16(x)
    return jnp.dot(hi, ones, preferred_element_type=F32) + jnp.dot(lo, ones, preferred_element_type=F32)


def _seg_sum_left(sel, x):
    hi, lo = _split_bf16(x)
    return jnp.dot(sel, hi, preferred_element_type=F32) + jnp.dot(sel, lo, preferred_element_type=F32)


def _seg_sum_left3(sel, x):
    hi, lo = _split_bf16(x)
    rest = (x - hi.astype(F32)) - lo.astype(F32)
    return (jnp.dot(sel, hi, preferred_element_type=F32) + jnp.dot(sel, lo, preferred_element_type=F32)
            + jnp.dot(sel, rest.astype(BF16), preferred_element_type=F32))


def _dot3(x, w_ref):
    hi, lo = _split_bf16(x)
    w_hi = w_ref[0]
    return (jnp.dot(hi, w_hi, preferred_element_type=F32) + jnp.dot(lo, w_hi, preferred_element_type=F32)
            + jnp.dot(hi, w_ref[1], preferred_element_type=F32))


def _ada_kernel(c_ref, w_ref, b_ref, o_ref):
    s = c_ref[...]
    s = s * jax.nn.sigmoid(s)
    o_ref[...] = jnp.dot(s, w_ref[...], precision=HIGHEST, preferred_element_type=F32) + b_ref[...]


def _ada_table(c_all, ada_w, ada_b):
    depth, d, nd = ada_w.shape
    out = pl.pallas_call(
        _ada_kernel,
        grid=(depth, nd // d),
        in_specs=[pl.BlockSpec((MOD_ROWS, d), lambda l, j: (0, 0)),
                  pl.BlockSpec((None, d, d), lambda l, j: (l, 0, j)),
                  pl.BlockSpec((None, 1, d), lambda l, j: (l, 0, j))],
        out_specs=pl.BlockSpec((None, MOD_ROWS, d), lambda l, j: (l, 0, j)),
        out_shape=jax.ShapeDtypeStruct((depth, MOD_ROWS, nd), F32),
        compiler_params=_cparams(("arbitrary", "arbitrary")),
        name="ada_table",
    )(c_all, ada_w, ada_b.reshape(depth, 1, nd))
    return out.reshape(depth, MOD_ROWS, N_MOD, d)


def _proj_kernel(x_ref, g_ref, mod_ref, w_ref, p_ref, *o_refs, splits, nperm):
    xm = _norm_mod(x_ref[...], g_ref[...], mod_ref[0], mod_ref[1]).astype(BF16)
    tm = xm.shape[0]
    xp = jnp.dot(p_ref[...], xm, preferred_element_type=F32).astype(BF16) if nperm else None
    off = 0
    for idx, (o_ref, n) in enumerate(zip(o_refs, splits)):
        for j in range(0, n, 512):
            c = min(512, n - j)
            w = w_ref[:, off + j:off + j + c]
            if idx < nperm:
                res = jnp.dot(xp, w, preferred_element_type=F32).astype(o_ref.dtype)
                o_ref[:, :, j:j + c] = res.reshape(NB, tm // NB, c)
            else:
                o_ref[:, j:j + c] = jnp.dot(xm, w, preferred_element_type=F32).astype(o_ref.dtype)
        off += n


def _norm_proj(h, g, mod, sel, w_bf16, perm, splits, dtypes, nperm):
    n, d = h.shape
    nout = w_bf16.shape[1]
    steps = TM // NB
    out_specs, out_shape = [], []
    for idx, (s, dt) in enumerate(zip(splits, dtypes)):
        if idx < nperm:
            out_specs.append(pl.BlockSpec((NB, steps, s), lambda t: (0, t, 0)))
            out_shape.append(jax.ShapeDtypeStruct((NB, n // NB, s), dt))
        else:
            out_specs.append(pl.BlockSpec((TM, s), lambda t: (t, 0)))
            out_shape.append(jax.ShapeDtypeStruct((n, s), dt))
    return pl.pallas_call(
        functools.partial(_proj_kernel, splits=splits, nperm=nperm),
        grid=(n // TM,),
        in_specs=[pl.BlockSpec((TM, d), lambda t: (t, 0)),
                  pl.BlockSpec((1, d), lambda t: (0, 0)),
                  _mod_spec(d, sel),
                  pl.BlockSpec((d, nout), lambda t: (0, 0)),
                  pl.BlockSpec(perm.shape, lambda t: (0, 0))],
        out_specs=out_specs,
        out_shape=out_shape,
        compiler_params=_cparams(("arbitrary",)),
        name="norm_proj",
    )(h, g.reshape(1, d), mod, w_bf16, perm)


def _ret_kernel(lg_ref, q_ref, k_ref, v_ref, o_ref, g_ref, *, lc, l, tq):
    nct = lc // tq
    nk = (lc + l) // tq
    dn = (((1,), (1,)), ((), ()))
    scale = RET_DK ** -0.5

    rel = (lax.broadcasted_iota(jnp.int32, (tq, tq), 0)
           - lax.broadcasted_iota(jnp.int32, (tq, tq), 1)).astype(F32)
    for hh in range(2):
        lg = lg_ref[hh][0:1, 0:1]
        g_ref[hh, 0] = jnp.exp(lg * rel)
        g_ref[hh, 1] = jnp.exp(-(lg * rel))
        g_ref[hh, 2] = jnp.exp(lg * jnp.abs(rel))

    k2 = k_ref[...]
    v2 = v_ref[...]

    def tile(qi, carry):
        rows = pl.ds(pl.multiple_of(qi * tq, tq), tq)
        q_lat = qi >= nct
        q2 = q_ref[rows, :]
        for hh in range(2):
            lg = lg_ref[hh][0:1, 0:1]
            s = lax.dot_general(q2[:, hh * RET_DK:(hh + 1) * RET_DK], k2[:, hh * RET_DK:(hh + 1) * RET_DK], dn,
                                preferred_element_type=F32)
            pieces = []
            for kj in range(nk):
                d = qi - kj
                idx = jnp.where(d > 0, 0, jnp.where(d < 0, 1, 2))
                dabs = jnp.full((1, 1), jnp.abs(d) * tq, jnp.int32).astype(F32)
                sig = jnp.exp(lg * dabs) * scale
                if kj < nct:
                    dback = jnp.full((1, 1), l + lc - d * tq, jnp.int32).astype(F32)
                    sig2 = jnp.where(q_lat, jnp.exp(lg * dback) * scale, 0.0)
                    m = g_ref[hh, idx] * sig + g_ref[hh, 1] * sig2
                else:
                    m = g_ref[hh, idx] * jnp.where(q_lat, sig, 0.0)
                pieces.append((s[:, kj * tq:(kj + 1) * tq] * m).astype(BF16))
            p = jnp.concatenate(pieces, axis=1)
            o_ref[rows, hh * RET_DV:(hh + 1) * RET_DV] = jnp.dot(p, v2[:, hh * RET_DV:(hh + 1) * RET_DV],
                                                                preferred_element_type=F32)
        return carry

    lax.fori_loop(0, nk, tile, 0, unroll=3)


def _retention(qkv, lg, lc, l):
    b, s, _ = qkv.shape
    tq = RET_TQ if lc % RET_TQ == 0 else 128
    qk_w = 2 * RET_DK
    v_w = 2 * RET_DV
    k_blk0 = RET_HEADS * RET_DK // qk_w
    v_blk0 = 2 * RET_HEADS * RET_DK // v_w
    return pl.pallas_call(
        functools.partial(_ret_kernel, lc=lc, l=l, tq=tq),
        grid=(b, RET_HEADS // 2),
        in_specs=[pl.BlockSpec((2, 1, 128), lambda bi, hp: (hp, 0, 0)),
                  pl.BlockSpec((None, s, qk_w), lambda bi, hp: (bi, 0, hp)),
                  pl.BlockSpec((None, s, qk_w), lambda bi, hp: (bi, 0, k_blk0 + hp)),
                  pl.BlockSpec((None, s, v_w), lambda bi, hp: (bi, 0, v_blk0 + hp))],
        out_specs=pl.BlockSpec((None, s, v_w), lambda bi, hp: (bi, 0, hp)),
        out_shape=jax.ShapeDtypeStruct((b, s, RET_HEADS * RET_DV), F32),
        scratch_shapes=[pltpu.VMEM((2, 3, tq, tq), F32)],
        compiler_params=_cparams(("arbitrary", "arbitrary")),
        name="retention",
    )(lg, qkv, qkv, qkv)


def _head_pairs(qa, qb):
    tm = qa.shape[0]
    lo = lax.broadcasted_iota(jnp.int32, (tm, 128), 1) < RWKV_N
    out = []
    for c in range(RWKV_W // 128):
        a = qa[:, c * 128:(c + 1) * 128]
        b = qb[:, c * 128:(c + 1) * 128]
        out.append(jnp.where(lo, a, pltpu.roll(b, RWKV_N, 1)))
        out.append(jnp.where(lo, pltpu.roll(a, RWKV_N, 1), b))
    return out


def _feat_kernel(rw_ref, prev_ref, next_ref, mu_ref, w0_ref, w2_ref, a0_ref, a2_ref, g2_ref,
                 kkw_ref, ka_ref, rk_ref, ones_ref,
                 p1_o, p2f_o, p2b_o, p3f_o, p3b_o, gate_o, bonus_o, *, nct, ntile):
    t = pl.program_id(0)
    first = jnp.logical_or(t == 0, t == nct)
    last = jnp.logical_or(t == nct - 1, t == ntile - 1)
    y = rw_ref[...]
    tm, wid = y.shape
    grp = lax.broadcasted_iota(jnp.int32, (tm // NB, 1, 1), 0)
    prow = jnp.where(first, 0.0, prev_ref[...])
    nrow = jnp.where(last, 0.0, next_ref[...])
    prev = jnp.where(grp == 0, prow[None], pltpu.roll(y, NB, 0).reshape(tm // NB, NB, wid)).reshape(tm, wid)
    nxt = jnp.where(grp == tm // NB - 1, nrow[None],
                    pltpu.roll(y, tm - NB, 0).reshape(tm // NB, NB, wid)).reshape(tm, wid)
    ys = y + mu_ref[0:1, :] * (prev - y) + mu_ref[1:2, :] * (nxt - y)

    w_ = RWKV_W
    r = ys[:, 0:w_]
    kr = ys[:, w_:2 * w_]
    vr = ys[:, 2 * w_:3 * w_]
    wd = ys[:, 3 * w_:3 * w_ + 128]
    ad = ys[:, 3 * w_ + 128:3 * w_ + 256]
    gd = ys[:, 3 * w_ + 256:3 * w_ + 512]
    ones = ones_ref[...]

    kk = kr * kkw_ref[...]
    ss = _seg_sum(kk * kk, ones)
    kk = kk / jnp.maximum(jnp.sqrt(ss), 1e-12)
    zw = w0_ref[...] + _dot3(jnp.tanh(wd), w2_ref)
    decay = jnp.exp(-(float(np.exp(-0.5)) * jax.nn.sigmoid(zw)))
    a = jax.nn.sigmoid(a0_ref[...] + _dot3(ad, a2_ref))
    gate_o[...] = _dot3(jax.nn.sigmoid(gd), g2_ref)
    ka = ka_ref[...]

    def emit(o_ref, qa, qb):
        for hh, slab in enumerate(_head_pairs(qa, qb)):
            o_ref[hh] = slab

    emit(p1_o, kk, r)
    ktsum = None
    for d, (p2_o, p3_o) in enumerate(((p2f_o, p3f_o), (p2b_o, p3b_o))):
        a_d = a[:, d * w_:(d + 1) * w_]
        kt = kr * (1.0 + (a_d - 1.0) * ka)
        emit(p2_o, decay[:, d * w_:(d + 1) * w_], kk * a_d)
        emit(p3_o, kt, vr)
        ktsum = kt if ktsum is None else ktsum + kt
    bonus_o[...] = _seg_sum(r * ktsum * rk_ref[...], ones) * vr


def _rwkv_features(rw, params, nct):
    n, wid = rw.shape
    ntile = n // TM
    rb = TM // 8
    nrb = n // 8
    w_ = RWKV_W
    tok = pl.BlockSpec((TM, w_), lambda t: (t, 0))
    pair = pl.BlockSpec((RWKV_HEADS, TM, 128), lambda t: (0, t, 0))

    def full(a):
        return pl.BlockSpec(a.shape, lambda t: (0,) * a.ndim)

    one = jax.ShapeDtypeStruct((n, w_), F32)
    pshape = jax.ShapeDtypeStruct((RWKV_HEADS, n, 128), F32)
    return pl.pallas_call(
        functools.partial(_feat_kernel, nct=nct, ntile=ntile),
        grid=(ntile,),
        in_specs=[pl.BlockSpec((TM, wid), lambda t: (t, 0)),
                  pl.BlockSpec((8, wid), lambda t: (jnp.maximum(t * rb - 1, 0), 0)),
                  pl.BlockSpec((8, wid), lambda t: (jnp.minimum((t + 1) * rb, nrb - 1), 0))]
                 + [full(a) for a in params],
        out_specs=[pair] * 5 + [tok, tok],
        out_shape=[pshape] * 5 + [one, one],
        compiler_params=_cparams(("arbitrary",)),
        name="rwkv_features",
    )(rw, rw, rw, *params)


def _scan_kernel(p1f, p1b, p2f, p2b, p3f, p3b, yf_ref, yb_ref, s_ref, t_ref, y_buf):
    @pl.when(pl.program_id(0) == 0)
    def _():
        s_ref[...] = jnp.zeros_like(s_ref)
        y_buf[...] = jnp.zeros_like(y_buf)

    nkey = s_ref.shape[0]
    nh = p1f.shape[0]
    tc = p1f.shape[1] // NB
    half = NB * nh
    lanes = 2 * half
    kblk = 32
    pairs = ((p1f, p1b), (p2f, p2b), (p3f, p3b))
    kk_q, r_q, w_q, kka_q, kt_q, v_q = (0, 0), (0, nkey), (1, 0), (1, nkey), (2, 0), (2, nkey)

    def rows_of(i):
        return pl.ds(pl.multiple_of(i * NB, NB), NB)

    def relayout(i, dst):
        for p, (f_ref, b_ref) in enumerate(pairs):
            rows = ([f_ref[hh, rows_of(i), :] for hh in range(nh)]
                    + [b_ref[hh, rows_of(tc - 1 - i), :] for hh in range(nh)])
            dst[p] = jnp.concatenate(rows, axis=0).T

    def emit(i):
        y = y_buf[...].reshape(nkey, lanes)
        yt = jnp.concatenate([y, y], axis=0).T
        for hh in range(nh):
            yf_ref[hh, rows_of(i), :] = yt[hh * NB:(hh + 1) * NB]
            yb_ref[hh, rows_of(tc - 1 - i), :] = yt[half + hh * NB:half + (hh + 1) * NB]

    def step(i, cur, nxt):
        def row(q, k):
            return cur[q[0], pl.ds(q[1] + k, 1), :][None]

        sa = s_ref[0] * row(kk_q, 0)
        for k in range(1, nkey):
            sa = sa + s_ref[k] * row(kk_q, k)
        relayout(jnp.minimum(i + 1, tc - 1), nxt)
        emit(jnp.maximum(i - 1, 0))
        v = cur[v_q[0], v_q[1]:v_q[1] + nkey, :].reshape(nkey // 8, 8, lanes)

        def upd_body(kb, y):
            base = pl.multiple_of(kb * kblk, kblk)
            for j in range(kblk):
                k = base + j
                s_new = s_ref[k] * row(w_q, k) + (v * row(kt_q, k) - sa * row(kka_q, k))
                s_ref[k] = s_new
                y = y + s_new * row(r_q, k)
            return y

        y_buf[...] = lax.fori_loop(0, nkey // kblk, upd_body, jnp.zeros((nkey // 8, 8, lanes), F32))

    relayout(0, t_ref.at[0])

    def two_steps(j, carry):
        step(2 * j, t_ref.at[0], t_ref.at[1])
        step(2 * j + 1, t_ref.at[1], t_ref.at[0])
        return carry

    lax.fori_loop(0, tc // 2, two_steps, 0)
    emit(tc - 1)


def _rwkv_scan(p1, p2f, p2b, p3f, p3b, nct):
    nh, n, _ = p1.shape
    nkey = RWKV_N
    ntb = n // TM

    def mirror(g):
        return jnp.where(g < nct, nct - 1 - g, nct + ntb - 1 - g)

    sf = pl.BlockSpec((nh, TM, 128), lambda g: (0, g, 0))
    sb = pl.BlockSpec((nh, TM, 128), lambda g: (0, mirror(g), 0))
    out = jax.ShapeDtypeStruct((nh, n, 128), F32)
    lanes = 2 * NB * nh
    return pl.pallas_call(
        _scan_kernel,
        grid=(ntb,),
        in_specs=[sf, sb, sf, sb, sf, sb],
        out_specs=[sf, sb],
        out_shape=[out, out],
        scratch_shapes=[pltpu.VMEM((nkey, nkey // 8, 8, lanes), F32),
                        pltpu.VMEM((2, 3, 2 * nkey, lanes), F32),
                        pltpu.VMEM((nkey // 8, 8, lanes), F32)],
        compiler_params=_cparams(("arbitrary",)),
        name="rwkv_scan",
    )(p1, p1, p2f, p2b, p3f, p3b)


def _even_out_kernel(o_ref, g_ref, yf_ref, yb_ref, bonus_ref, gate_ref, lnw_ref, lnb_ref, ones_ref, pt_ref,
                     w_ref, h_ref, mod_ref, out_ref):
    tm = h_ref.shape[0]
    o_all = _seg_sum_left(pt_ref[...], o_ref[...].reshape(tm, RET_HEADS * RET_DV))
    parts = []
    for hh in range(RET_HEADS):
        o = o_all[:, hh * RET_DV:(hh + 1) * RET_DV]
        o = o * lax.rsqrt(jnp.mean(o * o, axis=-1, keepdims=True) + EPS)
        gg = g_ref[:, hh * RET_DV:(hh + 1) * RET_DV]
        parts.append(o * (gg * jax.nn.sigmoid(gg)))
    ones = ones_ref[...]
    lo = lax.broadcasted_iota(jnp.int32, (tm, 128), 1) < RWKV_N
    chunks = []
    for c in range(RWKV_HEADS // 2):
        even = yf_ref[2 * c] + yb_ref[2 * c]
        odd = yf_ref[2 * c + 1] + yb_ref[2 * c + 1]
        chunks.append(jnp.where(lo, even, pltpu.roll(odd, RWKV_N, 1)))
    y = jnp.concatenate(chunks, axis=-1)
    mu = _seg_sum(y, ones) * (1.0 / RWKV_N)
    yc = y - mu
    var = _seg_sum(yc * yc, ones) * (1.0 / RWKV_N)
    yn = yc * lax.rsqrt(var + GN_EPS) * lnw_ref[...] + lnb_ref[...]
    parts.append((yn + bonus_ref[...]) * gate_ref[...])
    cat = jnp.concatenate(parts, axis=-1).astype(BF16)
    mix = jnp.dot(cat, w_ref[...], preferred_element_type=F32)
    out_ref[...] = _gated_residual(h_ref[...], mod_ref[2], mix)


def _even_out(o_ret, g, y2, bonus, gate, lnw, lnb, ones, perm_t, w_out, h, mod, sel):
    n, d = h.shape
    w_ = RWKV_W
    tok = pl.BlockSpec((TM, w_), lambda t: (t, 0))
    ysp = pl.BlockSpec((RWKV_HEADS, TM, 128), lambda t: (0, t, 0))
    osp = pl.BlockSpec((NB, TM // NB, o_ret.shape[2]), lambda t: (0, t, 0))

    def full(a):
        return pl.BlockSpec(a.shape, lambda t: (0,) * a.ndim)

    return pl.pallas_call(
        _even_out_kernel,
        grid=(n // TM,),
        in_specs=[osp, tok, ysp, ysp, tok, tok, full(lnw), full(lnb), full(ones), full(perm_t), full(w_out),
                  pl.BlockSpec((TM, d), lambda t: (t, 0)),
                  _mod_spec(d, sel)],
        out_specs=pl.BlockSpec((TM, d), lambda t: (t, 0)),
        out_shape=jax.ShapeDtypeStruct((n, d), F32),
        compiler_params=_cparams(("arbitrary",)),
        name="even_out",
    )(o_ret, g, y2[0], y2[1], bonus, gate, lnw, lnb, ones, perm_t, w_out, h, mod)


def _mla_proj_kernel(a_ref, qg_ref, kvg_ref, wuq_ref, wukv_ref, cos_ref, sin_ref,
                     qn_o, qp_o, kn_o, v_o, kp_o):
    nb, steps, wid = a_ref.shape
    tm = nb * steps
    a = a_ref[...].reshape(tm, wid)
    cos = jnp.broadcast_to(cos_ref[...][None], (nb, steps, QK_ROPE)).reshape(tm, QK_ROPE)
    sin = jnp.broadcast_to(sin_ref[...][None], (nb, steps, QK_ROPE)).reshape(tm, QK_ROPE)
    zpad = jnp.zeros((tm, 128 - QK_ROPE), F32)

    def rms(t, g):
        return (t * lax.rsqrt(jnp.mean(t * t, axis=-1, keepdims=True) + EPS) * g).astype(BF16)

    def pad128(pe):
        return jnp.concatenate([pe, zpad], axis=1).astype(BF16).reshape(nb, steps, 128)

    cq = rms(a[:, :Q_RANK], qg_ref[...])
    ckv = rms(a[:, Q_RANK:Q_RANK + KV_RANK], kvg_ref[...])
    pe0 = Q_RANK + KV_RANK
    kp_o[...] = pad128(a[:, pe0:pe0 + QK_ROPE] * cos + a[:, pe0 + QK_ROPE:pe0 + 2 * QK_ROPE] * sin)
    hw = QK_NOPE + 2 * QK_ROPE
    for hh in range(MLA_HEADS):
        hs = slice(hh * 128, (hh + 1) * 128)
        qh = jnp.dot(cq, wuq_ref[:, hh * hw:(hh + 1) * hw], preferred_element_type=F32)
        qn_o[:, :, hs] = (qh[:, :QK_NOPE] * MLA_SCALE).astype(BF16).reshape(nb, steps, 128)
        qp_o[:, :, hs] = pad128(
            (qh[:, QK_NOPE:QK_NOPE + QK_ROPE] * cos + qh[:, QK_NOPE + QK_ROPE:] * sin) * MLA_SCALE)
        kvh = jnp.dot(ckv, wukv_ref[:, hh * hw:(hh + 1) * hw], preferred_element_type=F32)
        kn_o[:, :, hs] = kvh[:, :QK_NOPE].astype(BF16).reshape(nb, steps, 128)
        v_o[:, :, hs] = kvh[:, QK_NOPE:].astype(BF16).reshape(nb, steps, 128)


def _mla_proj(a, qg, kvg, wuq, wukv, cos, sin, nct):
    nb, s, wid = a.shape
    hw = MLA_HEADS * 128
    steps = TM // NB
    ntile = s // steps
    l = (ntile - nct) * steps

    def full(x):
        return pl.BlockSpec(x.shape, lambda t: (0,) * x.ndim)

    def lat_blk(t):
        return jnp.maximum(t - nct, 0)

    allrows = pl.BlockSpec((nb, steps, hw), lambda t: (0, t, 0))
    latrows = pl.BlockSpec((nb, steps, hw), lambda t: (0, lat_blk(t), 0))
    rope = pl.BlockSpec((steps, QK_ROPE), lambda t: (t, 0))
    return pl.pallas_call(
        _mla_proj_kernel,
        grid=(ntile,),
        in_specs=[pl.BlockSpec((nb, steps, wid), lambda t: (0, t, 0)), full(qg), full(kvg), full(wuq),
                  full(wukv), rope, rope],
        out_specs=[latrows, latrows, allrows, allrows, pl.BlockSpec((nb, steps, 128), lambda t: (0, t, 0))],
        out_shape=[jax.ShapeDtypeStruct((nb, l, hw), BF16),
                   jax.ShapeDtypeStruct((nb, l, hw), BF16),
                   jax.ShapeDtypeStruct((nb, s, hw), BF16),
                   jax.ShapeDtypeStruct((nb, s, hw), BF16),
                   jax.ShapeDtypeStruct((nb, s, 128), BF16)],
        compiler_params=_cparams(("arbitrary",)),
        name="mla_proj",
    )(a, qg, kvg, wuq, wukv, cos, sin)


def _attn_kernel(qn_ref, qp_ref, kn_ref, kp_ref, v_ref, o_ref, *, tq):
    hi = pl.program_id(1)
    dn = (((1,), (1,)), ((), ()))
    hsl = pl.ds(pl.multiple_of(hi * 128, 128), 128)
    k = jnp.concatenate([kn_ref[:, hsl], kp_ref[...]], axis=1)
    v = v_ref[:, hsl]

    def tile(i, carry):
        rows = pl.ds(pl.multiple_of(i * tq, tq), tq)
        q = jnp.concatenate([qn_ref[rows, :], qp_ref[rows, :]], axis=1)
        s = lax.dot_general(q, k, dn, preferred_element_type=F32)
        m = jnp.max(s, axis=-1, keepdims=True)
        p = jnp.exp(s - m)
        l = jnp.sum(p, axis=-1, keepdims=True)
        o = jnp.dot(p.astype(BF16), v, preferred_element_type=F32)
        o_ref[rows, :] = (o / l).astype(o_ref.dtype)
        return carry

    lax.fori_loop(0, qn_ref.shape[0] // tq, tile, 0, unroll=8)


def _attention(qn, qp, kn, kp, v):
    b, l, hw = qn.shape
    s = kn.shape[1]
    hd = MLA_HEADS
    tq = TQ_ATTN if l % TQ_ATTN == 0 else 128
    qspec = pl.BlockSpec((None, l, 128), lambda bi, hi: (bi, 0, hi))
    kvspec = pl.BlockSpec((None, s, hw), lambda bi, hi: (bi, 0, 0))
    return pl.pallas_call(
        functools.partial(_attn_kernel, tq=tq),
        grid=(b, hd),
        in_specs=[qspec, qspec, kvspec,
                  pl.BlockSpec((None, s, 128), lambda bi, hi: (bi, 0, 0)),
                  kvspec],
        out_specs=qspec,
        out_shape=jax.ShapeDtypeStruct((b, l, hw), BF16),
        compiler_params=_cparams(("arbitrary", "arbitrary")),
        name="mla_attention",
    )(qn, qp, kn, kp, v)


def _oproj_kernel(o_ref, pt_ref, w_ref, h_ref, mod_ref, out_ref):
    nb, steps, wid = o_ref.shape
    o = jnp.dot(pt_ref[...], o_ref[...].reshape(nb * steps, wid), preferred_element_type=F32).astype(BF16)
    mix = jnp.dot(o, w_ref[...], preferred_element_type=F32)
    out_ref[...] = _gated_residual(h_ref[...], mod_ref[2], mix)


def _oproj(o, perm_t, w_o, h, mod, hrow, sel):
    nb, l, wid = o.shape
    d = h.shape[1]
    steps = TM // NB
    return pl.pallas_call(
        _oproj_kernel,
        grid=(l // steps,),
        in_specs=[pl.BlockSpec((nb, steps, wid), lambda t: (0, t, 0)),
                  pl.BlockSpec(perm_t.shape, lambda t: (0, 0)),
                  pl.BlockSpec(w_o.shape, lambda t: (0, 0)),
                  pl.BlockSpec((TM, d), lambda t: (hrow(t), 0)),
                  _mod_spec(d, sel)],
        out_specs=pl.BlockSpec((TM, d), lambda t: (t, 0)),
        out_shape=jax.ShapeDtypeStruct((nb * l, d), F32),
        compiler_params=_cparams(("arbitrary",)),
        name="mla_oproj",
    )(o, perm_t, w_o, h, mod)


def _route_kernel(h_ref, g_ref, mod_ref, wr_ref, br_ref, xl_ref, route_ref):
    xl = _norm_mod(h_ref[...], g_ref[...], mod_ref[3], mod_ref[4])
    xl_ref[...] = xl
    logits = jnp.dot(xl, wr_ref[...], precision=HIGHEST, preferred_element_type=F32) + br_ref[...]
    lane_i = lax.broadcasted_iota(jnp.int32, logits.shape, 1)
    lane = lane_i.astype(F32)
    lane_grp = (lane_i >> 3).astype(F32)
    neg = -jnp.inf
    big = 1e6
    gl = jnp.where(jnp.logical_and(lane_i >= N_EXPERTS, lane_i < N_EXPERTS + N_GROUPS), logits, neg)
    gmax = jnp.max(gl, axis=-1, keepdims=True)
    gsum = jnp.sum(jnp.exp(gl - gmax), axis=-1, keepdims=True)
    pg = 1.0 / gsum
    gidx = jnp.min(jnp.where(gl == gmax, lane - N_EXPERTS, big), axis=-1, keepdims=True)
    in_grp = jnp.logical_and(lane_i < N_EXPERTS, lane_grp == gidx)
    el = jnp.where(in_grp, logits, neg)
    emax = jnp.max(el, axis=-1, keepdims=True)
    esum = jnp.sum(jnp.exp(el - emax), axis=-1, keepdims=True)
    i1 = jnp.min(jnp.where(el == emax, lane, big), axis=-1, keepdims=True)
    el2 = jnp.where(lane == i1, neg, el)
    emax2 = jnp.max(el2, axis=-1, keepdims=True)
    i2 = jnp.min(jnp.where(el2 == emax2, lane, big), axis=-1, keepdims=True)
    pe1 = 1.0 / esum
    pe2 = jnp.exp(emax2 - emax) / esum
    den = pe1 + pe2
    w1 = pg * pe1 / den
    w2 = pg * pe2 / den
    route_ref[...] = jnp.where(lane_i == 0, i1,
                               jnp.where(lane_i == 1, i2,
                                         jnp.where(lane_i == 2, w1, jnp.where(lane_i == 3, w2, 0.0))))


def _route(h, g, mod, sel, wr, br):
    n, d = h.shape
    return pl.pallas_call(
        _route_kernel,
        grid=(n // TM,),
        in_specs=[pl.BlockSpec((TM, d), lambda t: (t, 0)),
                  pl.BlockSpec((1, d), lambda t: (0, 0)),
                  _mod_spec(d, sel),
                  pl.BlockSpec(wr.shape, lambda t: (0, 0)),
                  pl.BlockSpec(br.shape, lambda t: (0, 0))],
        out_specs=[pl.BlockSpec((TM, d), lambda t: (t, 0)),
                   pl.BlockSpec((TM, 128), lambda t: (t, 0))],
        out_shape=[jax.ShapeDtypeStruct((n, d), F32), jax.ShapeDtypeStruct((n, 128), F32)],
        compiler_params=_cparams(("arbitrary",)),
        name="moe_route",
    )(h, g.reshape(1, d), mod, wr, br)


def _expert_kernel(te_ref, nt_ref, x_ref, wg_ref, wu_ref, wd_ref, o_ref, wg_s, wu_s, wd_s):
    t = pl.program_id(0)

    @pl.when(t < nt_ref[0])
    def _():
        changed = jnp.logical_or(t == 0, te_ref[t] != te_ref[jnp.maximum(t - 1, 0)])

        @pl.when(changed)
        def _():
            wg_s[...] = wg_ref[...].astype(BF16)
            wu_s[...] = wu_ref[...].astype(BF16)
            wd_s[...] = wd_ref[...].astype(BF16)

        x = x_ref[...].astype(BF16)
        h1 = jnp.dot(x, wg_s[...], preferred_element_type=F32)
        h2 = jnp.dot(x, wu_s[...], preferred_element_type=F32)
        hid = ((h1 * jax.nn.sigmoid(h1)) * h2).astype(BF16)
        o_ref[...] = jnp.dot(hid, wd_s[...], preferred_element_type=F32)

    @pl.when(t >= nt_ref[0])
    def _():
        o_ref[...] = jnp.zeros_like(o_ref)


def _experts(tile_e, ntiles, xs, w_gate, w_up, w_down, layer):
    rows, d = xs.shape
    hid = w_gate.shape[-1]
    grid_spec = pltpu.PrefetchScalarGridSpec(
        num_scalar_prefetch=2,
        grid=(rows // TMOE,),
        in_specs=[pl.BlockSpec((TMOE, d), lambda t, te, nt: (jnp.minimum(t, nt[0] - 1), 0)),
                  pl.BlockSpec((None, None, d, hid), lambda t, te, nt: (layer, te[t], 0, 0)),
                  pl.BlockSpec((None, None, d, hid), lambda t, te, nt: (layer, te[t], 0, 0)),
                  pl.BlockSpec((None, None, hid, d), lambda t, te, nt: (layer, te[t], 0, 0))],
        out_specs=pl.BlockSpec((TMOE, d), lambda t, te, nt: (t, 0)),
        scratch_shapes=[pltpu.VMEM((d, hid), BF16), pltpu.VMEM((d, hid), BF16), pltpu.VMEM((hid, d), BF16)],
    )
    return pl.pallas_call(
        _expert_kernel,
        grid_spec=grid_spec,
        out_shape=jax.ShapeDtypeStruct((rows, d), F32),
        compiler_params=_cparams(("arbitrary",)),
        name="moe_experts",
    )(tile_e, ntiles, xs, w_gate, w_up, w_down)


def _combine_kernel(y1_ref, y2_ref, route_ref, h_ref, mod_ref, fg_ref, p_ref, out_ref, *, final):
    route = route_ref[...]
    y = route[:, 2:3] * y1_ref[...] + route[:, 3:4] * y2_ref[...]
    hn = _gated_residual(h_ref[...], mod_ref[5], y)
    if final:
        hn = hn * lax.rsqrt(jnp.mean(hn * hn, axis=-1, keepdims=True) + EPS) * fg_ref[...]
        out_ref[...] = _seg_sum_left3(p_ref[...], hn).reshape(out_ref.shape)
    else:
        out_ref[...] = hn


def _combine(ycat, route, h, mod, sel, fg, perm, final):
    n, d = h.shape
    tok = pl.BlockSpec((TM, d), lambda t: (t, 0))
    tok2 = pl.BlockSpec((TM, d), lambda t: (t + n // TM, 0))
    if final:
        out_spec = pl.BlockSpec((NB, TM // NB, d), lambda t: (0, t, 0))
        out_shape = jax.ShapeDtypeStruct((NB, n // NB, d), F32)
    else:
        out_spec, out_shape = tok, jax.ShapeDtypeStruct((n, d), F32)
    return pl.pallas_call(
        functools.partial(_combine_kernel, final=final),
        grid=(n // TM,),
        in_specs=[tok, tok2, pl.BlockSpec((TM, 128), lambda t: (t, 0)), tok,
                  _mod_spec(d, sel),
                  pl.BlockSpec((1, d), lambda t: (0, 0)),
                  pl.BlockSpec(perm.shape, lambda t: (0, 0))],
        out_specs=out_spec,
        out_shape=out_shape,
        compiler_params=_cparams(("arbitrary",)),
        name="moe_combine",
    )(ycat, ycat, route, h, mod, fg.reshape(1, d), perm)


def _moe(h, g, mod, sel, wr, br, w_gate, w_up, w_down, layer, fg, perm, final):
    n, d = h.shape
    xl, route = _route(h, g, mod, sel, wr, br)
    e = route[:, :2].astype(jnp.int32).reshape(-1)
    onehot = (e[:, None] == jnp.arange(N_EXPERTS, dtype=jnp.int32)[None, :]).astype(jnp.int32)
    csum = jnp.cumsum(onehot, axis=0)
    counts = csum[-1]
    rank = jnp.sum(csum * onehot, axis=1) - 1
    padded = ((counts + TMOE - 1) // TMOE) * TMOE
    pend = jnp.cumsum(padded)
    pos = (pend - padded)[e] + rank
    rows = 2 * n + N_EXPERTS * TMOE
    hit = jnp.zeros((rows,), jnp.int32).at[pos].add(jnp.arange(2 * n, dtype=jnp.int32) // 2 + 1,
                                                    mode="promise_in_bounds", unique_indices=True)
    src = jnp.where(hit > 0, hit - 1, jnp.arange(rows, dtype=jnp.int32) % n)
    ntile = rows // TMOE
    nvalid = (pend[-1] // TMOE).astype(jnp.int32)
    tstart = jnp.arange(ntile, dtype=jnp.int32) * TMOE
    tile_e = jnp.sum((tstart[:, None] >= pend[None, :]).astype(jnp.int32), axis=1)
    last_e = jnp.sum((((nvalid - 1) * TMOE) >= pend).astype(jnp.int32))
    tile_e = jnp.where(tstart < pend[-1], tile_e, last_e).astype(jnp.int32)
    xs = xl.at[src].get(mode="promise_in_bounds")
    ys = _experts(tile_e, nvalid.reshape(1), xs, w_gate, w_up, w_down, layer)
    ycat = ys.at[pos.reshape(n, 2).T.reshape(-1)].get(mode="promise_in_bounds", unique_indices=True)
    return _combine(ycat, route, h, mod, sel, fg, perm, final)


def _block_ones(width, seg):
    idx = np.arange(width) // seg
    return jnp.asarray((idx[:, None] == idx[None, :]).astype(np.float32))


def _rope_tables(lc, l):
    rows = l // GRID_W
    row = np.repeat(np.arange(rows, dtype=np.float32), GRID_W)
    col = np.tile(np.arange(GRID_W, dtype=np.float32), rows)
    n_freq = QK_ROPE // 4
    inv_freq = jnp.asarray(ROPE_BASE, F32) ** (-jnp.arange(n_freq, dtype=F32) / n_freq)
    ang_r = jnp.asarray(row)[:, None] * inv_freq
    ang_c = jnp.asarray(col)[:, None] * inv_freq
    cos = jnp.concatenate([jnp.cos(ang_r), jnp.cos(ang_r), jnp.cos(ang_c), jnp.cos(ang_c)], axis=-1)
    sin = jnp.concatenate([jnp.sin(ang_r), jnp.sin(ang_r), jnp.sin(ang_c), jnp.sin(ang_c)], axis=-1)
    cos = jnp.concatenate([jnp.ones((lc, QK_ROPE), F32), cos], axis=0)
    sin = jnp.concatenate([jnp.zeros((lc, QK_ROPE), F32), sin], axis=0)
    return cos, sin


def _batch_major_perm(tm):
    steps = tm // NB
    r_out = np.arange(tm)
    r_in = (r_out % steps) * NB + r_out // steps
    return jnp.asarray((r_in[:, None] == np.arange(tm)[None, :]).astype(np.float32)).astype(BF16)


def _rot_cols(pe):
    q = QK_ROPE // 4
    return jnp.concatenate([-pe[..., q:2 * q], pe[..., 0:q], -pe[..., 3 * q:4 * q], pe[..., 2 * q:3 * q]], axis=-1)


def _pad_cols(w, n):
    return jnp.pad(w, ((0, 0), (0, n - w.shape[1])))


def kernel(x, c, ctx, c_ctx, ada_w, ada_b, norm1_g, norm2_g, final_g, ev_w_in, ev_shift_mu, rwkv_w0, rwkv_w2, rwkv_a0, rwkv_a2, rwkv_g2, rwkv_k_k, rwkv_k_a, rwkv_r_k, rwkv_lnx_w, rwkv_lnx_b, ev_w_out, mla_w_in, mla_q_norm_g, mla_w_uq, mla_kv_norm_g, mla_w_ukv, mla_w_o, moe_w_grp, moe_b_grp, moe_w_exp, moe_b_exp, moe_w_gate, moe_w_up, moe_w_down):
    b, l, d = x.shape
    lc = ctx.shape[1]
    s = lc + l
    n = b * s
    steps = TM // NB
    nct = lc // steps
    assert b == NB and lc % steps == 0 and l % steps == 0 and ada_w.shape[0] == 2

    def sel_all(t):
        return (t >= nct).astype(jnp.int32)

    def sel_lat(t):
        return 1

    c_all = jnp.concatenate([c, c_ctx[None], jnp.zeros((MOD_ROWS - b - 1, d), F32)], axis=0)
    ada = _ada_table(c_all, ada_w, ada_b)
    mod = jnp.stack([jnp.broadcast_to(ada[:, b][:, :, None, :], (2, N_MOD, NB, d)),
                     ada[:, :b].transpose(0, 2, 1, 3)], axis=1)
    h = jnp.concatenate([ctx, x], axis=1).transpose(1, 0, 2).reshape(n, d)

    def router_weights(layer):
        wr = jnp.concatenate([moe_w_exp[layer], moe_w_grp[layer]], axis=1)
        br = jnp.concatenate([moe_b_exp[layer], moe_b_grp[layer]])[None]
        return _pad_cols(wr, 128), _pad_cols(br, 128)

    def two_term(w):
        hi = w.astype(BF16)
        return jnp.stack([hi, (w - hi.astype(F32)).astype(BF16)])

    w_ = RWKV_W
    ret_w = 2 * RET_HEADS * RET_DK + RET_HEADS * RET_DV
    w_in = _pad_cols(ev_w_in[0], 3584).astype(BF16)
    perm = _batch_major_perm(TM)
    perm_t = perm.T
    qkv, gret, rw = _norm_proj(h, norm1_g[0], mod[0], sel_all, w_in, perm,
                               (ret_w, RET_HEADS * RET_DV, 2048), (BF16, F32, F32), 1)
    lg = jnp.log1p(-jnp.exp2(-5.0 - jnp.arange(RET_HEADS, dtype=F32)))
    o_ret = _retention(qkv, jnp.broadcast_to(lg[:, None, None], (RET_HEADS, 1, 128)), lc, l)

    ones8 = _block_ones(w_, RWKV_N).astype(BF16)
    zero = jnp.zeros((64, w_), F32)
    w2bd = jnp.concatenate([jnp.concatenate([rwkv_w2[0, 0], zero], axis=1),
                            jnp.concatenate([zero, rwkv_w2[0, 1]], axis=1)], axis=0)
    a2bd = jnp.concatenate([jnp.concatenate([rwkv_a2[0, 0], zero], axis=1),
                            jnp.concatenate([zero, rwkv_a2[0, 1]], axis=1)], axis=0)
    g2p = jnp.pad(rwkv_g2[0], ((0, 256 - rwkv_g2.shape[1]), (0, 0)))
    mu = _pad_cols(ev_shift_mu[0], 2048)
    feat_params = (mu, rwkv_w0[0].reshape(1, 2 * w_), two_term(w2bd), rwkv_a0[0].reshape(1, 2 * w_),
                   two_term(a2bd), two_term(g2p),
                   rwkv_k_k[0][None], rwkv_k_a[0][None], rwkv_r_k[0].reshape(1, w_), ones8)
    p1, p2f, p2b, p3f, p3b, gate, bonus = _rwkv_features(rw, feat_params, nct)
    y2 = _rwkv_scan(p1, p2f, p2b, p3f, p3b, nct)

    h = _even_out(o_ret, gret, y2, bonus, gate, rwkv_lnx_w[0][None], rwkv_lnx_b[0][None], ones8, perm_t,
                  ev_w_out[0].astype(BF16), h, mod[0], sel_all)
    wr, br = router_weights(0)
    h = _moe(h, norm2_g[0], mod[0], sel_all, wr, br, moe_w_gate, moe_w_up, moe_w_down, 0, final_g, perm, False)

    w_in1 = jnp.concatenate([mla_w_in[0], _rot_cols(mla_w_in[0][:, Q_RANK + KV_RANK:])], axis=1).astype(BF16)
    (a1,) = _norm_proj(h, norm1_g[1], mod[1], sel_all, w_in1, perm, (w_in1.shape[1],), (F32,), 1)
    wq = mla_w_uq[0].reshape(Q_RANK, MLA_HEADS, QK_NOPE + QK_ROPE)
    wq = jnp.concatenate([wq, _rot_cols(wq[..., QK_NOPE:])], axis=-1).reshape(Q_RANK, -1).astype(BF16)
    cos, sin = _rope_tables(lc, l)
    qn, qp, kn, vv, kp = _mla_proj(a1, mla_q_norm_g[0][None], mla_kv_norm_g[0][None], wq,
                                   mla_w_ukv[0].astype(BF16), cos, sin, nct)
    o = _attention(qn, qp, kn, kp, vv)
    h = _oproj(o, perm_t, mla_w_o[0].astype(BF16), h, mod[1], lambda t: t + nct, sel_lat)
    wr, br = router_weights(1)
    return _moe(h, norm2_g[1], mod[1], sel_lat, wr, br, moe_w_gate, moe_w_up, moe_w_down, 1, final_g, perm, True)
```
